```python
import math
import jax
import jax.numpy as jnp
from jax import lax
import numpy as np

D_MODEL = 1024
BATCH = 16
SEQ = 256
DEPTH = 2
DEC_BATCH = 8
DEC_SEQ = 2048
PAST_LEN = 512

GRID_W = 64
N_MIXERS = 2
N_LAYERS_A = (DEPTH + 1) // 2
N_LAYERS_B = DEPTH // 2
HD_A = 64
H_A = D_MODEL // (2 * HD_A)
H_B = 16
HD_B = D_MODEL // H_B
NA_KH = 8
NA_KW = 16
ROPE_BASE = 10000.0
N_GROUPS = 4
EXP_PER_GROUP = 8
N_EXPERTS = N_GROUPS * EXP_PER_GROUP
TOP_K = 2
D_EXPERT = 512
MOE_BLOCK = 128
Q_BLOCK = 128
LN_EPS = 1e-5
DEEPNORM_ALPHA = (2.0 * DEPTH) ** 0.25
DEEPNORM_BETA = (8.0 * DEPTH) ** -0.25

kernel_name = 'hybrid_diffattn_natten_hmoe_prefix_step'


def layer_norm(x, g, b):
    xf = x.astype(jnp.float32)
    mu = jnp.mean(xf, -1, keepdims=True)
    var = jnp.mean(jnp.square(xf - mu), -1, keepdims=True)
    y = (xf - mu) * lax.rsqrt(var + LN_EPS)
    return (y * g.astype(jnp.float32) + b.astype(jnp.float32)).astype(x.dtype)


def rms_norm(x, g):
    xf = x.astype(jnp.float32)
    y = xf * lax.rsqrt(jnp.mean(jnp.square(xf), -1, keepdims=True) + LN_EPS)
    return (y * g.astype(jnp.float32)).astype(x.dtype)


def adaln(cond, w_ada_i, b_ada_i):
    m = jax.nn.silu(cond) @ w_ada_i + b_ada_i
    return jnp.split(m[..., None, :], 6, axis=-1)


def mod_in(x, shift, scale):
    return x * (1 + scale) + shift


def axial_rope_tables(n_tok, dim):
    t = jnp.arange(n_tok)
    pos = jnp.stack([t // GRID_W, t % GRID_W], -1).astype(jnp.float32)
    quarter = dim // 4
    inv_freq = ROPE_BASE ** (-jnp.arange(quarter, dtype=jnp.float32) / quarter)
    ang = pos[:, :, None] * inv_freq
    return jnp.cos(ang), jnp.sin(ang)


def apply_axial_rope(x, cos, sin):
    B, T, H, dim = x.shape
    xa = x.reshape(B, T, H, 2, 2, dim // 4).astype(jnp.float32)
    x1, x2 = xa[..., 0, :], xa[..., 1, :]
    cs, sn = cos[None, :, None], sin[None, :, None]
    out = jnp.stack([x1 * cs - x2 * sn, x2 * cs + x1 * sn], axis=-2)
    return out.reshape(B, T, H, dim).astype(x.dtype)


def diff_qkv(h, w_qkv):
    B, T, _ = h.shape
    q, k, v = jnp.split(h @ w_qkv, 3, -1)
    return (q.reshape(B, T, 2 * H_A, HD_A), k.reshape(B, T, 2 * H_A, HD_A),
            v.reshape(B, T, H_A, 2 * HD_A))


def diff_lambda_init(layer_idx):
    return 0.8 - 0.6 * math.exp(-0.3 * layer_idx)


def diff_lambda(lam_p, lam_init):
    lp = lam_p.astype(jnp.float32)
    return jnp.exp(jnp.sum(lp[0] * lp[1])) - jnp.exp(jnp.sum(lp[2] * lp[3])) + lam_init


def diff_attention(q, k, v, lam):
    B, Tq = q.shape[:2]
    nb = Tq // Q_BLOCK
    scale = HD_A ** -0.5
    qb = jnp.moveaxis(q.reshape(B, nb, Q_BLOCK, 2 * H_A, HD_A), 1, 0)

    def one_block(qi):
        s = jnp.einsum('bqhd,bkhd->bhqk', qi, k).astype(jnp.float32) * scale
        p = jax.nn.softmax(s, -1).reshape(B, H_A, 2, Q_BLOCK, -1)
        a = p[:, :, 0] - lam * p[:, :, 1]
        return jnp.einsum('bhqk,bkhe->bqhe', a.astype(v.dtype), v)

    o = lax.map(one_block, qb)
    return jnp.moveaxis(o, 0, 1).reshape(B, Tq, H_A, 2 * HD_A)


def diff_output(o, lam_init, subln_g, w_o):
    B, T = o.shape[:2]
    o = rms_norm(o, subln_g) * (1.0 - lam_init)
    return o.reshape(B, T, D_MODEL) @ w_o


def na_qkv(h, w_qkv):
    B, T, _ = h.shape
    q, k, v = jnp.split(h @ w_qkv, 3, -1)
    return q.reshape(B, T, H_B, HD_B), k.reshape(B, T, H_B, HD_B), v.reshape(B, T, H_B, HD_B)


def dense_attention(q, k, v):
    B, T, H, d = q.shape
    nb = T // Q_BLOCK
    qb = jnp.moveaxis(q.reshape(B, nb, Q_BLOCK, H, d), 1, 0)

    def one_block(qi):
        s = jnp.einsum('bqhd,bkhd->bhqk', qi, k).astype(jnp.float32) * (d ** -0.5)
        p = jax.nn.softmax(s, -1).astype(v.dtype)
        return jnp.einsum('bhqk,bkhd->bqhd', p, v)

    o = lax.map(one_block, qb)
    return jnp.moveaxis(o, 0, 1).reshape(B, T, H, d)


def neighborhood_attention(q, k, v, k_ctx, v_ctx, rpb):
    B, S, H, d = q.shape
    rows = S // GRID_W
    kh = min(NA_KH, rows)
    kw = NA_KW
    scale = d ** -0.5
    qg = q.reshape(B, rows, GRID_W, H, d)
    kg = k.reshape(B, rows, GRID_W, H, d)
    vg = v.reshape(B, rows, GRID_W, H, d)
    col = np.arange(GRID_W)
    col_start = np.clip(col - kw // 2, 0, GRID_W - kw)
    col_idx = col_start[:, None] + np.arange(kw)[None, :]
    col_off = col_idx - col[:, None] + (NA_KW - 1)
    rpb_cols = rpb[:, :, col_off]

    def one_row(r):
        r_start = jnp.clip(r - kh // 2, 0, rows - kh)
        q_r = lax.dynamic_index_in_dim(qg, r, axis=1, keepdims=False)
        k_band = lax.dynamic_slice_in_dim(kg, r_start, kh, axis=1)
        v_band = lax.dynamic_slice_in_dim(vg, r_start, kh, axis=1)
        k_nb = k_band[:, :, col_idx]
        v_nb = v_band[:, :, col_idx]
        row_off = r_start + jnp.arange(kh) - r + (NA_KH - 1)
        bias = jnp.transpose(jnp.take(rpb_cols, row_off, axis=1), (0, 2, 1, 3))
        s_nb = jnp.einsum('bqhd,biqjhd->bhqij', q_r, k_nb).astype(jnp.float32) * scale
        s_nb = s_nb + bias[None].astype(jnp.float32)
        s_ctx = jnp.einsum('bqhd,bphd->bhqp', q_r, k_ctx).astype(jnp.float32) * scale
        s = jnp.concatenate([s_nb.reshape(B, H, GRID_W, kh * kw), s_ctx], -1)
        p = jax.nn.softmax(s, -1).astype(v.dtype)
        p_nb = p[..., :kh * kw].reshape(B, H, GRID_W, kh, kw)
        p_ctx = p[..., kh * kw:]
        return (jnp.einsum('bhqij,biqjhd->bqhd', p_nb, v_nb)
                + jnp.einsum('bhqp,bphd->bqhd', p_ctx, v_ctx))

    o = lax.map(one_row, jnp.arange(rows))
    return jnp.moveaxis(o, 0, 1).reshape(B, S, H, d)


def hier_moe(h, w_rg, b_rg, w_re, b_re, w_gate, w_up, w_down):
    B, T, D = h.shape
    xt = h.reshape(B * T, D)
    n = xt.shape[0]
    g_logits = (xt @ w_rg).astype(jnp.float32) + b_rg.astype(jnp.float32)
    g_sel = jnp.argmax(g_logits, -1)
    p_g = jnp.take_along_axis(jax.nn.softmax(g_logits, -1), g_sel[:, None], -1)
    e_logits = ((xt @ w_re).astype(jnp.float32) + b_re.astype(jnp.float32)).reshape(n, N_GROUPS, EXP_PER_GROUP)
    e_in_group = jnp.take_along_axis(e_logits, g_sel[:, None, None], 1)[:, 0]
    top_v, top_i = lax.top_k(e_in_group, TOP_K)
    gates = jax.nn.softmax(top_v, -1) * p_g
    experts = g_sel[:, None] * EXP_PER_GROUP + top_i
    flat_e = experts.reshape(-1)
    flat_t = jnp.repeat(jnp.arange(n, dtype=jnp.int32), TOP_K)
    flat_w = gates.reshape(-1)
    order = jnp.argsort(flat_e)
    se, st, sw = flat_e[order], flat_t[order], flat_w[order]
    counts = jnp.bincount(flat_e, length=N_EXPERTS)
    starts = jnp.cumsum(counts) - counts
    padded = (counts + MOE_BLOCK - 1) // MOE_BLOCK * MOE_BLOCK
    pend = jnp.cumsum(padded)
    pstart = pend - padded
    dest = pstart[se] + (jnp.arange(n * TOP_K) - starts[se])
    n_blocks = -(-(n * TOP_K + N_EXPERTS * (MOE_BLOCK - 1)) // MOE_BLOCK)
    tok_buf = jnp.zeros(n_blocks * MOE_BLOCK, jnp.int32).at[dest].set(st)
    w_buf = jnp.zeros(n_blocks * MOE_BLOCK, jnp.float32).at[dest].set(sw)
    blk_e = jnp.minimum(jnp.searchsorted(pend, jnp.arange(n_blocks) * MOE_BLOCK, side='right'), N_EXPERTS - 1)

    def one_block(args):
        tok, e, w = args
        xb = xt[tok]
        hb = jax.nn.silu(xb @ w_gate[e]) * (xb @ w_up[e])
        return (hb @ w_down[e]) * w[:, None].astype(h.dtype)

    out = lax.map(one_block, (tok_buf.reshape(n_blocks, MOE_BLOCK), blk_e, w_buf.reshape(n_blocks, MOE_BLOCK)))
    y = jnp.zeros_like(xt).at[tok_buf].add(out.reshape(-1, D))
    return y.reshape(B, T, D)


def setup_inputs(seed: int = 0) -> dict:
    key = jax.random.key(seed)
    ks = jax.random.split(key, 26)
    f32 = jnp.float32
    D = D_MODEL

    def nrm(k, shape, scale):
        return jax.random.normal(k, shape, f32) * scale

    return {
        'x_prompt': nrm(ks[0], (BATCH, SEQ, D), 1.0),
        'x_sample': nrm(ks[1], (DEC_BATCH, DEC_SEQ, D), 1.0),
        'cache_a_k': nrm(ks[2], (DEC_BATCH, N_LAYERS_A, PAST_LEN, 2 * H_A, HD_A), 1.0),
        'cache_a_v': nrm(ks[3], (DEC_BATCH, N_LAYERS_A, PAST_LEN, H_A, 2 * HD_A), 1.0),
        'cache_b_k': nrm(ks[4], (DEC_BATCH, N_LAYERS_B, PAST_LEN, H_B, HD_B), 1.0),
        'cache_b_v': nrm(ks[5], (DEC_BATCH, N_LAYERS_B, PAST_LEN, H_B, HD_B), 1.0),
        'c': nrm(ks[6], (DEC_BATCH, D), 1.0),
        'c_ctx': nrm(ks[7], (D,), 1.0),
        'w_ada': nrm(ks[8], (DEPTH, D, 6 * D), 0.5 * D ** -0.5),
        'b_ada': nrm(ks[9], (DEPTH, 6 * D), 0.02),
        'ln_g': 1.0 + nrm(ks[10], (DEPTH, 2, D), 0.02),
        'ln_b': nrm(ks[11], (DEPTH, 2, D), 0.02),
        'w_qkv_a': nrm(ks[12], (N_LAYERS_A, D, 3 * D), D ** -0.5),
        'lam_a': nrm(ks[13], (N_LAYERS_A, 4, HD_A), 0.1),
        'subln_a': 1.0 + nrm(ks[14], (N_LAYERS_A, 2 * HD_A), 0.02),
        'w_o_a': nrm(ks[15], (N_LAYERS_A, D, D), DEEPNORM_BETA * D ** -0.5),
        'w_qkv_b': nrm(ks[16], (N_LAYERS_B, D, 3 * D), D ** -0.5),
        'rpb_b': nrm(ks[17], (N_LAYERS_B, H_B, 2 * NA_KH - 1, 2 * NA_KW - 1), 0.02),
        'w_o_b': nrm(ks[18], (N_LAYERS_B, D, D), DEEPNORM_BETA * D ** -0.5),
        'w_rg': nrm(ks[19], (DEPTH, D, N_GROUPS), D ** -0.5),
        'b_rg': nrm(ks[20], (DEPTH, N_GROUPS), 0.01),
        'w_re': nrm(ks[21], (DEPTH, D, N_EXPERTS), D ** -0.5),
        'b_re': nrm(ks[22], (DEPTH, N_EXPERTS), 0.01),
        'w_gate': nrm(ks[23], (DEPTH, N_EXPERTS, D, D_EXPERT), D ** -0.5),
        'w_up': nrm(ks[24], (DEPTH, N_EXPERTS, D, D_EXPERT), D ** -0.5),
        'w_down': nrm(ks[25], (DEPTH, N_EXPERTS, D_EXPERT, D), DEEPNORM_BETA * D_EXPERT ** -0.5),
    }


def reference(x_prompt, x_sample, cache_a_k, cache_a_v, cache_b_k, cache_b_v, c, c_ctx,
              w_ada, b_ada, ln_g, ln_b, w_qkv_a, lam_a, subln_a, w_o_a, w_qkv_b, rpb_b, w_o_b,
              w_rg, b_rg, w_re, b_re, w_gate, w_up, w_down):

    def post_norm(i, slot, x, gate, out):
        return layer_norm(DEEPNORM_ALPHA * x + gate * out, ln_g[i, slot], ln_b[i, slot])

    def channel_mixer(i, x, sh, sc, g):
        out = hier_moe(mod_in(x, sh, sc), w_rg[i], b_rg[i], w_re[i], b_re[i],
                       w_gate[i], w_up[i], w_down[i])
        return post_norm(i, 1, x, g, out)

    x = x_prompt
    new_a_k, new_a_v, new_b_k, new_b_v = [], [], [], []
    for i in range(DEPTH):
        sh1, sc1, g1, sh2, sc2, g2 = adaln(c_ctx, w_ada[i], b_ada[i])
        h = mod_in(x, sh1, sc1)
        j = i // N_MIXERS
        if i % N_MIXERS == 0:
            q, k, v = diff_qkv(h, w_qkv_a[j])
            lam_init = diff_lambda_init(i)
            o = diff_attention(q, k, v, diff_lambda(lam_a[j], lam_init))
            out = diff_output(o, lam_init, subln_a[j], w_o_a[j])
            new_a_k.append(k)
            new_a_v.append(v)
        else:
            q, k, v = na_qkv(h, w_qkv_b[j])
            out = dense_attention(q, k, v).reshape(h.shape) @ w_o_b[j]
            new_b_k.append(k)
            new_b_v.append(v)
        x = post_norm(i, 0, x, g1, out)
        x = channel_mixer(i, x, sh2, sc2, g2)
    y_prompt = x

    x = x_sample
    n_lat = x.shape[1]
    cos, sin = axial_rope_tables(n_lat, HD_A)
    for i in range(DEPTH):
        sh1, sc1, g1, sh2, sc2, g2 = adaln(c, w_ada[i], b_ada[i])
        h = mod_in(x, sh1, sc1)
        j = i // N_MIXERS
        if i % N_MIXERS == 0:
            q, k, v = diff_qkv(h, w_qkv_a[j])
            q = apply_axial_rope(q, cos, sin)
            k = apply_axial_rope(k, cos, sin)
            k_all = jnp.concatenate([cache_a_k[:, j].astype(k.dtype), k], axis=1)
            v_all = jnp.concatenate([cache_a_v[:, j].astype(v.dtype), v], axis=1)
            lam_init = diff_lambda_init(i)
            o = diff_attention(q, k_all, v_all, diff_lambda(lam_a[j], lam_init))
            out = diff_output(o, lam_init, subln_a[j], w_o_a[j])
        else:
            q, k, v = na_qkv(h, w_qkv_b[j])
            o = neighborhood_attention(q, k, v, cache_b_k[:, j].astype(k.dtype),
                                       cache_b_v[:, j].astype(v.dtype), rpb_b[j])
            out = o.reshape(h.shape) @ w_o_b[j]
        x = post_norm(i, 0, x, g1, out)
        x = channel_mixer(i, x, sh2, sc2, g2)
    y_sample = x

    return (y_prompt, y_sample, jnp.stack(new_a_k, axis=1), jnp.stack(new_a_v, axis=1),
            jnp.stack(new_b_k, axis=1), jnp.stack(new_b_v, axis=1))
```

```python
import functools
import math

import jax
import jax.numpy as jnp
import numpy as np
from jax import lax
from jax.experimental import pallas as pl
from jax.experimental.pallas import tpu as pltpu

D_MODEL = 1024
DEPTH = 2
GRID_W = 64
HD_A = 64
H_A = D_MODEL // (2 * HD_A)
H_B = 16
HD_B = D_MODEL // H_B
NA_KH = 8
NA_KW = 16
ROPE_BASE = 10000.0
N_GROUPS = 4
EXP_PER_GROUP = 8
N_EXPERTS = N_GROUPS * EXP_PER_GROUP
D_EXPERT = 512
LN_EPS = 1e-5
DEEPNORM_ALPHA = (2.0 * DEPTH) ** 0.25

F32 = jnp.float32
BF16 = jnp.bfloat16

V7X_LANES = 128
V7X_VMEM_BYTES = 64 * 1024 * 1024
VMEM_LIMIT = V7X_VMEM_BYTES * 7 // 8

TM = 256
TQ = 256
MOE_BM = 256
N_PAIRS = D_MODEL // V7X_LANES
NA_QROWS = TQ // GRID_W
NA_WIN_ROWS = NA_QROWS + NA_KH
NA_WIN = NA_WIN_ROWS * GRID_W
MASK_VALUE = -1e30
ROUTE_LANES = 128


def _params(sem):
    return pltpu.CompilerParams(dimension_semantics=sem, vmem_limit_bytes=VMEM_LIMIT)


def _layer_norm(y, g, b):
    mu = jnp.mean(y, -1, keepdims=True)
    d = y - mu
    var = jnp.mean(d * d, -1, keepdims=True)
    return d * lax.rsqrt(var + LN_EPS) * g + b


def _dot_nt(a, b):
    return lax.dot_general(a, b, (((1,), (1,)), ((), ())), preferred_element_type=F32)


def _ada_kernel(c_ref, w_ref, b_ref, o_ref):
    c = c_ref[...]
    a = c / (1.0 + jnp.exp(-c))
    o_ref[...] = jnp.dot(a, w_ref[...], preferred_element_type=F32,
                         precision=lax.Precision.HIGHEST) + b_ref[...]


def _adaln_all(cond, w_ada, b_ada):
    rows = cond.shape[0]
    tn = 1536
    n6 = 6 * D_MODEL
    return pl.pallas_call(
        _ada_kernel,
        out_shape=jax.ShapeDtypeStruct((DEPTH, rows, n6), F32),
        grid=(DEPTH, n6 // tn),
        in_specs=[
            pl.BlockSpec((rows, D_MODEL), lambda l, j: (0, 0)),
            pl.BlockSpec((None, D_MODEL, tn), lambda l, j: (l, 0, j)),
            pl.BlockSpec((None, 1, tn), lambda l, j: (l, 0, j)),
        ],
        out_specs=pl.BlockSpec((None, rows, tn), lambda l, j: (l, 0, j)),
        compiler_params=_params(("arbitrary", "arbitrary")),
        name="adaln",
    )(cond, w_ada, b_ada.reshape(DEPTH, 1, n6))


def _qkv_kernel(*refs, rope):
    if rope:
        x_ref, mod_ref, w_ref, cos_ref, sin_ref, q_ref, k_ref, v_ref = refs
    else:
        x_ref, mod_ref, w_ref, q_ref, k_ref, v_ref = refs
    h = (x_ref[...] * (1.0 + mod_ref[1:2, :]) + mod_ref[0:1, :]).astype(BF16)
    if rope:
        lane = lax.broadcasted_iota(jnp.int32, (TM, V7X_LANES), 1)
        first = (lane & (HD_A // 2 - 1)) < (HD_A // 4)
        cos = cos_ref[...]
        sin = sin_ref[...]
    for idx, o_ref in enumerate((q_ref, k_ref, v_ref)):
        y = jnp.dot(h, w_ref[:, idx * D_MODEL:(idx + 1) * D_MODEL], preferred_element_type=F32)
        if rope and idx < 2:
            for c in range(N_PAIRS):
                sl = slice(c * V7X_LANES, (c + 1) * V7X_LANES)
                yc = y[:, sl]
                partner = jnp.where(first, pltpu.roll(yc, V7X_LANES - HD_A // 4, 1),
                                    pltpu.roll(yc, HD_A // 4, 1))
                o_ref[:, sl] = (yc * cos + partner * sin).astype(o_ref.dtype)
        else:
            o_ref[...] = y.astype(o_ref.dtype)


def _qkv(x, mods, w_bf16, mod_row, kv_dtype, rope_tables=None):
    B, T, _ = x.shape
    n = B * T
    tpb = T // TM
    rope = rope_tables is not None
    in_specs = [
        pl.BlockSpec((TM, D_MODEL), lambda i: (i, 0)),
        pl.BlockSpec((None, 6, D_MODEL), lambda i: (mod_row(i // tpb), 0, 0)),
        pl.BlockSpec((D_MODEL, 3 * D_MODEL), lambda i: (0, 0)),
    ]
    args = [x.reshape(n, D_MODEL), mods, w_bf16]
    if rope:
        in_specs += [pl.BlockSpec((TM, V7X_LANES), lambda i: (i % tpb, 0))] * 2
        args += list(rope_tables)
    q, k, v = pl.pallas_call(
        functools.partial(_qkv_kernel, rope=rope),
        out_shape=(jax.ShapeDtypeStruct((n, D_MODEL), BF16),
                   jax.ShapeDtypeStruct((n, D_MODEL), kv_dtype),
                   jax.ShapeDtypeStruct((n, D_MODEL), kv_dtype)),
        grid=(n // TM,),
        in_specs=in_specs,
        out_specs=(pl.BlockSpec((TM, D_MODEL), lambda i: (i, 0)),) * 3,
        compiler_params=_params(("arbitrary",)),
        name="qkv",
    )(*args)
    shp = (B, T, D_MODEL)
    return q.reshape(shp), k.reshape(shp), v.reshape(shp)


def _rope_tables(n_tok):
    t = np.arange(n_tok)
    pos = jnp.asarray(np.stack([t // GRID_W, t % GRID_W], -1), F32)
    quarter = HD_A // 4
    inv_freq = ROPE_BASE ** (-jnp.arange(quarter, dtype=F32) / quarter)
    ang = pos[:, :, None] * inv_freq
    cos, sin = jnp.cos(ang), jnp.sin(ang)
    cos_h = jnp.broadcast_to(cos[:, :, None, :], (n_tok, 2, 2, quarter)).reshape(n_tok, HD_A)
    sin_h = jnp.stack([-sin, sin], axis=2).reshape(n_tok, HD_A)
    reps = V7X_LANES // HD_A
    return jnp.tile(cos_h, (1, reps)), jnp.tile(sin_h, (1, reps))


def _softmax_parts(scores):
    m = functools.reduce(jnp.maximum, [jnp.max(s, -1, keepdims=True) for s in scores])
    ps = [jnp.exp(s - m) for s in scores]
    l = functools.reduce(lambda a, b: a + b, [jnp.sum(p, -1, keepdims=True) for p in ps])
    return ps, l


def _diff_attn_kernel(*refs, nseg, lam_init):
    lam_ref, subln_ref, q_ref = refs[:3]
    seg_refs = refs[3:3 + 2 * nseg]
    o_ref = refs[3 + 2 * nseg]
    lp = lam_ref[...]
    lam = (jnp.exp(jnp.sum(lp[0:1] * lp[1:2], keepdims=True))
           - jnp.exp(jnp.sum(lp[2:3] * lp[3:4], keepdims=True)) + lam_init)
    lo = lax.broadcasted_iota(jnp.int32, (TQ, V7X_LANES), 1) < HD_A
    scale = HD_A ** -0.5
    for h in range(H_A):
        sl = slice(h * V7X_LANES, (h + 1) * V7X_LANES)
        qc = q_ref[:, sl] * scale
        zero = jnp.zeros_like(qc)
        q0 = jnp.where(lo, qc, zero)
        q1 = jnp.where(lo, zero, qc)
        ks = [seg_refs[2 * j][:, sl].astype(BF16) for j in range(nseg)]
        p0, l0 = _softmax_parts([_dot_nt(q0, kc) for kc in ks])
        p1, l1 = _softmax_parts([_dot_nt(q1, kc) for kc in ks])
        c0 = 1.0 / l0
        c1 = lam / l1
        acc = None
        for j in range(nseg):
            a = (p0[j] * c0 - p1[j] * c1).astype(BF16)
            part = jnp.dot(a, seg_refs[2 * j + 1][:, sl].astype(BF16), preferred_element_type=F32)
            acc = part if acc is None else acc + part
        ms = jnp.mean(acc * acc, -1, keepdims=True)
        o = acc * lax.rsqrt(ms + LN_EPS) * subln_ref[...] * (1.0 - lam_init)
        o_ref[:, sl] = o.astype(BF16)


def _diff_attention(q, segs, lam_p, subln, lam_init):
    B, T, _ = q.shape
    in_specs = [
        pl.BlockSpec((4, HD_A), lambda b, i: (0, 0)),
        pl.BlockSpec((1, 2 * HD_A), lambda b, i: (0, 0)),
        pl.BlockSpec((None, TQ, D_MODEL), lambda b, i: (b, i, 0)),
    ]
    args = [lam_p, subln.reshape(1, 2 * HD_A), q]
    for k, v in segs:
        tk = k.shape[1]
        in_specs += [pl.BlockSpec((None, tk, D_MODEL), lambda b, i: (b, 0, 0))] * 2
        args += [k, v]
    return pl.pallas_call(
        functools.partial(_diff_attn_kernel, nseg=len(segs), lam_init=lam_init),
        out_shape=jax.ShapeDtypeStruct((B, T, D_MODEL), BF16),
        grid=(B, T // TQ),
        in_specs=in_specs,
        out_specs=pl.BlockSpec((None, TQ, D_MODEL), lambda b, i: (b, i, 0)),
        compiler_params=_params(("arbitrary", "arbitrary")),
        name="diff_attn",
    )(*args)


def _pair_heads_attention(qc, score_fns, value_chunks):
    lo = lax.broadcasted_iota(jnp.int32, qc.shape, 1) < HD_B
    zero = jnp.zeros_like(qc)
    outs = []
    for head, qm in enumerate((jnp.where(lo, qc, zero), jnp.where(lo, zero, qc))):
        ps, l = _softmax_parts([fn(qm, head) for fn in score_fns])
        acc = None
        for p, vc in zip(ps, value_chunks):
            part = jnp.dot(p.astype(BF16), vc, preferred_element_type=F32)
            acc = part if acc is None else acc + part
        outs.append(acc / l)
    return jnp.where(lo, outs[0], outs[1])


def _dense_attn_kernel(q_ref, k_ref, v_ref, o_ref):
    scale = HD_B ** -0.5
    for c in range(N_PAIRS):
        sl = slice(c * V7X_LANES, (c + 1) * V7X_LANES)
        kc = k_ref[:, sl].astype(BF16)
        vc = v_ref[:, sl].astype(BF16)
        o = _pair_heads_attention(q_ref[:, sl] * scale, [lambda qm, head: _dot_nt(qm, kc)], [vc])
        o_ref[:, sl] = o.astype(BF16)


def _dense_attention(q, k, v):
    B, T, _ = q.shape
    spec = pl.BlockSpec((None, T, D_MODEL), lambda b: (b, 0, 0))
    return pl.pallas_call(
        _dense_attn_kernel,
        out_shape=jax.ShapeDtypeStruct((B, T, D_MODEL), BF16),
        grid=(B,),
        in_specs=[spec, spec, spec],
        out_specs=spec,
        compiler_params=_params(("arbitrary",)),
        name="dense_attn",
    )(q, k, v)


def _na_window_start(qb, rows):
    return jnp.clip(qb * NA_QROWS - NA_KH // 2, 0, rows - NA_WIN_ROWS) * GRID_W


def _na_attn_kernel(q_ref, k_ref, v_ref, kc_ref, vc_ref, bias_ref, o_ref, *, rows):
    qb = pl.program_id(0)
    ws = pl.multiple_of(_na_window_start(qb, rows), GRID_W)
    kwin = k_ref[pl.ds(ws, NA_WIN), :]
    vwin = v_ref[pl.ds(ws, NA_WIN), :]
    kctx = kc_ref[...].astype(BF16)
    vctx = vc_ref[...].astype(BF16)
    o = _pair_heads_attention(
        q_ref[...] * (HD_B ** -0.5),
        [lambda qm, head: _dot_nt(qm, kwin) + bias_ref[head], lambda qm, head: _dot_nt(qm, kctx)],
        [vwin, vctx])
    o_ref[...] = o.astype(BF16)


def _na_bias_tables(rpb, rows):
    kh = min(NA_KH, rows)
    ql = np.arange(TQ)
    kl = np.arange(NA_WIN)
    tables_r, tables_c, valids = [], [], []
    for r0 in (0, NA_QROWS, rows - NA_QROWS):
        ws = int(np.clip(r0 - NA_KH // 2, 0, rows - NA_WIN_ROWS))
        qrow = (r0 + ql // GRID_W)[:, None]
        qcol = (ql % GRID_W)[:, None]
        krow = (ws + kl // GRID_W)[None, :]
        kcol = (kl % GRID_W)[None, :]
        r_start = np.clip(qrow - kh // 2, 0, rows - kh)
        c_start = np.clip(qcol - NA_KW // 2, 0, GRID_W - NA_KW)
        valid = ((krow >= r_start) & (krow < r_start + kh)
                 & (kcol >= c_start) & (kcol < c_start + NA_KW))
        tables_r.append(np.clip(krow - qrow + NA_KH - 1, 0, 2 * NA_KH - 2) + 0 * kcol)
        tables_c.append(np.clip(kcol - qcol + NA_KW - 1, 0, 2 * NA_KW - 2) + 0 * krow)
        valids.append(valid)
    ro, co, ok = np.stack(tables_r), np.stack(tables_c), np.stack(valids)
    bias = rpb[:, ro, co]
    bias = jnp.where(ok[None], bias, MASK_VALUE)
    return jnp.transpose(bias, (1, 0, 2, 3))


def _neighborhood_attention(q, k, v, k_ctx, v_ctx, rpb):
    B, S, _ = q.shape
    P = k_ctx.shape[1]
    rows = S // GRID_W
    nqb = S // TQ
    assert rows >= NA_WIN_ROWS and nqb >= 3
    bias = _na_bias_tables(rpb, rows)

    def bias_cfg(qb):
        return jnp.where(qb == 0, 0, jnp.where(qb == nqb - 1, 2, 1))

    self_spec = pl.BlockSpec((None, S, V7X_LANES), lambda qb, hp, b: (b, 0, hp))
    ctx_spec = pl.BlockSpec((None, P, V7X_LANES), lambda qb, hp, b: (b, 0, hp))
    q_spec = pl.BlockSpec((None, TQ, V7X_LANES), lambda qb, hp, b: (b, qb, hp))
    return pl.pallas_call(
        functools.partial(_na_attn_kernel, rows=rows),
        out_shape=jax.ShapeDtypeStruct((B, S, D_MODEL), BF16),
        grid=(nqb, N_PAIRS, B),
        in_specs=[q_spec, self_spec, self_spec, ctx_spec, ctx_spec,
                  pl.BlockSpec((None, 2, TQ, NA_WIN), lambda qb, hp, b: (bias_cfg(qb), hp, 0, 0))],
        out_specs=q_spec,
        compiler_params=_params(("arbitrary", "arbitrary", "arbitrary")),
        name="na_attn",
    )(q, k, v, k_ctx, v_ctx, bias)


def _proj_route_kernel(o_ref, w_ref, x_ref, mod_ref, g_ref, b_ref, wr_ref, br_ref,
                       x1_ref, h2_ref, route_ref):
    out = jnp.dot(o_ref[...], w_ref[...], preferred_element_type=F32)
    x1 = _layer_norm(DEEPNORM_ALPHA * x_ref[...] + mod_ref[2:3, :] * out, g_ref[...], b_ref[...])
    x1_ref[...] = x1
    h2 = x1 * (1.0 + mod_ref[4:5, :]) + mod_ref[3:4, :]
    h2_ref[...] = h2
    logits = jnp.dot(h2, wr_ref[...], preferred_element_type=F32,
                     precision=lax.Precision.HIGHEST) + br_ref[...]
    lane = lax.broadcasted_iota(jnp.int32, logits.shape, 1).astype(F32)
    neg = -jnp.inf
    gmask = (lane >= N_EXPERTS) & (lane < N_EXPERTS + N_GROUPS)
    gl = jnp.where(gmask, logits, neg)
    gmax = jnp.max(gl, -1, keepdims=True)
    g_sel = jnp.min(jnp.where(gl == gmax, lane - N_EXPERTS, float(N_GROUPS)), -1, keepdims=True)
    p_g = 1.0 / jnp.sum(jnp.where(gmask, jnp.exp(gl - gmax), 0.0), -1, keepdims=True)
    emask = (lane >= g_sel * EXP_PER_GROUP) & (lane < (g_sel + 1.0) * EXP_PER_GROUP)
    el = jnp.where(emask, logits, neg)
    v1 = jnp.max(el, -1, keepdims=True)
    i1 = jnp.min(jnp.where(el == v1, lane, float(ROUTE_LANES)), -1, keepdims=True)
    el2 = jnp.where(lane == i1, neg, el)
    v2 = jnp.max(el2, -1, keepdims=True)
    i2 = jnp.min(jnp.where(el2 == v2, lane, float(ROUTE_LANES)), -1, keepdims=True)
    e21 = jnp.exp(v2 - v1)
    w1 = p_g / (1.0 + e21)
    w2 = p_g * e21 / (1.0 + e21)
    route = jnp.where(lane == 0.0, i1,
                      jnp.where(lane == 1.0, i2,
                                jnp.where(lane == 2.0, w1, jnp.where(lane == 3.0, w2, 0.0))))
    route_ref[...] = route


def _proj_route(o, w_o_bf16, x, mods, mod_row, ln_g, ln_b, w_router, b_router):
    B, T, _ = x.shape
    n = B * T
    tpb = T // TM
    tile = pl.BlockSpec((TM, D_MODEL), lambda i: (i, 0))
    vec = pl.BlockSpec((1, D_MODEL), lambda i: (0, 0))
    return pl.pallas_call(
        _proj_route_kernel,
        out_shape=(jax.ShapeDtypeStruct((n, D_MODEL), F32),
                   jax.ShapeDtypeStruct((n, D_MODEL), F32),
                   jax.ShapeDtypeStruct((n, ROUTE_LANES), F32)),
        grid=(n // TM,),
        in_specs=[tile,
                  pl.BlockSpec((D_MODEL, D_MODEL), lambda i: (0, 0)),
                  tile,
                  pl.BlockSpec((None, 6, D_MODEL), lambda i: (mod_row(i // tpb), 0, 0)),
                  vec, vec,
                  pl.BlockSpec((D_MODEL, ROUTE_LANES), lambda i: (0, 0)),
                  pl.BlockSpec((1, ROUTE_LANES), lambda i: (0, 0))],
        out_specs=(tile, tile, pl.BlockSpec((TM, ROUTE_LANES), lambda i: (i, 0))),
        compiler_params=_params(("arbitrary",)),
        name="proj_route",
    )(o.reshape(n, D_MODEL), w_o_bf16, x.reshape(n, D_MODEL), mods,
      ln_g.reshape(1, D_MODEL), ln_b.reshape(1, D_MODEL), w_router, b_router)


def _route_tables(route, n):
    flat_e = route[:, 0:2].astype(jnp.int32).reshape(-1)
    onehot = (flat_e[:, None] == jnp.arange(N_EXPERTS, dtype=jnp.int32)[None, :]).astype(jnp.int32)
    csum = jnp.cumsum(onehot, axis=0)
    counts = csum[-1]
    rank = jnp.take_along_axis(csum, flat_e[:, None], axis=1)[:, 0] - 1
    padded = (counts + MOE_BM - 1) // MOE_BM * MOE_BM
    pend = jnp.cumsum(padded)
    dest = (pend - padded)[flat_e] + rank
    n_blocks = -(-(2 * n + N_EXPERTS * (MOE_BM - 1)) // MOE_BM)
    tok = jnp.zeros(n_blocks * MOE_BM, jnp.int32).at[dest].set(
        jnp.arange(2 * n, dtype=jnp.int32) // 2)
    blk_e = jnp.minimum(jnp.searchsorted(pend, jnp.arange(n_blocks, dtype=jnp.int32) * MOE_BM,
                                         side='right'), N_EXPERTS - 1).astype(jnp.int32)
    n_used = (pend[-1:] // MOE_BM).astype(jnp.int32)
    return tok, blk_e, n_used, dest.astype(jnp.int32), n_blocks


def _expert_kernel(blk_e_ref, tok_ref, nused_ref, h_hbm, wg_ref, wu_ref, wd_ref, o_ref,
                   buf, wg_s, wu_s, wd_s, sem):
    i = pl.program_id(0)
    n_used = nused_ref[0]

    def row_copy(src_row, slot, r):
        return pltpu.make_async_copy(h_hbm.at[pl.ds(src_row, 1)], buf.at[slot, pl.ds(r, 1)],
                                     sem.at[slot])

    def issue(blk, slot):
        def body(r, carry):
            row_copy(tok_ref[blk * MOE_BM + r], slot, r).start()
            return carry
        lax.fori_loop(0, MOE_BM, body, 0, unroll=8)

    @pl.when(i == 0)
    def _():
        issue(0, 0)

    @pl.when(i + 1 < n_used)
    def _():
        issue(i + 1, (i + 1) % 2)

    active = i < n_used
    changed = (i == 0) | (blk_e_ref[i] != blk_e_ref[jnp.maximum(i - 1, 0)])

    @pl.when(active & changed)
    def _():
        wg_s[...] = wg_ref[...].astype(BF16)
        wu_s[...] = wu_ref[...].astype(BF16)
        wd_s[...] = wd_ref[...].astype(BF16)

    @pl.when(active)
    def _():
        slot = i % 2
        pltpu.make_async_copy(h_hbm.at[pl.ds(0, MOE_BM)], buf.at[slot], sem.at[slot]).wait()
        xb = buf[slot].astype(BF16)
        g = jnp.dot(xb, wg_s[...], preferred_element_type=F32)
        u = jnp.dot(xb, wu_s[...], preferred_element_type=F32)
        hb = (g / (1.0 + jnp.exp(-g)) * u).astype(BF16)
        o_ref[...] = jnp.dot(hb, wd_s[...], preferred_element_type=F32)

    @pl.when(jnp.logical_not(active))
    def _():
        o_ref[...] = jnp.zeros_like(o_ref)


def _experts(h2, tok, blk_e, n_used, n_blocks, layer, w_gate, w_up, w_down):
    blk_e = blk_e + layer * N_EXPERTS
    w_gate = w_gate.reshape(DEPTH * N_EXPERTS, D_MODEL, D_EXPERT)
    w_up = w_up.reshape(DEPTH * N_EXPERTS, D_MODEL, D_EXPERT)
    w_down = w_down.reshape(DEPTH * N_EXPERTS, D_EXPERT, D_MODEL)
    grid_spec = pltpu.PrefetchScalarGridSpec(
        num_scalar_prefetch=3,
        grid=(n_blocks,),
        in_specs=[
            pl.BlockSpec(memory_space=pl.ANY),
            pl.BlockSpec((None, D_MODEL, D_EXPERT), lambda i, be, tk, nu: (be[i], 0, 0)),
            pl.BlockSpec((None, D_MODEL, D_EXPERT), lambda i, be, tk, nu: (be[i], 0, 0)),
            pl.BlockSpec((None, D_EXPERT, D_MODEL), lambda i, be, tk, nu: (be[i], 0, 0)),
        ],
        out_specs=pl.BlockSpec((MOE_BM, D_MODEL), lambda i, be, tk, nu: (i, 0)),
        scratch_shapes=[
            pltpu.VMEM((2, MOE_BM, D_MODEL), F32),
            pltpu.VMEM((D_MODEL, D_EXPERT), BF16),
            pltpu.VMEM((D_MODEL, D_EXPERT), BF16),
            pltpu.VMEM((D_EXPERT, D_MODEL), BF16),
            pltpu.SemaphoreType.DMA((2,)),
        ],
    )
    return pl.pallas_call(
        _expert_kernel,
        out_shape=jax.ShapeDtypeStruct((n_blocks * MOE_BM, D_MODEL), F32),
        grid_spec=grid_spec,
        compiler_params=_params(("arbitrary",)),
        name="experts",
    )(blk_e, tok, n_used, h2, w_gate, w_up, w_down)


def _combine_kernel(pos_ref, rows_hbm, x1_ref, route_ref, mod_ref, g_ref, b_ref, y_ref, buf, sem):
    i = pl.program_id(0)
    n_tiles = pl.num_programs(0)

    def row_copy(src_row, slot, k, r):
        return pltpu.make_async_copy(rows_hbm.at[pl.ds(src_row, 1)], buf.at[slot, k, pl.ds(r, 1)],
                                     sem.at[slot])

    def issue(tile, slot):
        def body(r, carry):
            base = (tile * TM + r) * 2
            row_copy(pos_ref[base], slot, 0, r).start()
            row_copy(pos_ref[base + 1], slot, 1, r).start()
            return carry
        lax.fori_loop(0, TM, body, 0, unroll=8)

    @pl.when(i == 0)
    def _():
        issue(0, 0)

    @pl.when(i + 1 < n_tiles)
    def _():
        issue(i + 1, (i + 1) % 2)

    slot = i % 2
    for k in range(2):
        pltpu.make_async_copy(rows_hbm.at[pl.ds(0, TM)], buf.at[slot, k], sem.at[slot]).wait()
    route = route_ref[...]
    moe = route[:, 2:3] * buf[slot, 0] + route[:, 3:4] * buf[slot, 1]
    y_ref[...] = _layer_norm(DEEPNORM_ALPHA * x1_ref[...] + mod_ref[5:6, :] * moe,
                             g_ref[...], b_ref[...])


def _combine(rows, pos, x1, route, mods, mod_row, tpb, ln_g, ln_b):
    n = x1.shape[0]
    tile = lambda i, p: (i, 0)
    grid_spec = pltpu.PrefetchScalarGridSpec(
        num_scalar_prefetch=1,
        grid=(n // TM,),
        in_specs=[
            pl.BlockSpec(memory_space=pl.ANY),
            pl.BlockSpec((TM, D_MODEL), tile),
            pl.BlockSpec((TM, ROUTE_LANES), tile),
            pl.BlockSpec((None, 6, D_MODEL), lambda i, p: (mod_row(i // tpb), 0, 0)),
            pl.BlockSpec((1, D_MODEL), lambda i, p: (0, 0)),
            pl.BlockSpec((1, D_MODEL), lambda i, p: (0, 0)),
        ],
        out_specs=pl.BlockSpec((TM, D_MODEL), tile),
        scratch_shapes=[pltpu.VMEM((2, 2, TM, D_MODEL), F32), pltpu.SemaphoreType.DMA((2,))],
    )
    return pl.pallas_call(
        _combine_kernel,
        out_shape=jax.ShapeDtypeStruct((n, D_MODEL), F32),
        grid_spec=grid_spec,
        compiler_params=_params(("arbitrary",)),
        name="combine",
    )(pos, rows, x1, route, mods, ln_g.reshape(1, D_MODEL), ln_b.reshape(1, D_MODEL))


def _diff_lambda_init(layer_idx):
    return 0.8 - 0.6 * math.exp(-0.3 * layer_idx)


def _router_weights(w_rg, b_rg, w_re, b_re):
    pad = ROUTE_LANES - N_EXPERTS - N_GROUPS
    w = jnp.concatenate([w_re, w_rg, jnp.zeros((D_MODEL, pad), F32)], axis=1)
    b = jnp.concatenate([b_re, b_rg, jnp.zeros((pad,), F32)]).reshape(1, ROUTE_LANES)
    return w, b


def kernel(x_prompt, x_sample, cache_a_k, cache_a_v, cache_b_k, cache_b_v, c, c_ctx, w_ada, b_ada, ln_g, ln_b, w_qkv_a, lam_a, subln_a, w_o_a, w_qkv_b, rpb_b, w_o_b, w_rg, b_rg, w_re, b_re, w_gate, w_up, w_down):
    n_mod_rows = 16
    dec_batch = x_sample.shape[0]
    cond = jnp.zeros((n_mod_rows, D_MODEL), F32).at[0].set(c_ctx).at[1:1 + dec_batch].set(c)
    mods_all = _adaln_all(cond, w_ada, b_ada).reshape(DEPTH, n_mod_rows, 6, D_MODEL)
    rope_tables = _rope_tables(x_sample.shape[1])

    def run_group(x, mod_row, latent):
        B, T, _ = x.shape
        n = B * T
        new_kv = []
        for i in range(DEPTH):
            mods = mods_all[i]
            j = i // 2
            if i % 2 == 0:
                w_qkv, w_o = w_qkv_a[j].astype(BF16), w_o_a[j].astype(BF16)
                lam_init = _diff_lambda_init(i)
                if latent:
                    q, k, v = _qkv(x, mods, w_qkv, mod_row, BF16, rope_tables)
                    segs = [(cache_a_k[:, j].reshape(B, -1, D_MODEL),
                             cache_a_v[:, j].reshape(B, -1, D_MODEL)), (k, v)]
                else:
                    q, k, v = _qkv(x, mods, w_qkv, mod_row, F32)
                    segs = [(k, v)]
                    new_kv.append((k, v))
                o = _diff_attention(q, segs, lam_a[j], subln_a[j], lam_init)
            else:
                w_qkv, w_o = w_qkv_b[j].astype(BF16), w_o_b[j].astype(BF16)
                if latent:
                    q, k, v = _qkv(x, mods, w_qkv, mod_row, BF16)
                    o = _neighborhood_attention(q, k, v, cache_b_k[:, j].reshape(B, -1, D_MODEL),
                                                cache_b_v[:, j].reshape(B, -1, D_MODEL), rpb_b[j])
                else:
                    q, k, v = _qkv(x, mods, w_qkv, mod_row, F32)
                    new_kv.append((k, v))
                    o = _dense_attention(q, k, v)
            w_router, b_router = _router_weights(w_rg[i], b_rg[i], w_re[i], b_re[i])
            x1, h2, route = _proj_route(o, w_o, x, mods, mod_row, ln_g[i, 0], ln_b[i, 0],
                                        w_router, b_router)
            tok, blk_e, n_used, dest, n_blocks = _route_tables(route, n)
            rows = _experts(h2, tok, blk_e, n_used, n_blocks, i, w_gate, w_up, w_down)
            x = _combine(rows, dest, x1, route, mods, mod_row, T // TM,
                         ln_g[i, 1], ln_b[i, 1]).reshape(B, T, D_MODEL)
        return x, new_kv

    y_prompt, kv = run_group(x_prompt, lambda b: 0, latent=False)
    y_sample, _ = run_group(x_sample, lambda b: 1 + b, latent=True)

    B, T, _ = x_prompt.shape
    new_a_k = kv[0][0].reshape(B, 1, T, 2 * H_A, HD_A)
    new_a_v = kv[0][1].reshape(B, 1, T, H_A, 2 * HD_A)
    new_b_k = kv[1][0].reshape(B, 1, T, H_B, HD_B)
    new_b_v = kv[1][1].reshape(B, 1, T, H_B, HD_B)
    return y_prompt, y_sample, new_a_k, new_a_v, new_b_k, new_b_v
```

```python
import functools
import math

import jax
import jax.numpy as jnp
import numpy as np
from jax import lax
from jax.experimental import pallas as pl
from jax.experimental.pallas import tpu as pltpu

D_MODEL = 1024
DEPTH = 2
GRID_W = 64
HD_A = 64
H_A = D_MODEL // (2 * HD_A)
H_B = 16
HD_B = D_MODEL // H_B
NA_KH = 8
NA_KW = 16
ROPE_BASE = 10000.0
N_GROUPS = 4
EXP_PER_GROUP = 8
N_EXPERTS = N_GROUPS * EXP_PER_GROUP
D_EXPERT = 512
LN_EPS = 1e-5
DEEPNORM_ALPHA = (2.0 * DEPTH) ** 0.25

F32 = jnp.float32
BF16 = jnp.bfloat16

V7X_LANES = 128
V7X_VMEM_BYTES = 64 * 1024 * 1024
VMEM_LIMIT = V7X_VMEM_BYTES * 7 // 8

TM = 256
TQ = 256
MOE_BM = 256
N_PAIRS = D_MODEL // V7X_LANES
NA_QROWS = TQ // GRID_W
NA_WIN_ROWS = NA_QROWS + NA_KH
NA_WIN = NA_WIN_ROWS * GRID_W
MASK_VALUE = -1e30
ROUTE_LANES = 128


def _params(sem):
    return pltpu.CompilerParams(dimension_semantics=sem, vmem_limit_bytes=VMEM_LIMIT)


def _layer_norm(y, g, b):
    mu = jnp.mean(y, -1, keepdims=True)
    d = y - mu
    var = jnp.mean(d * d, -1, keepdims=True)
    return d * lax.rsqrt(var + LN_EPS) * g + b


def _dot_nt(a, b):
    return lax.dot_general(a, b, (((1,), (1,)), ((), ())), preferred_element_type=F32)


def _ada_kernel(c_ref, w_ref, b_ref, o_ref):
    c = c_ref[...]
    a = c / (1.0 + jnp.exp(-c))
    o_ref[...] = jnp.dot(a, w_ref[...], preferred_element_type=F32,
                         precision=lax.Precision.HIGHEST) + b_ref[...]


def _adaln_all(cond, w_ada, b_ada):
    rows = cond.shape[0]
    tn = 1536
    n6 = 6 * D_MODEL
    return pl.pallas_call(
        _ada_kernel,
        out_shape=jax.ShapeDtypeStruct((DEPTH, rows, n6), F32),
        grid=(DEPTH, n6 // tn),
        in_specs=[
            pl.BlockSpec((rows, D_MODEL), lambda l, j: (0, 0)),
            pl.BlockSpec((None, D_MODEL, tn), lambda l, j: (l, 0, j)),
            pl.BlockSpec((None, 1, tn), lambda l, j: (l, 0, j)),
        ],
        out_specs=pl.BlockSpec((None, rows, tn), lambda l, j: (l, 0, j)),
        compiler_params=_params(("arbitrary", "arbitrary")),
        name="adaln",
    )(cond, w_ada, b_ada.reshape(DEPTH, 1, n6))


def _qkv_kernel(*refs, rope):
    if rope:
        x_ref, mod_ref, w_ref, cos_ref, sin_ref, q_ref, k_ref, v_ref = refs
    else:
        x_ref, mod_ref, w_ref, q_ref, k_ref, v_ref = refs
    h = (x_ref[...] * (1.0 + mod_ref[1:2, :]) + mod_ref[0:1, :]).astype(BF16)
    if rope:
        lane = lax.broadcasted_iota(jnp.int32, (TM, V7X_LANES), 1)
        first = (lane & (HD_A // 2 - 1)) < (HD_A // 4)
        cos = cos_ref[...]
        sin = sin_ref[...]
    for idx, o_ref in enumerate((q_ref, k_ref, v_ref)):
        y = jnp.dot(h, w_ref[:, idx * D_MODEL:(idx + 1) * D_MODEL], preferred_element_type=F32)
        if rope and idx < 2:
            for c in range(N_PAIRS):
                sl = slice(c * V7X_LANES, (c + 1) * V7X_LANES)
                yc = y[:, sl]
                partner = jnp.where(first, pltpu.roll(yc, V7X_LANES - HD_A // 4, 1),
                                    pltpu.roll(yc, HD_A // 4, 1))
                o_ref[:, sl] = (yc * cos + partner * sin).astype(o_ref.dtype)
        else:
            o_ref[...] = y.astype(o_ref.dtype)


def _qkv(x, mods, w_bf16, mod_row, kv_dtype, rope_tables=None):
    B, T, _ = x.shape
    n = B * T
    tpb = T // TM
    rope = rope_tables is not None
    in_specs = [
        pl.BlockSpec((TM, D_MODEL), lambda i: (i, 0)),
        pl.BlockSpec((None, 6, D_MODEL), lambda i: (mod_row(i // tpb), 0, 0)),
        pl.BlockSpec((D_MODEL, 3 * D_MODEL), lambda i: (0, 0)),
    ]
    args = [x.reshape(n, D_MODEL), mods, w_bf16]
    if rope:
        in_specs += [pl.BlockSpec((TM, V7X_LANES), lambda i: (i % tpb, 0))] * 2
        args += list(rope_tables)
    q, k, v = pl.pallas_call(
        functools.partial(_qkv_kernel, rope=rope),
        out_shape=(jax.ShapeDtypeStruct((n, D_MODEL), BF16),
                   jax.ShapeDtypeStruct((n, D_MODEL), kv_dtype),
                   jax.ShapeDtypeStruct((n, D_MODEL), kv_dtype)),
        grid=(n // TM,),
        in_specs=in_specs,
        out_specs=(pl.BlockSpec((TM, D_MODEL), lambda i: (i, 0)),) * 3,
        compiler_params=_params(("arbitrary",)),
        name="qkv",
    )(*args)
    shp = (B, T, D_MODEL)
    return q.reshape(shp), k.reshape(shp), v.reshape(shp)


def _rope_tables(n_tok):
    t = np.arange(n_tok)
    pos = jnp.asarray(np.stack([t // GRID_W, t % GRID_W], -1), F32)
    quarter = HD_A // 4
    inv_freq = ROPE_BASE ** (-jnp.arange(quarter, dtype=F32) / quarter)
    ang = pos[:, :, None] * inv_freq
    cos, sin = jnp.cos(ang), jnp.sin(ang)
    cos_h = jnp.broadcast_to(cos[:, :, None, :], (n_tok, 2, 2, quarter)).reshape(n_tok, HD_A)
    sin_h = jnp.stack([-sin, sin], axis=2).reshape(n_tok, HD_A)
    reps = V7X_LANES // HD_A
    return jnp.tile(cos_h, (1, reps)), jnp.tile(sin_h, (1, reps))


def _softmax_parts(scores):
    m = functools.reduce(jnp.maximum, [jnp.max(s, -1, keepdims=True) for s in scores])
    ps = [jnp.exp(s - m) for s in scores]
    l = functools.reduce(lambda a, b: a + b, [jnp.sum(p, -1, keepdims=True) for p in ps])
    return ps, l


def _diff_attn_kernel(*refs, nseg, lam_init):
    lam_ref, subln_ref, q_ref = refs[:3]
    seg_refs = refs[3:3 + 2 * nseg]
    o_ref = refs[3 + 2 * nseg]
    lp = lam_ref[...]
    lam = (jnp.exp(jnp.sum(lp[0:1] * lp[1:2], keepdims=True))
           - jnp.exp(jnp.sum(lp[2:3] * lp[3:4], keepdims=True)) + lam_init)
    lo = lax.broadcasted_iota(jnp.int32, (TQ, V7X_LANES), 1) < HD_A
    scale = HD_A ** -0.5
    for h in range(H_A):
        sl = slice(h * V7X_LANES, (h + 1) * V7X_LANES)
        qc = q_ref[:, sl] * scale
        zero = jnp.zeros_like(qc)
        q0 = jnp.where(lo, qc, zero)
        q1 = jnp.where(lo, zero, qc)
        ks = [seg_refs[2 * j][:, sl].astype(BF16) for j in range(nseg)]
        p0, l0 = _softmax_parts([_dot_nt(q0, kc) for kc in ks])
        p1, l1 = _softmax_parts([_dot_nt(q1, kc) for kc in ks])
        c0 = 1.0 / l0
        c1 = lam / l1
        acc = None
        for j in range(nseg):
            a = (p0[j] * c0 - p1[j] * c1).astype(BF16)
            part = jnp.dot(a, seg_refs[2 * j + 1][:, sl].astype(BF16), preferred_element_type=F32)
            acc = part if acc is None else acc + part
        ms = jnp.mean(acc * acc, -1, keepdims=True)
        o = acc * lax.rsqrt(ms + LN_EPS) * subln_ref[...] * (1.0 - lam_init)
        o_ref[:, sl] = o.astype(BF16)


def _diff_attention(q, segs, lam_p, subln, lam_init):
    B, T, _ = q.shape
    in_specs = [
        pl.BlockSpec((4, HD_A), lambda b, i: (0, 0)),
        pl.BlockSpec((1, 2 * HD_A), lambda b, i: (0, 0)),
        pl.BlockSpec((None, TQ, D_MODEL), lambda b, i: (b, i, 0)),
    ]
    args = [lam_p, subln.reshape(1, 2 * HD_A), q]
    for k, v in segs:
        tk = k.shape[1]
        in_specs += [pl.BlockSpec((None, tk, D_MODEL), lambda b, i: (b, 0, 0))] * 2
        args += [k, v]
    return pl.pallas_call(
        functools.partial(_diff_attn_kernel, nseg=len(segs), lam_init=lam_init),
        out_shape=jax.ShapeDtypeStruct((B, T, D_MODEL), BF16),
        grid=(B, T // TQ),
        in_specs=in_specs,
        out_specs=pl.BlockSpec((None, TQ, D_MODEL), lambda b, i: (b, i, 0)),
        compiler_params=_params(("arbitrary", "arbitrary")),
        name="diff_attn",
    )(*args)


def _pair_heads_attention(qc, score_fns, value_chunks):
    lo = lax.broadcasted_iota(jnp.int32, qc.shape, 1) < HD_B
    zero = jnp.zeros_like(qc)
    outs = []
    for head, qm in enumerate((jnp.where(lo, qc, zero), jnp.where(lo, zero, qc))):
        ps, l = _softmax_parts([fn(qm, head) for fn in score_fns])
        acc = None
        for p, vc in zip(ps, value_chunks):
            part = jnp.dot(p.astype(BF16), vc, preferred_element_type=F32)
            acc = part if acc is None else acc + part
        outs.append(acc / l)
    return jnp.where(lo, outs[0], outs[1])


def _dense_attn_kernel(q_ref, k_ref, v_ref, o_ref):
    scale = HD_B ** -0.5
    for c in range(N_PAIRS):
        sl = slice(c * V7X_LANES, (c + 1) * V7X_LANES)
        kc = k_ref[:, sl].astype(BF16)
        vc = v_ref[:, sl].astype(BF16)
        o = _pair_heads_attention(q_ref[:, sl] * scale, [lambda qm, head: _dot_nt(qm, kc)], [vc])
        o_ref[:, sl] = o.astype(BF16)


def _dense_attention(q, k, v):
    B, T, _ = q.shape
    spec = pl.BlockSpec((None, T, D_MODEL), lambda b: (b, 0, 0))
    return pl.pallas_call(
        _dense_attn_kernel,
        out_shape=jax.ShapeDtypeStruct((B, T, D_MODEL), BF16),
        grid=(B,),
        in_specs=[spec, spec, spec],
        out_specs=spec,
        compiler_params=_params(("arbitrary",)),
        name="dense_attn",
    )(q, k, v)


def _na_window_start(qb, rows):
    return jnp.clip(qb * NA_QROWS - NA_KH // 2, 0, rows - NA_WIN_ROWS) * GRID_W


def _na_attn_kernel(q_ref, k_ref, v_ref, kc_ref, vc_ref, bias_ref, o_ref, *, rows):
    qb = pl.program_id(0)
    ws = pl.multiple_of(_na_window_start(qb, rows), GRID_W)
    kwin = k_ref[pl.ds(ws, NA_WIN), :]
    vwin = v_ref[pl.ds(ws, NA_WIN), :]
    kctx = kc_ref[...].astype(BF16)
    vctx = vc_ref[...].astype(BF16)
    o = _pair_heads_attention(
        q_ref[...] * (HD_B ** -0.5),
        [lambda qm, head: _dot_nt(qm, kwin) + bias_ref[head], lambda qm, head: _dot_nt(qm, kctx)],
        [vwin, vctx])
    o_ref[...] = o.astype(BF16)


def _na_bias_tables(rpb, rows):
    kh = min(NA_KH, rows)
    n_heads = rpb.shape[0]
    qcol = np.arange(GRID_W)[:, None]
    kcol = np.arange(GRID_W)[None, :]
    c_start = np.clip(qcol - NA_KW // 2, 0, GRID_W - NA_KW)
    col_ok = (kcol >= c_start) & (kcol < c_start + NA_KW)
    col_off = np.clip(kcol - qcol + NA_KW - 1, 0, 2 * NA_KW - 2)
    onehot = (col_off[None] == np.arange(2 * NA_KW - 1)[:, None, None]).astype(np.float32)
    cols = jnp.einsum('hrc,cqk->hqrk', rpb, jnp.asarray(onehot), precision=lax.Precision.HIGHEST)
    cols = jnp.where(col_ok[None, :, None, :], cols, MASK_VALUE)
    tables = []
    for r0 in (0, NA_QROWS, rows - NA_QROWS):
        ws = int(np.clip(r0 - NA_KH // 2, 0, rows - NA_WIN_ROWS))
        blocks = []
        for qrow in range(r0, r0 + NA_QROWS):
            r_start = int(np.clip(qrow - kh // 2, 0, rows - kh))
            off_lo = r_start - qrow + NA_KH - 1
            band = cols[:, :, off_lo:off_lo + kh, :].reshape(n_heads, GRID_W, kh * GRID_W)
            left = (r_start - ws) * GRID_W
            blocks.append(jnp.pad(band, ((0, 0), (0, 0), (left, NA_WIN - kh * GRID_W - left)),
                                  constant_values=MASK_VALUE))
        tables.append(jnp.concatenate(blocks, axis=1))
    return jnp.stack(tables)


def _neighborhood_attention(q, k, v, k_ctx, v_ctx, rpb):
    B, S, _ = q.shape
    P = k_ctx.shape[1]
    rows = S // GRID_W
    nqb = S // TQ
    assert rows >= NA_WIN_ROWS and nqb >= 3
    bias = _na_bias_tables(rpb, rows)

    def bias_cfg(qb):
        return jnp.where(qb == 0, 0, jnp.where(qb == nqb - 1, 2, 1))

    self_spec = pl.BlockSpec((None, S, V7X_LANES), lambda qb, hp, b: (b, 0, hp))
    ctx_spec = pl.BlockSpec((None, P, V7X_LANES), lambda qb, hp, b: (b, 0, hp))
    q_spec = pl.BlockSpec((None, TQ, V7X_LANES), lambda qb, hp, b: (b, qb, hp))
    return pl.pallas_call(
        functools.partial(_na_attn_kernel, rows=rows),
        out_shape=jax.ShapeDtypeStruct((B, S, D_MODEL), BF16),
        grid=(nqb, N_PAIRS, B),
        in_specs=[q_spec, self_spec, self_spec, ctx_spec, ctx_spec,
                  pl.BlockSpec((None, 2, TQ, NA_WIN), lambda qb, hp, b: (bias_cfg(qb), hp, 0, 0))],
        out_specs=q_spec,
        compiler_params=_params(("arbitrary", "arbitrary", "arbitrary")),
        name="na_attn",
    )(q, k, v, k_ctx, v_ctx, bias)


INFO_E, INFO_W, INFO_RANK = 0, 2, 4


def _route(logits):
    lane = lax.broadcasted_iota(jnp.int32, logits.shape, 1).astype(F32)
    neg = -jnp.inf
    gmask = (lane >= N_EXPERTS) & (lane < N_EXPERTS + N_GROUPS)
    gl = jnp.where(gmask, logits, neg)
    gmax = jnp.max(gl, -1, keepdims=True)
    g_sel = jnp.min(jnp.where(gl == gmax, lane - N_EXPERTS, float(N_GROUPS)), -1, keepdims=True)
    p_g = 1.0 / jnp.sum(jnp.where(gmask, jnp.exp(gl - gmax), 0.0), -1, keepdims=True)
    emask = (lane >= g_sel * EXP_PER_GROUP) & (lane < (g_sel + 1.0) * EXP_PER_GROUP)
    el = jnp.where(emask, logits, neg)
    v1 = jnp.max(el, -1, keepdims=True)
    e1 = jnp.min(jnp.where(el == v1, lane, float(ROUTE_LANES)), -1, keepdims=True)
    el2 = jnp.where(lane == e1, neg, el)
    v2 = jnp.max(el2, -1, keepdims=True)
    e2 = jnp.min(jnp.where(el2 == v2, lane, float(ROUTE_LANES)), -1, keepdims=True)
    e21 = jnp.exp(v2 - v1)
    w1 = p_g / (1.0 + e21)
    w2 = p_g * e21 / (1.0 + e21)
    return lane, e1, e2, w1, w2


def _proj_route_kernel(o_ref, w_ref, x_ref, mod_ref, g_ref, b_ref, wr_ref, br_ref,
                       x1_ref, info_ref, cnt_ref, run):
    @pl.when(pl.program_id(0) == 0)
    def _():
        run[...] = jnp.zeros_like(run)

    out = jnp.dot(o_ref[...], w_ref[...], preferred_element_type=F32)
    x1 = _layer_norm(DEEPNORM_ALPHA * x_ref[...] + mod_ref[2:3, :] * out, g_ref[...], b_ref[...])
    x1_ref[...] = x1
    h2 = x1 * (1.0 + mod_ref[4:5, :]) + mod_ref[3:4, :]
    logits = jnp.dot(h2, wr_ref[...], preferred_element_type=F32,
                     precision=lax.Precision.HIGHEST) + br_ref[...]
    lane, e1, e2, w1, w2 = _route(logits)

    oh1 = lane == e1
    oh2 = lane == e2
    oh = jnp.where(oh1 | oh2, 1.0, 0.0)
    t_row = lax.broadcasted_iota(jnp.int32, (TM, TM), 0)
    t_col = lax.broadcasted_iota(jnp.int32, (TM, TM), 1)
    earlier = jnp.where(t_col < t_row, 1.0, 0.0).astype(BF16)
    base = run[0:1, :] + jnp.dot(earlier, oh.astype(BF16), preferred_element_type=F32)
    rank1 = jnp.sum(jnp.where(oh1, base, 0.0), -1, keepdims=True)
    rank2 = jnp.sum(jnp.where(oh2, base, 0.0), -1, keepdims=True)
    run[0:1, :] = run[0:1, :] + jnp.sum(oh, 0, keepdims=True)
    cnt_ref[...] = run[...]

    info = jnp.zeros_like(logits)
    for col, val in ((INFO_E, e1), (INFO_E + 1, e2), (INFO_W, w1), (INFO_W + 1, w2),
                     (INFO_RANK, rank1), (INFO_RANK + 1, rank2)):
        info = jnp.where(lane == float(col), val, info)
    info_ref[...] = info


def _proj_route(o, w_o_bf16, x, mods, mod_row, ln_g, ln_b, w_router, b_router):
    B, T, _ = x.shape
    n = B * T
    tpb = T // TM
    tile = pl.BlockSpec((TM, D_MODEL), lambda i: (i, 0))
    vec = pl.BlockSpec((1, D_MODEL), lambda i: (0, 0))
    return pl.pallas_call(
        _proj_route_kernel,
        out_shape=(jax.ShapeDtypeStruct((n, D_MODEL), F32),
                   jax.ShapeDtypeStruct((n, ROUTE_LANES), F32),
                   jax.ShapeDtypeStruct((8, ROUTE_LANES), F32)),
        grid=(n // TM,),
        in_specs=[tile,
                  pl.BlockSpec((D_MODEL, D_MODEL), lambda i: (0, 0)),
                  tile,
                  pl.BlockSpec((None, 6, D_MODEL), lambda i: (mod_row(i // tpb), 0, 0)),
                  vec, vec,
                  pl.BlockSpec((D_MODEL, ROUTE_LANES), lambda i: (0, 0)),
                  pl.BlockSpec((1, ROUTE_LANES), lambda i: (0, 0))],
        out_specs=(tile,
                   pl.BlockSpec((TM, ROUTE_LANES), lambda i: (i, 0)),
                   pl.BlockSpec((8, ROUTE_LANES), lambda i: (0, 0))),
        scratch_shapes=[pltpu.VMEM((8, ROUTE_LANES), F32)],
        compiler_params=_params(("arbitrary",)),
        name="proj_route",
    )(o.reshape(n, D_MODEL), w_o_bf16, x.reshape(n, D_MODEL), mods,
      ln_g.reshape(1, D_MODEL), ln_b.reshape(1, D_MODEL), w_router, b_router)


def _route_tables(info, counts, n, layer):
    cnt = counts[0, :N_EXPERTS].astype(jnp.int32)
    padded = (cnt + MOE_BM - 1) // MOE_BM * MOE_BM
    pend = jnp.cumsum(padded)
    pstart = pend - padded
    n_blocks = 2 * n // MOE_BM + N_EXPERTS
    n_used = pend[-1:] // MOE_BM
    e = info[:, INFO_E:INFO_E + 2].astype(jnp.int32)
    rank = info[:, INFO_RANK:INFO_RANK + 2].astype(jnp.int32)
    experts = jnp.arange(N_EXPERTS, dtype=jnp.int32)
    pos = jnp.sum(jnp.where(e[..., None] == experts, pstart, 0), -1) + rank
    blk = jnp.minimum(jnp.arange(n_blocks, dtype=jnp.int32), n_used - 1) * MOE_BM
    blk_e = jnp.sum((blk[:, None] >= pend[None, :]).astype(jnp.int32), axis=1) + layer * N_EXPERTS
    meta = jnp.concatenate([pstart + cnt, padded - cnt, n_used]).astype(jnp.int32)
    return pos.reshape(-1).astype(jnp.int32), blk_e.astype(jnp.int32), meta, n_used, n_blocks


def _dispatch_kernel(pos_ref, meta_ref, x1_ref, mod_ref, xs_hbm, hbuf, zbuf, sem, sem_z, *,
                     n_blocks):
    i = pl.program_id(0)
    n_tiles = pl.num_programs(0)
    slot = i % 2

    @pl.when(i == 0)
    def _():
        zbuf[...] = jnp.zeros_like(zbuf)

    def wait_slot(s):
        for _ in range(2):
            pltpu.make_async_copy(hbuf.at[s], xs_hbm.at[pl.ds(0, TM)], sem.at[s]).wait()

    @pl.when(i >= 2)
    def _():
        wait_slot(slot)

    hbuf[slot] = x1_ref[...] * (1.0 + mod_ref[4:5, :]) + mod_ref[3:4, :]

    def scatter_row(r, carry):
        base = (i * TM + r) * 2
        for k in range(2):
            pltpu.make_async_copy(hbuf.at[slot, pl.ds(r, 1)],
                                  xs_hbm.at[pl.ds(pos_ref[base + k], 1)], sem.at[slot]).start()
        return carry
    lax.fori_loop(0, TM, scatter_row, 0, unroll=8)

    @pl.when(i == n_tiles - 1)
    def _():
        def zero_copy(row, size):
            return pltpu.make_async_copy(zbuf.at[pl.ds(0, size)], xs_hbm.at[pl.ds(row, size)], sem_z)

        def pad_expert(e, carry):
            start = meta_ref[e]

            def pad_row(r, c):
                zero_copy(start + r, 1).start()
                return c
            lax.fori_loop(0, meta_ref[N_EXPERTS + e], pad_row, 0)
            return carry
        lax.fori_loop(0, N_EXPERTS, pad_expert, 0)

        def tail_block(blk, carry):
            zero_copy(blk * MOE_BM, MOE_BM).start()
            return carry
        lax.fori_loop(meta_ref[2 * N_EXPERTS], n_blocks, tail_block, 0)

        wait_slot(slot)

        @pl.when(i >= 1)
        def _():
            wait_slot(1 - slot)

        for _ in range(N_EXPERTS):
            zero_copy(0, MOE_BM).wait()


def _dispatch(x1, pos, meta, mods, mod_row, tpb, n_blocks):
    n = x1.shape[0]
    assert n_blocks * MOE_BM - 2 * n == N_EXPERTS * MOE_BM
    grid_spec = pltpu.PrefetchScalarGridSpec(
        num_scalar_prefetch=2,
        grid=(n // TM,),
        in_specs=[
            pl.BlockSpec((TM, D_MODEL), lambda i, p, m: (i, 0)),
            pl.BlockSpec((None, 6, D_MODEL), lambda i, p, m: (mod_row(i // tpb), 0, 0)),
        ],
        out_specs=pl.BlockSpec(memory_space=pl.ANY),
        scratch_shapes=[
            pltpu.VMEM((2, TM, D_MODEL), F32),
            pltpu.VMEM((MOE_BM, D_MODEL), F32),
            pltpu.SemaphoreType.DMA((2,)),
            pltpu.SemaphoreType.DMA(()),
        ],
    )
    return pl.pallas_call(
        functools.partial(_dispatch_kernel, n_blocks=n_blocks),
        out_shape=jax.ShapeDtypeStruct((n_blocks * MOE_BM, D_MODEL), F32),
        grid_spec=grid_spec,
        compiler_params=_params(("arbitrary",)),
        name="dispatch",
    )(pos, meta, x1, mods)


def _expert_kernel(be_ref, nused_ref, x_ref, wg_ref, wu_ref, wd_ref, o_ref, wg_s, wu_s, wd_s):
    i = pl.program_id(0)
    active = i < nused_ref[0]
    changed = (i == 0) | (be_ref[i] != be_ref[jnp.maximum(i - 1, 0)])

    @pl.when(active & changed)
    def _():
        wg_s[...] = wg_ref[...].astype(BF16)
        wu_s[...] = wu_ref[...].astype(BF16)
        wd_s[...] = wd_ref[...].astype(BF16)

    @pl.when(active)
    def _():
        xb = x_ref[...].astype(BF16)
        g = jnp.dot(xb, wg_s[...], preferred_element_type=F32)
        u = jnp.dot(xb, wu_s[...], preferred_element_type=F32)
        hb = (g / (1.0 + jnp.exp(-g)) * u).astype(BF16)
        o_ref[...] = jnp.dot(hb, wd_s[...], preferred_element_type=F32)

    @pl.when(jnp.logical_not(active))
    def _():
        o_ref[...] = jnp.zeros_like(o_ref)


def _experts(xs, blk_e, n_used, n_blocks, w_gate, w_up, w_down):
    w_gate = w_gate.reshape(DEPTH * N_EXPERTS, D_MODEL, D_EXPERT)
    w_up = w_up.reshape(DEPTH * N_EXPERTS, D_MODEL, D_EXPERT)
    w_down = w_down.reshape(DEPTH * N_EXPERTS, D_EXPERT, D_MODEL)
    rows_spec = pl.BlockSpec((MOE_BM, D_MODEL), lambda i, be, nu: (i, 0))
    grid_spec = pltpu.PrefetchScalarGridSpec(
        num_scalar_prefetch=2,
        grid=(n_blocks,),
        in_specs=[
            rows_spec,
            pl.BlockSpec((None, D_MODEL, D_EXPERT), lambda i, be, nu: (be[i], 0, 0)),
            pl.BlockSpec((None, D_MODEL, D_EXPERT), lambda i, be, nu: (be[i], 0, 0)),
            pl.BlockSpec((None, D_EXPERT, D_MODEL), lambda i, be, nu: (be[i], 0, 0)),
        ],
        out_specs=rows_spec,
        scratch_shapes=[
            pltpu.VMEM((D_MODEL, D_EXPERT), BF16),
            pltpu.VMEM((D_MODEL, D_EXPERT), BF16),
            pltpu.VMEM((D_EXPERT, D_MODEL), BF16),
        ],
    )
    return pl.pallas_call(
        _expert_kernel,
        out_shape=jax.ShapeDtypeStruct(xs.shape, F32),
        grid_spec=grid_spec,
        compiler_params=_params(("arbitrary",)),
        name="experts",
    )(blk_e, n_used, xs, w_gate, w_up, w_down)


def _combine_kernel(pos_ref, rows_hbm, x1_ref, route_ref, mod_ref, g_ref, b_ref, y_ref, buf, sem):
    i = pl.program_id(0)
    n_tiles = pl.num_programs(0)

    def row_copy(src_row, slot, k, r):
        return pltpu.make_async_copy(rows_hbm.at[pl.ds(src_row, 1)], buf.at[slot, k, pl.ds(r, 1)],
                                     sem.at[slot])

    def issue(tile, slot):
        def body(r, carry):
            base = (tile * TM + r) * 2
            row_copy(pos_ref[base], slot, 0, r).start()
            row_copy(pos_ref[base + 1], slot, 1, r).start()
            return carry
        lax.fori_loop(0, TM, body, 0, unroll=8)

    @pl.when(i == 0)
    def _():
        issue(0, 0)

    @pl.when(i + 1 < n_tiles)
    def _():
        issue(i + 1, (i + 1) % 2)

    slot = i % 2
    for k in range(2):
        pltpu.make_async_copy(rows_hbm.at[pl.ds(0, TM)], buf.at[slot, k], sem.at[slot]).wait()
    route = route_ref[...]
    moe = (route[:, INFO_W:INFO_W + 1] * buf[slot, 0]
           + route[:, INFO_W + 1:INFO_W + 2] * buf[slot, 1])
    y_ref[...] = _layer_norm(DEEPNORM_ALPHA * x1_ref[...] + mod_ref[5:6, :] * moe,
                             g_ref[...], b_ref[...])


def _combine(rows, pos, x1, route, mods, mod_row, tpb, ln_g, ln_b):
    n = x1.shape[0]
    tile = lambda i, p: (i, 0)
    grid_spec = pltpu.PrefetchScalarGridSpec(
        num_scalar_prefetch=1,
        grid=(n // TM,),
        in_specs=[
            pl.BlockSpec(memory_space=pl.ANY),
            pl.BlockSpec((TM, D_MODEL), tile),
            pl.BlockSpec((TM, ROUTE_LANES), tile),
            pl.BlockSpec((None, 6, D_MODEL), lambda i, p: (mod_row(i // tpb), 0, 0)),
            pl.BlockSpec((1, D_MODEL), lambda i, p: (0, 0)),
            pl.BlockSpec((1, D_MODEL), lambda i, p: (0, 0)),
        ],
        out_specs=pl.BlockSpec((TM, D_MODEL), tile),
        scratch_shapes=[pltpu.VMEM((2, 2, TM, D_MODEL), F32), pltpu.SemaphoreType.DMA((2,))],
    )
    return pl.pallas_call(
        _combine_kernel,
        out_shape=jax.ShapeDtypeStruct((n, D_MODEL), F32),
        grid_spec=grid_spec,
        compiler_params=_params(("arbitrary",)),
        name="combine",
    )(pos, rows, x1, route, mods, ln_g.reshape(1, D_MODEL), ln_b.reshape(1, D_MODEL))


def _diff_lambda_init(layer_idx):
    return 0.8 - 0.6 * math.exp(-0.3 * layer_idx)


def _router_weights(w_rg, b_rg, w_re, b_re):
    pad = ROUTE_LANES - N_EXPERTS - N_GROUPS
    w = jnp.concatenate([w_re, w_rg, jnp.zeros((D_MODEL, pad), F32)], axis=1)
    b = jnp.concatenate([b_re, b_rg, jnp.zeros((pad,), F32)]).reshape(1, ROUTE_LANES)
    return w, b


def kernel(x_prompt, x_sample, cache_a_k, cache_a_v, cache_b_k, cache_b_v, c, c_ctx, w_ada, b_ada, ln_g, ln_b, w_qkv_a, lam_a, subln_a, w_o_a, w_qkv_b, rpb_b, w_o_b, w_rg, b_rg, w_re, b_re, w_gate, w_up, w_down):
    n_mod_rows = 16
    dec_batch = x_sample.shape[0]
    cond = jnp.zeros((n_mod_rows, D_MODEL), F32).at[0].set(c_ctx).at[1:1 + dec_batch].set(c)
    mods_all = _adaln_all(cond, w_ada, b_ada).reshape(DEPTH, n_mod_rows, 6, D_MODEL)
    rope_tables = _rope_tables(x_sample.shape[1])

    def run_group(x, mod_row, latent):
        B, T, _ = x.shape
        n = B * T
        new_kv = []
        for i in range(DEPTH):
            mods = mods_all[i]
            j = i // 2
            if i % 2 == 0:
                w_qkv, w_o = w_qkv_a[j].astype(BF16), w_o_a[j].astype(BF16)
                lam_init = _diff_lambda_init(i)
                if latent:
                    q, k, v = _qkv(x, mods, w_qkv, mod_row, BF16, rope_tables)
                    segs = [(cache_a_k[:, j].reshape(B, -1, D_MODEL),
                             cache_a_v[:, j].reshape(B, -1, D_MODEL)), (k, v)]
                else:
                    q, k, v = _qkv(x, mods, w_qkv, mod_row, F32)
                    segs = [(k, v)]
                    new_kv.append((k, v))
                o = _diff_attention(q, segs, lam_a[j], subln_a[j], lam_init)
            else:
                w_qkv, w_o = w_qkv_b[j].astype(BF16), w_o_b[j].astype(BF16)
                if latent:
                    q, k, v = _qkv(x, mods, w_qkv, mod_row, BF16)
                    o = _neighborhood_attention(q, k, v, cache_b_k[:, j].reshape(B, -1, D_MODEL),
                                                cache_b_v[:, j].reshape(B, -1, D_MODEL), rpb_b[j])
                else:
                    q, k, v = _qkv(x, mods, w_qkv, mod_row, F32)
                    new_kv.append((k, v))
                    o = _dense_attention(q, k, v)
            w_router, b_router = _router_weights(w_rg[i], b_rg[i], w_re[i], b_re[i])
            x1, info, counts = _proj_route(o, w_o, x, mods, mod_row, ln_g[i, 0], ln_b[i, 0],
                                           w_router, b_router)
            pos, blk_e, meta, n_used, n_blocks = _route_tables(info, counts, n, i)
            xs = _dispatch(x1, pos, meta, mods, mod_row, T // TM, n_blocks)
            rows = _experts(xs, blk_e, n_used, n_blocks, w_gate, w_up, w_down)
            x = _combine(rows, pos, x1, info, mods, mod_row, T // TM,
                         ln_g[i, 1], ln_b[i, 1]).reshape(B, T, D_MODEL)
        return x, new_kv

    y_prompt, kv = run_group(x_prompt, lambda b: 0, latent=False)
    y_sample, _ = run_group(x_sample, lambda b: 1 + b, latent=True)

    B, T, _ = x_prompt.shape
    new_a_k = kv[0][0].reshape(B, 1, T, 2 * H_A, HD_A)
    new_a_v = kv[0][1].reshape(B, 1, T, H_A, 2 * HD_A)
    new_b_k = kv[1][0].reshape(B, 1, T, H_B, HD_B)
    new_b_v = kv[1][1].reshape(B, 1, T, H_B, HD_B)
    return y_prompt, y_sample, new_a_k, new_a_v, new_b_k, new_b_v
```

```python
import functools
import math

import jax
import jax.numpy as jnp
import numpy as np
from jax import lax
from jax.experimental import pallas as pl
from jax.experimental.pallas import tpu as pltpu

D_MODEL = 1024
DEPTH = 2
GRID_W = 64
HD_A = 64
H_A = D_MODEL // (2 * HD_A)
H_B = 16
HD_B = D_MODEL // H_B
NA_KH = 8
NA_KW = 16
ROPE_BASE = 10000.0
N_GROUPS = 4
EXP_PER_GROUP = 8
N_EXPERTS = N_GROUPS * EXP_PER_GROUP
D_EXPERT = 512
LN_EPS = 1e-5
DEEPNORM_ALPHA = (2.0 * DEPTH) ** 0.25

F32 = jnp.float32
BF16 = jnp.bfloat16

V7X_LANES = 128
V7X_VMEM_BYTES = 64 * 1024 * 1024
VMEM_LIMIT = V7X_VMEM_BYTES * 7 // 8

TM = 256
TQ = 256
MOE_BM = 256
N_PAIRS = D_MODEL // V7X_LANES
NA_QROWS = TQ // GRID_W
NA_WIN_ROWS = NA_QROWS + NA_KH
NA_WIN = NA_WIN_ROWS * GRID_W
MASK_VALUE = -1e30
ROUTE_LANES = 128


def _params(sem):
    return pltpu.CompilerParams(dimension_semantics=sem, vmem_limit_bytes=VMEM_LIMIT)


def _layer_norm(y, g, b):
    mu = jnp.mean(y, -1, keepdims=True)
    d = y - mu
    var = jnp.mean(d * d, -1, keepdims=True)
    return d * lax.rsqrt(var + LN_EPS) * g + b


def _dot_nt(a, b):
    return lax.dot_general(a, b, (((1,), (1,)), ((), ())), preferred_element_type=F32)


def _ada_kernel(c_ref, w_ref, b_ref, o_ref):
    c = c_ref[...]
    a = c / (1.0 + jnp.exp(-c))
    o_ref[...] = jnp.dot(a, w_ref[...], preferred_element_type=F32,
                         precision=lax.Precision.HIGHEST) + b_ref[...]


def _adaln_all(cond, w_ada, b_ada):
    rows = cond.shape[0]
    tn = 1536
    n6 = 6 * D_MODEL
    return pl.pallas_call(
        _ada_kernel,
        out_shape=jax.ShapeDtypeStruct((DEPTH, rows, n6), F32),
        grid=(DEPTH, n6 // tn),
        in_specs=[
            pl.BlockSpec((rows, D_MODEL), lambda l, j: (0, 0)),
            pl.BlockSpec((None, D_MODEL, tn), lambda l, j: (l, 0, j)),
            pl.BlockSpec((None, 1, tn), lambda l, j: (l, 0, j)),
        ],
        out_specs=pl.BlockSpec((None, rows, tn), lambda l, j: (l, 0, j)),
        compiler_params=_params(("arbitrary", "arbitrary")),
        name="adaln",
    )(cond, w_ada, b_ada.reshape(DEPTH, 1, n6))


def _qkv_kernel(*refs, rope):
    if rope:
        x_ref, mod_ref, w_ref, cos_ref, sin_ref, q_ref, k_ref, v_ref = refs
    else:
        x_ref, mod_ref, w_ref, q_ref, k_ref, v_ref = refs
    h = (x_ref[...] * (1.0 + mod_ref[1:2, :]) + mod_ref[0:1, :]).astype(BF16)
    if rope:
        lane = lax.broadcasted_iota(jnp.int32, (TM, V7X_LANES), 1)
        first = (lane & (HD_A // 2 - 1)) < (HD_A // 4)
        cos = cos_ref[...]
        sin = sin_ref[...]
    for idx, o_ref in enumerate((q_ref, k_ref, v_ref)):
        y = jnp.dot(h, w_ref[:, idx * D_MODEL:(idx + 1) * D_MODEL], preferred_element_type=F32)
        if rope and idx < 2:
            for c in range(N_PAIRS):
                sl = slice(c * V7X_LANES, (c + 1) * V7X_LANES)
                yc = y[:, sl]
                partner = jnp.where(first, pltpu.roll(yc, V7X_LANES - HD_A // 4, 1),
                                    pltpu.roll(yc, HD_A // 4, 1))
                o_ref[:, sl] = (yc * cos + partner * sin).astype(o_ref.dtype)
        else:
            o_ref[...] = y.astype(o_ref.dtype)


def _qkv(x, mods, w_bf16, mod_row, kv_dtype, rope_tables=None):
    B, T, _ = x.shape
    n = B * T
    tpb = T // TM
    rope = rope_tables is not None
    in_specs = [
        pl.BlockSpec((TM, D_MODEL), lambda i: (i, 0)),
        pl.BlockSpec((None, 6, D_MODEL), lambda i: (mod_row(i // tpb), 0, 0)),
        pl.BlockSpec((D_MODEL, 3 * D_MODEL), lambda i: (0, 0)),
    ]
    args = [x.reshape(n, D_MODEL), mods, w_bf16]
    if rope:
        in_specs += [pl.BlockSpec((TM, V7X_LANES), lambda i: (i % tpb, 0))] * 2
        args += list(rope_tables)
    q, k, v = pl.pallas_call(
        functools.partial(_qkv_kernel, rope=rope),
        out_shape=(jax.ShapeDtypeStruct((n, D_MODEL), BF16),
                   jax.ShapeDtypeStruct((n, D_MODEL), kv_dtype),
                   jax.ShapeDtypeStruct((n, D_MODEL), kv_dtype)),
        grid=(n // TM,),
        in_specs=in_specs,
        out_specs=(pl.BlockSpec((TM, D_MODEL), lambda i: (i, 0)),) * 3,
        compiler_params=_params(("arbitrary",)),
        name="qkv",
    )(*args)
    shp = (B, T, D_MODEL)
    return q.reshape(shp), k.reshape(shp), v.reshape(shp)


def _rope_tables(n_tok):
    t = np.arange(n_tok)
    pos = jnp.asarray(np.stack([t // GRID_W, t % GRID_W], -1), F32)
    quarter = HD_A // 4
    inv_freq = ROPE_BASE ** (-jnp.arange(quarter, dtype=F32) / quarter)
    ang = pos[:, :, None] * inv_freq
    cos, sin = jnp.cos(ang), jnp.sin(ang)
    cos_h = jnp.broadcast_to(cos[:, :, None, :], (n_tok, 2, 2, quarter)).reshape(n_tok, HD_A)
    sin_h = jnp.stack([-sin, sin], axis=2).reshape(n_tok, HD_A)
    reps = V7X_LANES // HD_A
    return jnp.tile(cos_h, (1, reps)), jnp.tile(sin_h, (1, reps))


def _pair_attention(qc, key_chunks, value_chunks, biases=None):
    lo = lax.broadcasted_iota(jnp.int32, qc.shape, 1) < V7X_LANES // 2
    zero = jnp.zeros_like(qc)
    qq = jnp.concatenate([jnp.where(lo, qc, zero), jnp.where(lo, zero, qc)], axis=0)
    scores = [_dot_nt(qq, kc) for kc in key_chunks]
    if biases is not None:
        scores = [s if b is None else s + b for s, b in zip(scores, biases)]
    m = functools.reduce(jnp.maximum, [jnp.max(s, -1, keepdims=True) for s in scores])
    acc = None
    for s, vc in zip(scores, value_chunks):
        p = jnp.exp(s - m).astype(BF16)
        v1 = jnp.concatenate([vc, jnp.ones_like(vc)], axis=1)
        part = jnp.dot(p, v1, preferred_element_type=F32)
        acc = part if acc is None else acc + part
    return acc[:, :V7X_LANES] / acc[:, V7X_LANES:]


def _diff_attn_kernel(*refs, nseg, lam_init):
    lam_ref, subln_ref, q_ref = refs[:3]
    seg_refs = refs[3:3 + 2 * nseg]
    o_ref = refs[3 + 2 * nseg]
    lp = lam_ref[...]
    lam = (jnp.exp(jnp.sum(lp[0:1] * lp[1:2], keepdims=True))
           - jnp.exp(jnp.sum(lp[2:3] * lp[3:4], keepdims=True)) + lam_init)
    scale = HD_A ** -0.5
    for h in range(H_A):
        sl = slice(h * V7X_LANES, (h + 1) * V7X_LANES)
        o2 = _pair_attention(q_ref[:, sl] * scale,
                             [seg_refs[2 * j][:, sl].astype(BF16) for j in range(nseg)],
                             [seg_refs[2 * j + 1][:, sl].astype(BF16) for j in range(nseg)])
        acc = o2[:TQ] - lam * o2[TQ:]
        ms = jnp.mean(acc * acc, -1, keepdims=True)
        o = acc * lax.rsqrt(ms + LN_EPS) * subln_ref[...] * (1.0 - lam_init)
        o_ref[:, sl] = o.astype(BF16)


def _diff_attention(q, segs, lam_p, subln, lam_init):
    B, T, _ = q.shape
    in_specs = [
        pl.BlockSpec((4, HD_A), lambda b, i: (0, 0)),
        pl.BlockSpec((1, 2 * HD_A), lambda b, i: (0, 0)),
        pl.BlockSpec((None, TQ, D_MODEL), lambda b, i: (b, i, 0)),
    ]
    args = [lam_p, subln.reshape(1, 2 * HD_A), q]
    for k, v in segs:
        tk = k.shape[1]
        in_specs += [pl.BlockSpec((None, tk, D_MODEL), lambda b, i: (b, 0, 0))] * 2
        args += [k, v]
    return pl.pallas_call(
        functools.partial(_diff_attn_kernel, nseg=len(segs), lam_init=lam_init),
        out_shape=jax.ShapeDtypeStruct((B, T, D_MODEL), BF16),
        grid=(B, T // TQ),
        in_specs=in_specs,
        out_specs=pl.BlockSpec((None, TQ, D_MODEL), lambda b, i: (b, i, 0)),
        compiler_params=_params(("arbitrary", "arbitrary")),
        name="diff_attn",
    )(*args)


def _merge_pair(o2):
    tq = o2.shape[0] // 2
    lo = lax.broadcasted_iota(jnp.int32, (tq, V7X_LANES), 1) < V7X_LANES // 2
    return jnp.where(lo, o2[:tq], o2[tq:])


def _dense_attn_kernel(q_ref, k_ref, v_ref, o_ref):
    scale = HD_B ** -0.5
    for c in range(N_PAIRS):
        sl = slice(c * V7X_LANES, (c + 1) * V7X_LANES)
        o2 = _pair_attention(q_ref[:, sl] * scale, [k_ref[:, sl].astype(BF16)],
                             [v_ref[:, sl].astype(BF16)])
        o_ref[:, sl] = _merge_pair(o2).astype(BF16)


def _dense_attention(q, k, v):
    B, T, _ = q.shape
    spec = pl.BlockSpec((None, T, D_MODEL), lambda b: (b, 0, 0))
    return pl.pallas_call(
        _dense_attn_kernel,
        out_shape=jax.ShapeDtypeStruct((B, T, D_MODEL), BF16),
        grid=(B,),
        in_specs=[spec, spec, spec],
        out_specs=spec,
        compiler_params=_params(("arbitrary",)),
        name="dense_attn",
    )(q, k, v)


NA_WIN_BLOCKS = NA_WIN // TQ


def _na_attn_kernel(*refs):
    q_ref = refs[0]
    k_refs = refs[1:1 + NA_WIN_BLOCKS]
    v_refs = refs[1 + NA_WIN_BLOCKS:1 + 2 * NA_WIN_BLOCKS]
    kc_ref, vc_ref, bias_ref, o_ref = refs[1 + 2 * NA_WIN_BLOCKS:]
    scale = HD_B ** -0.5
    for c in range(N_PAIRS):
        sl = slice(c * V7X_LANES, (c + 1) * V7X_LANES)
        kwin = jnp.concatenate([r[:, sl] for r in k_refs], axis=0)
        vwin = jnp.concatenate([r[:, sl] for r in v_refs], axis=0)
        bias = bias_ref[2 * c:2 * c + 2].reshape(2 * TQ, NA_WIN)
        o2 = _pair_attention(q_ref[:, sl] * scale,
                             [kwin, kc_ref[:, sl].astype(BF16)],
                             [vwin, vc_ref[:, sl].astype(BF16)],
                             biases=[bias, None])
        o_ref[:, sl] = _merge_pair(o2).astype(BF16)


def _na_bias_tables(rpb, rows):
    kh = min(NA_KH, rows)
    n_heads = rpb.shape[0]
    qcol = np.arange(GRID_W)[:, None]
    kcol = np.arange(GRID_W)[None, :]
    c_start = np.clip(qcol - NA_KW // 2, 0, GRID_W - NA_KW)
    col_ok = (kcol >= c_start) & (kcol < c_start + NA_KW)
    col_off = np.clip(kcol - qcol + NA_KW - 1, 0, 2 * NA_KW - 2)
    onehot = (col_off[None] == np.arange(2 * NA_KW - 1)[:, None, None]).astype(np.float32)
    cols = jnp.einsum('hrc,cqk->hqrk', rpb, jnp.asarray(onehot), precision=lax.Precision.HIGHEST)
    cols = jnp.where(col_ok[None, :, None, :], cols, MASK_VALUE)
    tables = []
    for r0 in (0, NA_QROWS, rows - NA_QROWS):
        ws = int(np.clip(r0 - NA_KH // 2, 0, rows - NA_WIN_ROWS))
        blocks = []
        for qrow in range(r0, r0 + NA_QROWS):
            r_start = int(np.clip(qrow - kh // 2, 0, rows - kh))
            off_lo = r_start - qrow + NA_KH - 1
            band = cols[:, :, off_lo:off_lo + kh, :].reshape(n_heads, GRID_W, kh * GRID_W)
            left = (r_start - ws) * GRID_W
            blocks.append(jnp.pad(band, ((0, 0), (0, 0), (left, NA_WIN - kh * GRID_W - left)),
                                  constant_values=MASK_VALUE))
        tables.append(jnp.concatenate(blocks, axis=1))
    return jnp.stack(tables)


def _neighborhood_attention(q, k, v, k_ctx, v_ctx, rpb):
    B, S, _ = q.shape
    P = k_ctx.shape[1]
    rows = S // GRID_W
    nqb = S // TQ
    assert rows >= NA_WIN_ROWS and nqb >= NA_WIN_BLOCKS
    bias = _na_bias_tables(rpb, rows)

    def bias_cfg(qb):
        return jnp.where(qb == 0, 0, jnp.where(qb == nqb - 1, 2, 1))

    def win_spec(j):
        return pl.BlockSpec((None, TQ, D_MODEL),
                            lambda qb, b: (b, jnp.clip(qb - 1, 0, nqb - NA_WIN_BLOCKS) + j, 0))

    win_specs = [win_spec(j) for j in range(NA_WIN_BLOCKS)]
    ctx_spec = pl.BlockSpec((None, P, D_MODEL), lambda qb, b: (b, 0, 0))
    q_spec = pl.BlockSpec((None, TQ, D_MODEL), lambda qb, b: (b, qb, 0))
    bias_spec = pl.BlockSpec((None, H_B, TQ, NA_WIN), lambda qb, b: (bias_cfg(qb), 0, 0, 0),
                             pipeline_mode=pl.Buffered(1))
    return pl.pallas_call(
        _na_attn_kernel,
        out_shape=jax.ShapeDtypeStruct((B, S, D_MODEL), BF16),
        grid=(nqb, B),
        in_specs=[q_spec] + win_specs + win_specs + [ctx_spec, ctx_spec, bias_spec],
        out_specs=q_spec,
        compiler_params=_params(("arbitrary", "arbitrary")),
        name="na_attn",
    )(q, *([k] * NA_WIN_BLOCKS), *([v] * NA_WIN_BLOCKS), k_ctx, v_ctx, bias)


INFO_E, INFO_W, INFO_RANK = 0, 2, 4


def _route(logits):
    lane = lax.broadcasted_iota(jnp.int32, logits.shape, 1).astype(F32)
    neg = -jnp.inf
    gmask = (lane >= N_EXPERTS) & (lane < N_EXPERTS + N_GROUPS)
    gl = jnp.where(gmask, logits, neg)
    gmax = jnp.max(gl, -1, keepdims=True)
    g_sel = jnp.min(jnp.where(gl == gmax, lane - N_EXPERTS, float(N_GROUPS)), -1, keepdims=True)
    p_g = 1.0 / jnp.sum(jnp.where(gmask, jnp.exp(gl - gmax), 0.0), -1, keepdims=True)
    emask = (lane >= g_sel * EXP_PER_GROUP) & (lane < (g_sel + 1.0) * EXP_PER_GROUP)
    el = jnp.where(emask, logits, neg)
    v1 = jnp.max(el, -1, keepdims=True)
    e1 = jnp.min(jnp.where(el == v1, lane, float(ROUTE_LANES)), -1, keepdims=True)
    el2 = jnp.where(lane == e1, neg, el)
    v2 = jnp.max(el2, -1, keepdims=True)
    e2 = jnp.min(jnp.where(el2 == v2, lane, float(ROUTE_LANES)), -1, keepdims=True)
    e21 = jnp.exp(v2 - v1)
    w1 = p_g / (1.0 + e21)
    w2 = p_g * e21 / (1.0 + e21)
    return lane, e1, e2, w1, w2


def _proj_route_kernel(o_ref, w_ref, x_ref, mod_ref, g_ref, b_ref, wr_ref, br_ref,
                       x1_ref, info_ref, cnt_ref, run):
    @pl.when(pl.program_id(0) == 0)
    def _():
        run[...] = jnp.zeros_like(run)

    out = jnp.dot(o_ref[...], w_ref[...], preferred_element_type=F32)
    x1 = _layer_norm(DEEPNORM_ALPHA * x_ref[...] + mod_ref[2:3, :] * out, g_ref[...], b_ref[...])
    x1_ref[...] = x1
    h2 = x1 * (1.0 + mod_ref[4:5, :]) + mod_ref[3:4, :]
    logits = jnp.dot(h2.astype(BF16), wr_ref[...], preferred_element_type=F32) + br_ref[...]
    lane, e1, e2, w1, w2 = _route(logits)

    oh1 = lane == e1
    oh2 = lane == e2
    oh = jnp.where(oh1 | oh2, 1.0, 0.0)
    t_row = lax.broadcasted_iota(jnp.int32, (TM, TM), 0)
    t_col = lax.broadcasted_iota(jnp.int32, (TM, TM), 1)
    earlier = jnp.where(t_col < t_row, 1.0, 0.0).astype(BF16)
    base = run[0:1, :] + jnp.dot(earlier, oh.astype(BF16), preferred_element_type=F32)
    rank1 = jnp.sum(jnp.where(oh1, base, 0.0), -1, keepdims=True)
    rank2 = jnp.sum(jnp.where(oh2, base, 0.0), -1, keepdims=True)
    run[0:1, :] = run[0:1, :] + jnp.sum(oh, 0, keepdims=True)
    cnt_ref[...] = run[...]

    info = jnp.zeros_like(logits)
    for col, val in ((INFO_E, e1), (INFO_E + 1, e2), (INFO_W, w1), (INFO_W + 1, w2),
                     (INFO_RANK, rank1), (INFO_RANK + 1, rank2)):
        info = jnp.where(lane == float(col), val, info)
    info_ref[...] = info


def _proj_route(o, w_o_bf16, x, mods, mod_row, ln_g, ln_b, w_router, b_router):
    B, T, _ = x.shape
    n = B * T
    tpb = T // TM
    tile = pl.BlockSpec((TM, D_MODEL), lambda i: (i, 0))
    vec = pl.BlockSpec((1, D_MODEL), lambda i: (0, 0))
    return pl.pallas_call(
        _proj_route_kernel,
        out_shape=(jax.ShapeDtypeStruct((n, D_MODEL), F32),
                   jax.ShapeDtypeStruct((n, ROUTE_LANES), F32),
                   jax.ShapeDtypeStruct((8, ROUTE_LANES), F32)),
        grid=(n // TM,),
        in_specs=[tile,
                  pl.BlockSpec((D_MODEL, D_MODEL), lambda i: (0, 0)),
                  tile,
                  pl.BlockSpec((None, 6, D_MODEL), lambda i: (mod_row(i // tpb), 0, 0)),
                  vec, vec,
                  pl.BlockSpec((D_MODEL, ROUTE_LANES), lambda i: (0, 0)),
                  pl.BlockSpec((1, ROUTE_LANES), lambda i: (0, 0))],
        out_specs=(tile,
                   pl.BlockSpec((TM, ROUTE_LANES), lambda i: (i, 0)),
                   pl.BlockSpec((8, ROUTE_LANES), lambda i: (0, 0))),
        scratch_shapes=[pltpu.VMEM((8, ROUTE_LANES), F32)],
        compiler_params=_params(("arbitrary",)),
        name="proj_route",
    )(o.reshape(n, D_MODEL), w_o_bf16, x.reshape(n, D_MODEL), mods,
      ln_g.reshape(1, D_MODEL), ln_b.reshape(1, D_MODEL), w_router, b_router)


def _route_tables(info, counts, n, layer):
    cnt = counts[0, :N_EXPERTS].astype(jnp.int32)
    padded = (cnt + MOE_BM - 1) // MOE_BM * MOE_BM
    pend = jnp.cumsum(padded)
    pstart = pend - padded
    n_blocks = 2 * n // MOE_BM + N_EXPERTS
    n_used = pend[-1:] // MOE_BM
    e = info[:, INFO_E:INFO_E + 2].astype(jnp.int32)
    rank = info[:, INFO_RANK:INFO_RANK + 2].astype(jnp.int32)
    experts = jnp.arange(N_EXPERTS, dtype=jnp.int32)
    pos = jnp.sum(jnp.where(e[..., None] == experts, pstart, 0), -1) + rank
    blk = jnp.minimum(jnp.arange(n_blocks, dtype=jnp.int32), n_used - 1) * MOE_BM
    blk_e = jnp.sum((blk[:, None] >= pend[None, :]).astype(jnp.int32), axis=1) + layer * N_EXPERTS
    meta = jnp.concatenate([pstart + cnt, padded - cnt, n_used]).astype(jnp.int32)
    return pos.reshape(-1).astype(jnp.int32), blk_e.astype(jnp.int32), meta, n_used, n_blocks


def _dispatch_kernel(pos_ref, meta_ref, x1_ref, mod_ref, xs_hbm, hbuf, zbuf, sem, sem_z, *,
                     n_blocks):
    i = pl.program_id(0)
    n_tiles = pl.num_programs(0)
    slot = i % 2

    @pl.when(i == 0)
    def _():
        zbuf[...] = jnp.zeros_like(zbuf)

    def wait_slot(s):
        for _ in range(2):
            pltpu.make_async_copy(hbuf.at[s], xs_hbm.at[pl.ds(0, TM)], sem.at[s]).wait()

    @pl.when(i >= 2)
    def _():
        wait_slot(slot)

    hbuf[slot] = x1_ref[...] * (1.0 + mod_ref[4:5, :]) + mod_ref[3:4, :]

    def scatter_row(r, carry):
        base = (i * TM + r) * 2
        for k in range(2):
            pltpu.make_async_copy(hbuf.at[slot, pl.ds(r, 1)],
                                  xs_hbm.at[pl.ds(pos_ref[base + k], 1)], sem.at[slot]).start()
        return carry
    lax.fori_loop(0, TM, scatter_row, 0, unroll=8)

    @pl.when(i == n_tiles - 1)
    def _():
        def zero_copy(row, size):
            return pltpu.make_async_copy(zbuf.at[pl.ds(0, size)], xs_hbm.at[pl.ds(row, size)], sem_z)

        def pad_expert(e, carry):
            start = meta_ref[e]

            def pad_row(r, c):
                zero_copy(start + r, 1).start()
                return c
            lax.fori_loop(0, meta_ref[N_EXPERTS + e], pad_row, 0)
            return carry
        lax.fori_loop(0, N_EXPERTS, pad_expert, 0)

        def tail_block(blk, carry):
            zero_copy(blk * MOE_BM, MOE_BM).start()
            return carry
        lax.fori_loop(meta_ref[2 * N_EXPERTS], n_blocks, tail_block, 0)

        wait_slot(slot)

        @pl.when(i >= 1)
        def _():
            wait_slot(1 - slot)

        for _ in range(N_EXPERTS):
            zero_copy(0, MOE_BM).wait()


def _dispatch(x1, pos, meta, mods, mod_row, tpb, n_blocks):
    n = x1.shape[0]
    assert n_blocks * MOE_BM - 2 * n == N_EXPERTS * MOE_BM
    grid_spec = pltpu.PrefetchScalarGridSpec(
        num_scalar_prefetch=2,
        grid=(n // TM,),
        in_specs=[
            pl.BlockSpec((TM, D_MODEL), lambda i, p, m: (i, 0)),
            pl.BlockSpec((None, 6, D_MODEL), lambda i, p, m: (mod_row(i // tpb), 0, 0)),
        ],
        out_specs=pl.BlockSpec(memory_space=pl.ANY),
        scratch_shapes=[
            pltpu.VMEM((2, TM, D_MODEL), F32),
            pltpu.VMEM((MOE_BM, D_MODEL), F32),
            pltpu.SemaphoreType.DMA((2,)),
            pltpu.SemaphoreType.DMA(()),
        ],
    )
    return pl.pallas_call(
        functools.partial(_dispatch_kernel, n_blocks=n_blocks),
        out_shape=jax.ShapeDtypeStruct((n_blocks * MOE_BM, D_MODEL), F32),
        grid_spec=grid_spec,
        compiler_params=_params(("arbitrary",)),
        name="dispatch",
    )(pos, meta, x1, mods)


def _expert_kernel(be_ref, nused_ref, x_ref, wg_ref, wu_ref, wd_ref, o_ref, wg_s, wu_s, wd_s):
    i = pl.program_id(0)
    active = i < nused_ref[0]
    changed = (i == 0) | (be_ref[i] != be_ref[jnp.maximum(i - 1, 0)])

    @pl.when(active & changed)
    def _():
        wg_s[...] = wg_ref[...].astype(BF16)
        wu_s[...] = wu_ref[...].astype(BF16)
        wd_s[...] = wd_ref[...].astype(BF16)

    @pl.when(active)
    def _():
        xb = x_ref[...].astype(BF16)
        g = jnp.dot(xb, wg_s[...], preferred_element_type=F32)
        u = jnp.dot(xb, wu_s[...], preferred_element_type=F32)
        hb = (g / (1.0 + jnp.exp(-g)) * u).astype(BF16)
        o_ref[...] = jnp.dot(hb, wd_s[...], preferred_element_type=F32)

    @pl.when(jnp.logical_not(active))
    def _():
        o_ref[...] = jnp.zeros_like(o_ref)


def _experts(xs, blk_e, n_used, n_blocks, w_gate, w_up, w_down):
    w_gate = w_gate.reshape(DEPTH * N_EXPERTS, D_MODEL, D_EXPERT)
    w_up = w_up.reshape(DEPTH * N_EXPERTS, D_MODEL, D_EXPERT)
    w_down = w_down.reshape(DEPTH * N_EXPERTS, D_EXPERT, D_MODEL)
    rows_spec = pl.BlockSpec((MOE_BM, D_MODEL), lambda i, be, nu: (i, 0))
    grid_spec = pltpu.PrefetchScalarGridSpec(
        num_scalar_prefetch=2,
        grid=(n_blocks,),
        in_specs=[
            rows_spec,
            pl.BlockSpec((None, D_MODEL, D_EXPERT), lambda i, be, nu: (be[i], 0, 0)),
            pl.BlockSpec((None, D_MODEL, D_EXPERT), lambda i, be, nu: (be[i], 0, 0)),
            pl.BlockSpec((None, D_EXPERT, D_MODEL), lambda i, be, nu: (be[i], 0, 0)),
        ],
        out_specs=rows_spec,
        scratch_shapes=[
            pltpu.VMEM((D_MODEL, D_EXPERT), BF16),
            pltpu.VMEM((D_MODEL, D_EXPERT), BF16),
            pltpu.VMEM((D_EXPERT, D_MODEL), BF16),
        ],
    )
    return pl.pallas_call(
        _expert_kernel,
        out_shape=jax.ShapeDtypeStruct(xs.shape, F32),
        grid_spec=grid_spec,
        compiler_params=_params(("arbitrary",)),
        name="experts",
    )(blk_e, n_used, xs, w_gate, w_up, w_down)


def _combine_kernel(pos_ref, rows_hbm, x1_ref, route_ref, mod_ref, g_ref, b_ref, y_ref, buf, sem):
    i = pl.program_id(0)
    n_tiles = pl.num_programs(0)

    def row_copy(src_row, slot, k, r):
        return pltpu.make_async_copy(rows_hbm.at[pl.ds(src_row, 1)], buf.at[slot, k, pl.ds(r, 1)],
                                     sem.at[slot])

    def issue(tile, slot):
        def body(r, carry):
            base = (tile * TM + r) * 2
            row_copy(pos_ref[base], slot, 0, r).start()
            row_copy(pos_ref[base + 1], slot, 1, r).start()
            return carry
        lax.fori_loop(0, TM, body, 0, unroll=8)

    @pl.when(i == 0)
    def _():
        issue(0, 0)

    @pl.when(i + 1 < n_tiles)
    def _():
        issue(i + 1, (i + 1) % 2)

    slot = i % 2
    for k in range(2):
        pltpu.make_async_copy(rows_hbm.at[pl.ds(0, TM)], buf.at[slot, k], sem.at[slot]).wait()
    route = route_ref[...]
    moe = (route[:, INFO_W:INFO_W + 1] * buf[slot, 0]
           + route[:, INFO_W + 1:INFO_W + 2] * buf[slot, 1])
    y_ref[...] = _layer_norm(DEEPNORM_ALPHA * x1_ref[...] + mod_ref[5:6, :] * moe,
                             g_ref[...], b_ref[...])


def _combine(rows, pos, x1, route, mods, mod_row, tpb, ln_g, ln_b):
    n = x1.shape[0]
    tile = lambda i, p: (i, 0)
    grid_spec = pltpu.PrefetchScalarGridSpec(
        num_scalar_prefetch=1,
        grid=(n // TM,),
        in_specs=[
            pl.BlockSpec(memory_space=pl.ANY),
            pl.BlockSpec((TM, D_MODEL), tile),
            pl.BlockSpec((TM, ROUTE_LANES), tile),
            pl.BlockSpec((None, 6, D_MODEL), lambda i, p: (mod_row(i // tpb), 0, 0)),
            pl.BlockSpec((1, D_MODEL), lambda i, p: (0, 0)),
            pl.BlockSpec((1, D_MODEL), lambda i, p: (0, 0)),
        ],
        out_specs=pl.BlockSpec((TM, D_MODEL), tile),
        scratch_shapes=[pltpu.VMEM((2, 2, TM, D_MODEL), F32), pltpu.SemaphoreType.DMA((2,))],
    )
    return pl.pallas_call(
        _combine_kernel,
        out_shape=jax.ShapeDtypeStruct((n, D_MODEL), F32),
        grid_spec=grid_spec,
        compiler_params=_params(("arbitrary",)),
        name="combine",
    )(pos, rows, x1, route, mods, ln_g.reshape(1, D_MODEL), ln_b.reshape(1, D_MODEL))


def _diff_lambda_init(layer_idx):
    return 0.8 - 0.6 * math.exp(-0.3 * layer_idx)


def _router_weights(w_rg, b_rg, w_re, b_re):
    pad = ROUTE_LANES - N_EXPERTS - N_GROUPS
    w = jnp.concatenate([w_re, w_rg, jnp.zeros((D_MODEL, pad), F32)], axis=1)
    b = jnp.concatenate([b_re, b_rg, jnp.zeros((pad,), F32)]).reshape(1, ROUTE_LANES)
    return w.astype(BF16), b


def kernel(x_prompt, x_sample, cache_a_k, cache_a_v, cache_b_k, cache_b_v, c, c_ctx, w_ada, b_ada, ln_g, ln_b, w_qkv_a, lam_a, subln_a, w_o_a, w_qkv_b, rpb_b, w_o_b, w_rg, b_rg, w_re, b_re, w_gate, w_up, w_down):
    n_mod_rows = 16
    dec_batch = x_sample.shape[0]
    cond = jnp.zeros((n_mod_rows, D_MODEL), F32).at[0].set(c_ctx).at[1:1 + dec_batch].set(c)
    mods_all = _adaln_all(cond, w_ada, b_ada).reshape(DEPTH, n_mod_rows, 6, D_MODEL)
    rope_tables = _rope_tables(x_sample.shape[1])

    def run_group(x, mod_row, latent):
        B, T, _ = x.shape
        n = B * T
        new_kv = []
        for i in range(DEPTH):
            mods = mods_all[i]
            j = i // 2
            if i % 2 == 0:
                w_qkv, w_o = w_qkv_a[j].astype(BF16), w_o_a[j].astype(BF16)
                lam_init = _diff_lambda_init(i)
                if latent:
                    q, k, v = _qkv(x, mods, w_qkv, mod_row, BF16, rope_tables)
                    segs = [(cache_a_k[:, j].reshape(B, -1, D_MODEL),
                             cache_a_v[:, j].reshape(B, -1, D_MODEL)), (k, v)]
                else:
                    q, k, v = _qkv(x, mods, w_qkv, mod_row, F32)
                    segs = [(k, v)]
                    new_kv.append((k, v))
                o = _diff_attention(q, segs, lam_a[j], subln_a[j], lam_init)
            else:
                w_qkv, w_o = w_qkv_b[j].astype(BF16), w_o_b[j].astype(BF16)
                if latent:
                    q, k, v = _qkv(x, mods, w_qkv, mod_row, BF16)
                    o = _neighborhood_attention(q, k, v, cache_b_k[:, j].reshape(B, -1, D_MODEL),
                                                cache_b_v[:, j].reshape(B, -1, D_MODEL), rpb_b[j])
                else:
                    q, k, v = _qkv(x, mods, w_qkv, mod_row, F32)
                    new_kv.append((k, v))
                    o = _dense_attention(q, k, v)
            w_router, b_router = _router_weights(w_rg[i], b_rg[i], w_re[i], b_re[i])
            x1, info, counts = _proj_route(o, w_o, x, mods, mod_row, ln_g[i, 0], ln_b[i, 0],
                                           w_router, b_router)
            pos, blk_e, meta, n_used, n_blocks = _route_tables(info, counts, n, i)
            xs = _dispatch(x1, pos, meta, mods, mod_row, T // TM, n_blocks)
            rows = _experts(xs, blk_e, n_used, n_blocks, w_gate, w_up, w_down)
            x = _combine(rows, pos, x1, info, mods, mod_row, T // TM,
                         ln_g[i, 1], ln_b[i, 1]).reshape(B, T, D_MODEL)
        return x, new_kv

    y_prompt, kv = run_group(x_prompt, lambda b: 0, latent=False)
    y_sample, _ = run_group(x_sample, lambda b: 1 + b, latent=True)

    B, T, _ = x_prompt.shape
    new_a_k = kv[0][0].reshape(B, 1, T, 2 * H_A, HD_A)
    new_a_v = kv[0][1].reshape(B, 1, T, H_A, 2 * HD_A)
    new_b_k = kv[1][0].reshape(B, 1, T, H_B, HD_B)
    new_b_v = kv[1][1].reshape(B, 1, T, H_B, HD_B)
    return y_prompt, y_sample, new_a_k, new_a_v, new_b_k, new_b_v
```

```python
import functools
import math

import jax
import jax.numpy as jnp
import numpy as np
from jax import lax
from jax.experimental import pallas as pl
from jax.experimental.pallas import tpu as pltpu

D_MODEL = 1024
DEPTH = 2
GRID_W = 64
HD_A = 64
H_A = D_MODEL // (2 * HD_A)
H_B = 16
HD_B = D_MODEL // H_B
NA_KH = 8
NA_KW = 16
ROPE_BASE = 10000.0
N_GROUPS = 4
EXP_PER_GROUP = 8
N_EXPERTS = N_GROUPS * EXP_PER_GROUP
D_EXPERT = 512
LN_EPS = 1e-5
DEEPNORM_ALPHA = (2.0 * DEPTH) ** 0.25

F32 = jnp.float32
BF16 = jnp.bfloat16

V7X_LANES = 128
V7X_VMEM_BYTES = 64 * 1024 * 1024
VMEM_LIMIT = V7X_VMEM_BYTES * 7 // 8

TM = 256
TQ = 256
MOE_BM = 256
N_PAIRS = D_MODEL // V7X_LANES
NA_QROWS = TQ // GRID_W
NA_WIN_ROWS = NA_QROWS + NA_KH
NA_WIN = NA_WIN_ROWS * GRID_W
MASK_VALUE = -1e30
ROUTE_LANES = 128


def _params(sem):
    return pltpu.CompilerParams(dimension_semantics=sem, vmem_limit_bytes=VMEM_LIMIT)


def _layer_norm(y, g, b):
    mu = jnp.mean(y, -1, keepdims=True)
    d = y - mu
    var = jnp.mean(d * d, -1, keepdims=True)
    return d * lax.rsqrt(var + LN_EPS) * g + b


def _dot_nt(a, b):
    return lax.dot_general(a, b, (((1,), (1,)), ((), ())), preferred_element_type=F32)


def _ada_kernel(c_ref, w_ref, b_ref, o_ref):
    c = c_ref[...]
    a = c / (1.0 + jnp.exp(-c))
    o_ref[...] = jnp.dot(a, w_ref[...], preferred_element_type=F32,
                         precision=lax.Precision.HIGHEST) + b_ref[...]


def _adaln_all(cond, w_ada, b_ada):
    rows = cond.shape[0]
    tn = 1536
    n6 = 6 * D_MODEL
    return pl.pallas_call(
        _ada_kernel,
        out_shape=jax.ShapeDtypeStruct((DEPTH, rows, n6), F32),
        grid=(DEPTH, n6 // tn),
        in_specs=[
            pl.BlockSpec((rows, D_MODEL), lambda l, j: (0, 0)),
            pl.BlockSpec((None, D_MODEL, tn), lambda l, j: (l, 0, j)),
            pl.BlockSpec((None, 1, tn), lambda l, j: (l, 0, j)),
        ],
        out_specs=pl.BlockSpec((None, rows, tn), lambda l, j: (l, 0, j)),
        compiler_params=_params(("arbitrary", "arbitrary")),
        name="adaln",
    )(cond, w_ada, b_ada.reshape(DEPTH, 1, n6))


def _qkv_kernel(*refs, rope):
    if rope:
        x_ref, mod_ref, w_ref, cos_ref, sin_ref, q_ref, k_ref, v_ref = refs
    else:
        x_ref, mod_ref, w_ref, q_ref, k_ref, v_ref = refs
    h = (x_ref[...] * (1.0 + mod_ref[1:2, :]) + mod_ref[0:1, :]).astype(BF16)
    if rope:
        lane = lax.broadcasted_iota(jnp.int32, (TM, V7X_LANES), 1)
        first = (lane & (HD_A // 2 - 1)) < (HD_A // 4)
        cos = cos_ref[...]
        sin = sin_ref[...]
    for idx, o_ref in enumerate((q_ref, k_ref, v_ref)):
        y = jnp.dot(h, w_ref[:, idx * D_MODEL:(idx + 1) * D_MODEL], preferred_element_type=F32)
        if rope and idx < 2:
            for c in range(N_PAIRS):
                sl = slice(c * V7X_LANES, (c + 1) * V7X_LANES)
                yc = y[:, sl]
                partner = jnp.where(first, pltpu.roll(yc, V7X_LANES - HD_A // 4, 1),
                                    pltpu.roll(yc, HD_A // 4, 1))
                o_ref[:, sl] = (yc * cos + partner * sin).astype(o_ref.dtype)
        else:
            o_ref[...] = y.astype(o_ref.dtype)


def _qkv(x, mods, w_bf16, mod_row, kv_dtype, rope_tables=None):
    B, T, _ = x.shape
    n = B * T
    tpb = T // TM
    rope = rope_tables is not None
    in_specs = [
        pl.BlockSpec((TM, D_MODEL), lambda i: (i, 0)),
        pl.BlockSpec((None, 6, D_MODEL), lambda i: (mod_row(i // tpb), 0, 0)),
        pl.BlockSpec((D_MODEL, 3 * D_MODEL), lambda i: (0, 0)),
    ]
    args = [x.reshape(n, D_MODEL), mods, w_bf16]
    if rope:
        in_specs += [pl.BlockSpec((TM, V7X_LANES), lambda i: (i % tpb, 0))] * 2
        args += list(rope_tables)
    q, k, v = pl.pallas_call(
        functools.partial(_qkv_kernel, rope=rope),
        out_shape=(jax.ShapeDtypeStruct((n, D_MODEL), BF16),
                   jax.ShapeDtypeStruct((n, D_MODEL), kv_dtype),
                   jax.ShapeDtypeStruct((n, D_MODEL), kv_dtype)),
        grid=(n // TM,),
        in_specs=in_specs,
        out_specs=(pl.BlockSpec((TM, D_MODEL), lambda i: (i, 0)),) * 3,
        compiler_params=_params(("arbitrary",)),
        name="qkv",
    )(*args)
    shp = (B, T, D_MODEL)
    return q.reshape(shp), k.reshape(shp), v.reshape(shp)


def _rope_tables(n_tok):
    t = np.arange(n_tok)
    pos = jnp.asarray(np.stack([t // GRID_W, t % GRID_W], -1), F32)
    quarter = HD_A // 4
    inv_freq = ROPE_BASE ** (-jnp.arange(quarter, dtype=F32) / quarter)
    ang = pos[:, :, None] * inv_freq
    cos, sin = jnp.cos(ang), jnp.sin(ang)
    cos_h = jnp.broadcast_to(cos[:, :, None, :], (n_tok, 2, 2, quarter)).reshape(n_tok, HD_A)
    sin_h = jnp.stack([-sin, sin], axis=2).reshape(n_tok, HD_A)
    reps = V7X_LANES // HD_A
    return jnp.tile(cos_h, (1, reps)), jnp.tile(sin_h, (1, reps))


def _pair_attention(qc, key_chunks, value_chunks, biases=None):
    lo = lax.broadcasted_iota(jnp.int32, qc.shape, 1) < V7X_LANES // 2
    zero = jnp.zeros_like(qc)
    qq = jnp.concatenate([jnp.where(lo, qc, zero), jnp.where(lo, zero, qc)], axis=0)
    scores = [_dot_nt(qq, kc) for kc in key_chunks]
    if biases is not None:
        scores = [s if b is None else s + b for s, b in zip(scores, biases)]
    m = functools.reduce(jnp.maximum, [jnp.max(s, -1, keepdims=True) for s in scores])
    acc = None
    for s, vc in zip(scores, value_chunks):
        p = jnp.exp(s - m).astype(BF16)
        v1 = jnp.concatenate([vc, jnp.ones_like(vc)], axis=1)
        part = jnp.dot(p, v1, preferred_element_type=F32)
        acc = part if acc is None else acc + part
    return acc[:, :V7X_LANES] / acc[:, V7X_LANES:]


def _diff_attn_kernel(*refs, nseg, lam_init):
    lam_ref, subln_ref, q_ref = refs[:3]
    seg_refs = refs[3:3 + 2 * nseg]
    o_ref = refs[3 + 2 * nseg]
    lp = lam_ref[...]
    lam = (jnp.exp(jnp.sum(lp[0:1] * lp[1:2], keepdims=True))
           - jnp.exp(jnp.sum(lp[2:3] * lp[3:4], keepdims=True)) + lam_init)
    scale = HD_A ** -0.5
    for h in range(H_A):
        sl = slice(h * V7X_LANES, (h + 1) * V7X_LANES)
        o2 = _pair_attention(q_ref[:, sl] * scale,
                             [seg_refs[2 * j][:, sl].astype(BF16) for j in range(nseg)],
                             [seg_refs[2 * j + 1][:, sl].astype(BF16) for j in range(nseg)])
        acc = o2[:TQ] - lam * o2[TQ:]
        ms = jnp.mean(acc * acc, -1, keepdims=True)
        o = acc * lax.rsqrt(ms + LN_EPS) * subln_ref[...] * (1.0 - lam_init)
        o_ref[:, sl] = o.astype(BF16)


def _diff_attention(q, segs, lam_p, subln, lam_init):
    B, T, _ = q.shape
    in_specs = [
        pl.BlockSpec((4, HD_A), lambda b, i: (0, 0)),
        pl.BlockSpec((1, 2 * HD_A), lambda b, i: (0, 0)),
        pl.BlockSpec((None, TQ, D_MODEL), lambda b, i: (b, i, 0)),
    ]
    args = [lam_p, subln.reshape(1, 2 * HD_A), q]
    for k, v in segs:
        tk = k.shape[1]
        in_specs += [pl.BlockSpec((None, tk, D_MODEL), lambda b, i: (b, 0, 0))] * 2
        args += [k, v]
    return pl.pallas_call(
        functools.partial(_diff_attn_kernel, nseg=len(segs), lam_init=lam_init),
        out_shape=jax.ShapeDtypeStruct((B, T, D_MODEL), BF16),
        grid=(B, T // TQ),
        in_specs=in_specs,
        out_specs=pl.BlockSpec((None, TQ, D_MODEL), lambda b, i: (b, i, 0)),
        compiler_params=_params(("arbitrary", "arbitrary")),
        name="diff_attn",
    )(*args)


def _merge_pair(o2):
    tq = o2.shape[0] // 2
    lo = lax.broadcasted_iota(jnp.int32, (tq, V7X_LANES), 1) < V7X_LANES // 2
    return jnp.where(lo, o2[:tq], o2[tq:])


def _dense_attn_kernel(q_ref, k_ref, v_ref, o_ref):
    scale = HD_B ** -0.5
    for c in range(N_PAIRS):
        sl = slice(c * V7X_LANES, (c + 1) * V7X_LANES)
        o2 = _pair_attention(q_ref[:, sl] * scale, [k_ref[:, sl].astype(BF16)],
                             [v_ref[:, sl].astype(BF16)])
        o_ref[:, sl] = _merge_pair(o2).astype(BF16)


def _dense_attention(q, k, v):
    B, T, _ = q.shape
    spec = pl.BlockSpec((None, T, D_MODEL), lambda b: (b, 0, 0))
    return pl.pallas_call(
        _dense_attn_kernel,
        out_shape=jax.ShapeDtypeStruct((B, T, D_MODEL), BF16),
        grid=(B,),
        in_specs=[spec, spec, spec],
        out_specs=spec,
        compiler_params=_params(("arbitrary",)),
        name="dense_attn",
    )(q, k, v)


NA_WIN_BLOCKS = NA_WIN // TQ


def _na_attn_kernel(*refs):
    q_ref = refs[0]
    k_refs = refs[1:1 + NA_WIN_BLOCKS]
    v_refs = refs[1 + NA_WIN_BLOCKS:1 + 2 * NA_WIN_BLOCKS]
    kc_ref, vc_ref, bias_ref, o_ref = refs[1 + 2 * NA_WIN_BLOCKS:]
    scale = HD_B ** -0.5
    for c in range(N_PAIRS):
        sl = slice(c * V7X_LANES, (c + 1) * V7X_LANES)
        kwin = jnp.concatenate([r[:, sl] for r in k_refs], axis=0)
        vwin = jnp.concatenate([r[:, sl] for r in v_refs], axis=0)
        bias = bias_ref[2 * c:2 * c + 2].reshape(2 * TQ, NA_WIN)
        o2 = _pair_attention(q_ref[:, sl] * scale,
                             [kwin, kc_ref[:, sl].astype(BF16)],
                             [vwin, vc_ref[:, sl].astype(BF16)],
                             biases=[bias, None])
        o_ref[:, sl] = _merge_pair(o2).astype(BF16)


def _na_bias_tables(rpb, rows):
    kh = min(NA_KH, rows)
    n_heads = rpb.shape[0]
    qcol = np.arange(GRID_W)[:, None]
    kcol = np.arange(GRID_W)[None, :]
    c_start = np.clip(qcol - NA_KW // 2, 0, GRID_W - NA_KW)
    col_ok = (kcol >= c_start) & (kcol < c_start + NA_KW)
    col_off = np.clip(kcol - qcol + NA_KW - 1, 0, 2 * NA_KW - 2)
    onehot = (col_off[None] == np.arange(2 * NA_KW - 1)[:, None, None]).astype(np.float32)
    cols = jnp.einsum('hrc,cqk->hqrk', rpb, jnp.asarray(onehot), precision=lax.Precision.HIGHEST)
    cols = jnp.where(col_ok[None, :, None, :], cols, MASK_VALUE)
    tables = []
    for r0 in (0, NA_QROWS, rows - NA_QROWS):
        ws = int(np.clip(r0 - NA_KH // 2, 0, rows - NA_WIN_ROWS))
        blocks = []
        for qrow in range(r0, r0 + NA_QROWS):
            r_start = int(np.clip(qrow - kh // 2, 0, rows - kh))
            off_lo = r_start - qrow + NA_KH - 1
            band = cols[:, :, off_lo:off_lo + kh, :].reshape(n_heads, GRID_W, kh * GRID_W)
            left = (r_start - ws) * GRID_W
            blocks.append(jnp.pad(band, ((0, 0), (0, 0), (left, NA_WIN - kh * GRID_W - left)),
                                  constant_values=MASK_VALUE))
        tables.append(jnp.concatenate(blocks, axis=1))
    return jnp.stack(tables)


def _neighborhood_attention(q, k, v, k_ctx, v_ctx, rpb):
    B, S, _ = q.shape
    P = k_ctx.shape[1]
    rows = S // GRID_W
    nqb = S // TQ
    assert rows >= NA_WIN_ROWS and nqb >= NA_WIN_BLOCKS
    bias = _na_bias_tables(rpb, rows)

    def bias_cfg(qb):
        return jnp.where(qb == 0, 0, jnp.where(qb == nqb - 1, 2, 1))

    def win_spec(j):
        return pl.BlockSpec((None, TQ, D_MODEL),
                            lambda qb, b: (b, jnp.clip(qb - 1, 0, nqb - NA_WIN_BLOCKS) + j, 0))

    win_specs = [win_spec(j) for j in range(NA_WIN_BLOCKS)]
    ctx_spec = pl.BlockSpec((None, P, D_MODEL), lambda qb, b: (b, 0, 0))
    q_spec = pl.BlockSpec((None, TQ, D_MODEL), lambda qb, b: (b, qb, 0))
    bias_spec = pl.BlockSpec((None, H_B, TQ, NA_WIN), lambda qb, b: (bias_cfg(qb), 0, 0, 0),
                             pipeline_mode=pl.Buffered(1))
    return pl.pallas_call(
        _na_attn_kernel,
        out_shape=jax.ShapeDtypeStruct((B, S, D_MODEL), BF16),
        grid=(nqb, B),
        in_specs=[q_spec] + win_specs + win_specs + [ctx_spec, ctx_spec, bias_spec],
        out_specs=q_spec,
        compiler_params=_params(("arbitrary", "arbitrary")),
        name="na_attn",
    )(q, *([k] * NA_WIN_BLOCKS), *([v] * NA_WIN_BLOCKS), k_ctx, v_ctx, bias)


INFO_E, INFO_W, INFO_RANK = 0, 2, 4


def _route(logits):
    lane = lax.broadcasted_iota(jnp.int32, logits.shape, 1).astype(F32)
    neg = -jnp.inf
    gmask = (lane >= N_EXPERTS) & (lane < N_EXPERTS + N_GROUPS)
    gl = jnp.where(gmask, logits, neg)
    gmax = jnp.max(gl, -1, keepdims=True)
    g_sel = jnp.min(jnp.where(gl == gmax, lane - N_EXPERTS, float(N_GROUPS)), -1, keepdims=True)
    p_g = 1.0 / jnp.sum(jnp.where(gmask, jnp.exp(gl - gmax), 0.0), -1, keepdims=True)
    emask = (lane >= g_sel * EXP_PER_GROUP) & (lane < (g_sel + 1.0) * EXP_PER_GROUP)
    el = jnp.where(emask, logits, neg)
    v1 = jnp.max(el, -1, keepdims=True)
    e1 = jnp.min(jnp.where(el == v1, lane, float(ROUTE_LANES)), -1, keepdims=True)
    el2 = jnp.where(lane == e1, neg, el)
    v2 = jnp.max(el2, -1, keepdims=True)
    e2 = jnp.min(jnp.where(el2 == v2, lane, float(ROUTE_LANES)), -1, keepdims=True)
    e21 = jnp.exp(v2 - v1)
    w1 = p_g / (1.0 + e21)
    w2 = p_g * e21 / (1.0 + e21)
    return lane, e1, e2, w1, w2


def _proj_route_kernel(o_ref, w_ref, x_ref, mod_ref, g_ref, b_ref, wr_ref, br_ref,
                       x1_ref, info_ref, cnt_ref, run):
    @pl.when(pl.program_id(0) == 0)
    def _():
        run[...] = jnp.zeros_like(run)

    out = jnp.dot(o_ref[...], w_ref[...], preferred_element_type=F32)
    x1 = _layer_norm(DEEPNORM_ALPHA * x_ref[...] + mod_ref[2:3, :] * out, g_ref[...], b_ref[...])
    x1_ref[...] = x1
    h2 = x1 * (1.0 + mod_ref[4:5, :]) + mod_ref[3:4, :]
    logits = jnp.dot(h2.astype(BF16), wr_ref[...], preferred_element_type=F32) + br_ref[...]
    lane, e1, e2, w1, w2 = _route(logits)

    oh1 = lane == e1
    oh2 = lane == e2
    oh = jnp.where(oh1 | oh2, 1.0, 0.0)
    t_row = lax.broadcasted_iota(jnp.int32, (TM, TM), 0)
    t_col = lax.broadcasted_iota(jnp.int32, (TM, TM), 1)
    earlier = jnp.where(t_col < t_row, 1.0, 0.0).astype(BF16)
    base = run[0:1, :] + jnp.dot(earlier, oh.astype(BF16), preferred_element_type=F32)
    rank1 = jnp.sum(jnp.where(oh1, base, 0.0), -1, keepdims=True)
    rank2 = jnp.sum(jnp.where(oh2, base, 0.0), -1, keepdims=True)
    run[0:1, :] = run[0:1, :] + jnp.sum(oh, 0, keepdims=True)
    cnt_ref[...] = run[...]

    info = jnp.zeros_like(logits)
    for col, val in ((INFO_E, e1), (INFO_E + 1, e2), (INFO_W, w1), (INFO_W + 1, w2),
                     (INFO_RANK, rank1), (INFO_RANK + 1, rank2)):
        info = jnp.where(lane == float(col), val, info)
    info_ref[...] = info


def _proj_route(o, w_o_bf16, x, mods, mod_row, ln_g, ln_b, w_router, b_router):
    B, T, _ = x.shape
    n = B * T
    tpb = T // TM
    tile = pl.BlockSpec((TM, D_MODEL), lambda i: (i, 0))
    vec = pl.BlockSpec((1, D_MODEL), lambda i: (0, 0))
    return pl.pallas_call(
        _proj_route_kernel,
        out_shape=(jax.ShapeDtypeStruct((n, D_MODEL), F32),
                   jax.ShapeDtypeStruct((n, ROUTE_LANES), F32),
                   jax.ShapeDtypeStruct((8, ROUTE_LANES), F32)),
        grid=(n // TM,),
        in_specs=[tile,
                  pl.BlockSpec((D_MODEL, D_MODEL), lambda i: (0, 0)),
                  tile,
                  pl.BlockSpec((None, 6, D_MODEL), lambda i: (mod_row(i // tpb), 0, 0)),
                  vec, vec,
                  pl.BlockSpec((D_MODEL, ROUTE_LANES), lambda i: (0, 0)),
                  pl.BlockSpec((1, ROUTE_LANES), lambda i: (0, 0))],
        out_specs=(tile,
                   pl.BlockSpec((TM, ROUTE_LANES), lambda i: (i, 0)),
                   pl.BlockSpec((8, ROUTE_LANES), lambda i: (0, 0))),
        scratch_shapes=[pltpu.VMEM((8, ROUTE_LANES), F32)],
        compiler_params=_params(("arbitrary",)),
        name="proj_route",
    )(o.reshape(n, D_MODEL), w_o_bf16, x.reshape(n, D_MODEL), mods,
      ln_g.reshape(1, D_MODEL), ln_b.reshape(1, D_MODEL), w_router, b_router)


def _route_tables(info, counts, n, layer):
    cnt = counts[0, :N_EXPERTS].astype(jnp.int32)
    padded = (cnt + MOE_BM - 1) // MOE_BM * MOE_BM
    pend = jnp.cumsum(padded)
    pstart = pend - padded
    n_blocks = 2 * n // MOE_BM + N_EXPERTS
    n_used = pend[-1:] // MOE_BM
    e = info[:, INFO_E:INFO_E + 2].astype(jnp.int32)
    rank = info[:, INFO_RANK:INFO_RANK + 2].astype(jnp.int32)
    experts = jnp.arange(N_EXPERTS, dtype=jnp.int32)
    pos = jnp.sum(jnp.where(e[..., None] == experts, pstart, 0), -1) + rank
    blk = jnp.minimum(jnp.arange(n_blocks, dtype=jnp.int32), n_used - 1) * MOE_BM
    blk_e = jnp.sum((blk[:, None] >= pend[None, :]).astype(jnp.int32), axis=1) + layer * N_EXPERTS
    meta = jnp.concatenate([pstart + cnt, padded - cnt, n_used]).astype(jnp.int32)
    return pos.reshape(-1).astype(jnp.int32), blk_e.astype(jnp.int32), meta, n_used, n_blocks


ROW_SLAB = D_MODEL // V7X_LANES


def _slab(row, n_rows=1):
    return pl.ds(pl.multiple_of(row * ROW_SLAB, ROW_SLAB), n_rows * ROW_SLAB)


def _store_rows(ref, x):
    rows = x.shape[0]
    for c in range(ROW_SLAB):
        ref[pl.ds(c, rows, stride=ROW_SLAB), :] = x[:, c * V7X_LANES:(c + 1) * V7X_LANES]


def _load_rows(ref, rows):
    return jnp.concatenate([ref[pl.ds(c, rows, stride=ROW_SLAB), :] for c in range(ROW_SLAB)], axis=1)


def _dispatch_kernel(pos_ref, meta_ref, x1_ref, mod_ref, xs_hbm, hbuf, zbuf, sem, sem_z, *,
                     n_blocks):
    i = pl.program_id(0)
    n_tiles = pl.num_programs(0)
    slot = i % 2

    @pl.when(i == 0)
    def _():
        zbuf[...] = jnp.zeros_like(zbuf)

    def wait_slot(s):
        for _ in range(2):
            pltpu.make_async_copy(hbuf.at[s], xs_hbm.at[pl.ds(0, TM * ROW_SLAB)], sem.at[s]).wait()

    @pl.when(i >= 2)
    def _():
        wait_slot(slot)

    _store_rows(hbuf.at[slot], x1_ref[...] * (1.0 + mod_ref[4:5, :]) + mod_ref[3:4, :])

    def scatter_row(r, carry):
        base = (i * TM + r) * 2
        src = hbuf.at[slot, _slab(r)]
        for k in range(2):
            pltpu.make_async_copy(src, xs_hbm.at[_slab(pos_ref[base + k])],
                                  sem.at[slot]).start(priority=k)
        return carry
    lax.fori_loop(0, TM, scatter_row, 0, unroll=8)

    @pl.when(i == n_tiles - 1)
    def _():
        def zero_copy(row, size):
            return pltpu.make_async_copy(zbuf.at[pl.ds(0, size * ROW_SLAB)],
                                         xs_hbm.at[_slab(row, size)], sem_z)

        def pad_expert(e, carry):
            start = meta_ref[e]
            length = meta_ref[N_EXPERTS + e]
            size = MOE_BM // 2
            while size >= 1:
                @pl.when((length & size) != 0)
                def _(size=size):
                    zero_copy(start + (length & ~(2 * size - 1)), size).start()
                size //= 2
            return carry
        lax.fori_loop(0, N_EXPERTS, pad_expert, 0)

        def tail_block(blk, carry):
            zero_copy(blk * MOE_BM, MOE_BM).start()
            return carry
        lax.fori_loop(meta_ref[2 * N_EXPERTS], n_blocks, tail_block, 0)

        wait_slot(slot)

        @pl.when(i >= 1)
        def _():
            wait_slot(1 - slot)

        for _ in range(N_EXPERTS):
            zero_copy(0, MOE_BM).wait()


def _dispatch(x1, pos, meta, mods, mod_row, tpb, n_blocks):
    n = x1.shape[0]
    assert n_blocks * MOE_BM - 2 * n == N_EXPERTS * MOE_BM
    grid_spec = pltpu.PrefetchScalarGridSpec(
        num_scalar_prefetch=2,
        grid=(n // TM,),
        in_specs=[
            pl.BlockSpec((TM, D_MODEL), lambda i, p, m: (i, 0)),
            pl.BlockSpec((None, 6, D_MODEL), lambda i, p, m: (mod_row(i // tpb), 0, 0)),
        ],
        out_specs=pl.BlockSpec(memory_space=pl.ANY),
        scratch_shapes=[
            pltpu.VMEM((2, TM * ROW_SLAB, V7X_LANES), F32),
            pltpu.VMEM((MOE_BM * ROW_SLAB, V7X_LANES), F32),
            pltpu.SemaphoreType.DMA((2,)),
            pltpu.SemaphoreType.DMA(()),
        ],
    )
    return pl.pallas_call(
        functools.partial(_dispatch_kernel, n_blocks=n_blocks),
        out_shape=jax.ShapeDtypeStruct((n_blocks * MOE_BM * ROW_SLAB, V7X_LANES), F32),
        grid_spec=grid_spec,
        compiler_params=_params(("arbitrary",)),
        name="dispatch",
    )(pos, meta, x1, mods)


def _expert_kernel(be_ref, nused_ref, x_ref, wg_ref, wu_ref, wd_ref, o_ref, wg_s, wu_s, wd_s):
    i = pl.program_id(0)
    active = i < nused_ref[0]
    changed = (i == 0) | (be_ref[i] != be_ref[jnp.maximum(i - 1, 0)])

    @pl.when(active & changed)
    def _():
        wg_s[...] = wg_ref[...].astype(BF16)
        wu_s[...] = wu_ref[...].astype(BF16)
        wd_s[...] = wd_ref[...].astype(BF16)

    @pl.when(active)
    def _():
        xb = _load_rows(x_ref, MOE_BM).astype(BF16)
        g = jnp.dot(xb, wg_s[...], preferred_element_type=F32)
        u = jnp.dot(xb, wu_s[...], preferred_element_type=F32)
        hb = (g / (1.0 + jnp.exp(-g)) * u).astype(BF16)
        _store_rows(o_ref, jnp.dot(hb, wd_s[...], preferred_element_type=F32))

    @pl.when(jnp.logical_not(active))
    def _():
        o_ref[...] = jnp.zeros_like(o_ref)


def _experts(xs, blk_e, n_used, n_blocks, w_gate, w_up, w_down):
    w_gate = w_gate.reshape(DEPTH * N_EXPERTS, D_MODEL, D_EXPERT)
    w_up = w_up.reshape(DEPTH * N_EXPERTS, D_MODEL, D_EXPERT)
    w_down = w_down.reshape(DEPTH * N_EXPERTS, D_EXPERT, D_MODEL)
    rows_spec = pl.BlockSpec((MOE_BM * ROW_SLAB, V7X_LANES), lambda i, be, nu: (i, 0))
    grid_spec = pltpu.PrefetchScalarGridSpec(
        num_scalar_prefetch=2,
        grid=(n_blocks,),
        in_specs=[
            rows_spec,
            pl.BlockSpec((None, D_MODEL, D_EXPERT), lambda i, be, nu: (be[i], 0, 0)),
            pl.BlockSpec((None, D_MODEL, D_EXPERT), lambda i, be, nu: (be[i], 0, 0)),
            pl.BlockSpec((None, D_EXPERT, D_MODEL), lambda i, be, nu: (be[i], 0, 0)),
        ],
        out_specs=rows_spec,
        scratch_shapes=[
            pltpu.VMEM((D_MODEL, D_EXPERT), BF16),
            pltpu.VMEM((D_MODEL, D_EXPERT), BF16),
            pltpu.VMEM((D_EXPERT, D_MODEL), BF16),
        ],
    )
    return pl.pallas_call(
        _expert_kernel,
        out_shape=jax.ShapeDtypeStruct(xs.shape, F32),
        grid_spec=grid_spec,
        compiler_params=_params(("arbitrary",)),
        name="experts",
    )(blk_e, n_used, xs, w_gate, w_up, w_down)


def _combine_kernel(pos_ref, rows_hbm, x1_ref, route_ref, mod_ref, g_ref, b_ref, y_ref, buf, sem):
    i = pl.program_id(0)
    n_tiles = pl.num_programs(0)

    def issue(tile, slot):
        def body(r, carry):
            base = (tile * TM + r) * 2
            for k in range(2):
                pltpu.make_async_copy(rows_hbm.at[_slab(pos_ref[base + k])],
                                      buf.at[slot, k, _slab(r)], sem.at[slot]).start(priority=k)
            return carry
        lax.fori_loop(0, TM, body, 0, unroll=8)

    @pl.when(i == 0)
    def _():
        issue(0, 0)

    @pl.when(i + 1 < n_tiles)
    def _():
        issue(i + 1, (i + 1) % 2)

    slot = i % 2
    for k in range(2):
        pltpu.make_async_copy(rows_hbm.at[pl.ds(0, TM * ROW_SLAB)], buf.at[slot, k],
                              sem.at[slot]).wait()
    route = route_ref[...]
    moe = (route[:, INFO_W:INFO_W + 1] * _load_rows(buf.at[slot, 0], TM)
           + route[:, INFO_W + 1:INFO_W + 2] * _load_rows(buf.at[slot, 1], TM))
    y_ref[...] = _layer_norm(DEEPNORM_ALPHA * x1_ref[...] + mod_ref[5:6, :] * moe,
                             g_ref[...], b_ref[...])


def _combine(rows, pos, x1, route, mods, mod_row, tpb, ln_g, ln_b):
    n = x1.shape[0]
    tile = lambda i, p: (i, 0)
    grid_spec = pltpu.PrefetchScalarGridSpec(
        num_scalar_prefetch=1,
        grid=(n // TM,),
        in_specs=[
            pl.BlockSpec(memory_space=pl.ANY),
            pl.BlockSpec((TM, D_MODEL), tile),
            pl.BlockSpec((TM, ROUTE_LANES), tile),
            pl.BlockSpec((None, 6, D_MODEL), lambda i, p: (mod_row(i // tpb), 0, 0)),
            pl.BlockSpec((1, D_MODEL), lambda i, p: (0, 0)),
            pl.BlockSpec((1, D_MODEL), lambda i, p: (0, 0)),
        ],
        out_specs=pl.BlockSpec((TM, D_MODEL), tile),
        scratch_shapes=[pltpu.VMEM((2, 2, TM * ROW_SLAB, V7X_LANES), F32),
                        pltpu.SemaphoreType.DMA((2,))],
    )
    return pl.pallas_call(
        _combine_kernel,
        out_shape=jax.ShapeDtypeStruct((n, D_MODEL), F32),
        grid_spec=grid_spec,
        compiler_params=_params(("arbitrary",)),
        name="combine",
    )(pos, rows, x1, route, mods, ln_g.reshape(1, D_MODEL), ln_b.reshape(1, D_MODEL))


def _diff_lambda_init(layer_idx):
    return 0.8 - 0.6 * math.exp(-0.3 * layer_idx)


def _router_weights(w_rg, b_rg, w_re, b_re):
    pad = ROUTE_LANES - N_EXPERTS - N_GROUPS
    w = jnp.concatenate([w_re, w_rg, jnp.zeros((D_MODEL, pad), F32)], axis=1)
    b = jnp.concatenate([b_re, b_rg, jnp.zeros((pad,), F32)]).reshape(1, ROUTE_LANES)
    return w.astype(BF16), b


def kernel(x_prompt, x_sample, cache_a_k, cache_a_v, cache_b_k, cache_b_v, c, c_ctx, w_ada, b_ada, ln_g, ln_b, w_qkv_a, lam_a, subln_a, w_o_a, w_qkv_b, rpb_b, w_o_b, w_rg, b_rg, w_re, b_re, w_gate, w_up, w_down):
    n_mod_rows = 16
    dec_batch = x_sample.shape[0]
    cond = jnp.zeros((n_mod_rows, D_MODEL), F32).at[0].set(c_ctx).at[1:1 + dec_batch].set(c)
    mods_all = _adaln_all(cond, w_ada, b_ada).reshape(DEPTH, n_mod_rows, 6, D_MODEL)
    rope_tables = _rope_tables(x_sample.shape[1])

    def run_group(x, mod_row, latent):
        B, T, _ = x.shape
        n = B * T
        new_kv = []
        for i in range(DEPTH):
            mods = mods_all[i]
            j = i // 2
            if i % 2 == 0:
                w_qkv, w_o = w_qkv_a[j].astype(BF16), w_o_a[j].astype(BF16)
                lam_init = _diff_lambda_init(i)
                if latent:
                    q, k, v = _qkv(x, mods, w_qkv, mod_row, BF16, rope_tables)
                    segs = [(cache_a_k[:, j].reshape(B, -1, D_MODEL),
                             cache_a_v[:, j].reshape(B, -1, D_MODEL)), (k, v)]
                else:
                    q, k, v = _qkv(x, mods, w_qkv, mod_row, F32)
                    segs = [(k, v)]
                    new_kv.append((k, v))
                o = _diff_attention(q, segs, lam_a[j], subln_a[j], lam_init)
            else:
                w_qkv, w_o = w_qkv_b[j].astype(BF16), w_o_b[j].astype(BF16)
                if latent:
                    q, k, v = _qkv(x, mods, w_qkv, mod_row, BF16)
                    o = _neighborhood_attention(q, k, v, cache_b_k[:, j].reshape(B, -1, D_MODEL),
                                                cache_b_v[:, j].reshape(B, -1, D_MODEL), rpb_b[j])
                else:
                    q, k, v = _qkv(x, mods, w_qkv, mod_row, F32)
                    new_kv.append((k, v))
                    o = _dense_attention(q, k, v)
            w_router, b_router = _router_weights(w_rg[i], b_rg[i], w_re[i], b_re[i])
            x1, info, counts = _proj_route(o, w_o, x, mods, mod_row, ln_g[i, 0], ln_b[i, 0],
                                           w_router, b_router)
            pos, blk_e, meta, n_used, n_blocks = _route_tables(info, counts, n, i)
            xs = _dispatch(x1, pos, meta, mods, mod_row, T // TM, n_blocks)
            rows = _experts(xs, blk_e, n_used, n_blocks, w_gate, w_up, w_down)
            x = _combine(rows, pos, x1, info, mods, mod_row, T // TM,
                         ln_g[i, 1], ln_b[i, 1]).reshape(B, T, D_MODEL)
        return x, new_kv

    y_prompt, kv = run_group(x_prompt, lambda b: 0, latent=False)
    y_sample, _ = run_group(x_sample, lambda b: 1 + b, latent=True)

    B, T, _ = x_prompt.shape
    new_a_k = kv[0][0].reshape(B, 1, T, 2 * H_A, HD_A)
    new_a_v = kv[0][1].reshape(B, 1, T, H_A, 2 * HD_A)
    new_b_k = kv[1][0].reshape(B, 1, T, H_B, HD_B)
    new_b_v = kv[1][1].reshape(B, 1, T, H_B, HD_B)
    return y_prompt, y_sample, new_a_k, new_a_v, new_b_k, new_b_v
```

```python
import functools
import math

import jax
import jax.numpy as jnp
import numpy as np
from jax import lax
from jax.experimental import pallas as pl
from jax.experimental.pallas import tpu as pltpu

D_MODEL = 1024
DEPTH = 2
GRID_W = 64
HD_A = 64
H_A = D_MODEL // (2 * HD_A)
H_B = 16
HD_B = D_MODEL // H_B
NA_KH = 8
NA_KW = 16
ROPE_BASE = 10000.0
N_GROUPS = 4
EXP_PER_GROUP = 8
N_EXPERTS = N_GROUPS * EXP_PER_GROUP
D_EXPERT = 512
LN_EPS = 1e-5
DEEPNORM_ALPHA = (2.0 * DEPTH) ** 0.25

F32 = jnp.float32
BF16 = jnp.bfloat16

V7X_LANES = 128
V7X_VMEM_BYTES = 64 * 1024 * 1024
VMEM_LIMIT = V7X_VMEM_BYTES * 7 // 8

TM = 256
TQ = 256
MOE_BM = 256
N_PAIRS = D_MODEL // V7X_LANES
NA_QROWS = TQ // GRID_W
NA_WIN_ROWS = NA_QROWS + NA_KH
NA_WIN = NA_WIN_ROWS * GRID_W
MASK_VALUE = -1e30
ROUTE_LANES = 128


def _params(sem):
    return pltpu.CompilerParams(dimension_semantics=sem, vmem_limit_bytes=VMEM_LIMIT)


def _layer_norm(y, g, b):
    mu = jnp.mean(y, -1, keepdims=True)
    d = y - mu
    var = jnp.mean(d * d, -1, keepdims=True)
    return d * lax.rsqrt(var + LN_EPS) * g + b


def _dot_nt(a, b):
    return lax.dot_general(a, b, (((1,), (1,)), ((), ())), preferred_element_type=F32)


def _ada_kernel(c_ref, w_ref, b_ref, o_ref):
    c = c_ref[...]
    a = c / (1.0 + jnp.exp(-c))
    o_ref[...] = jnp.dot(a, w_ref[...], preferred_element_type=F32,
                         precision=lax.Precision.HIGHEST) + b_ref[...]


def _adaln_all(cond, w_ada, b_ada):
    rows = cond.shape[0]
    tn = 1536
    n6 = 6 * D_MODEL
    return pl.pallas_call(
        _ada_kernel,
        out_shape=jax.ShapeDtypeStruct((DEPTH, rows, n6), F32),
        grid=(DEPTH, n6 // tn),
        in_specs=[
            pl.BlockSpec((rows, D_MODEL), lambda l, j: (0, 0)),
            pl.BlockSpec((None, D_MODEL, tn), lambda l, j: (l, 0, j)),
            pl.BlockSpec((None, 1, tn), lambda l, j: (l, 0, j)),
        ],
        out_specs=pl.BlockSpec((None, rows, tn), lambda l, j: (l, 0, j)),
        compiler_params=_params(("arbitrary", "arbitrary")),
        name="adaln",
    )(cond, w_ada, b_ada.reshape(DEPTH, 1, n6))


def _qkv_kernel(*refs, rope):
    if rope:
        x_ref, mod_ref, w_ref, cos_ref, sin_ref, q_ref, k_ref, v_ref = refs
    else:
        x_ref, mod_ref, w_ref, q_ref, k_ref, v_ref = refs
    h = (x_ref[...] * (1.0 + mod_ref[1:2, :]) + mod_ref[0:1, :]).astype(BF16)
    if rope:
        lane = lax.broadcasted_iota(jnp.int32, (TM, V7X_LANES), 1)
        first = (lane & (HD_A // 2 - 1)) < (HD_A // 4)
        cos = cos_ref[...]
        sin = sin_ref[...]
    for idx, o_ref in enumerate((q_ref, k_ref, v_ref)):
        y = jnp.dot(h, w_ref[:, idx * D_MODEL:(idx + 1) * D_MODEL], preferred_element_type=F32)
        if rope and idx < 2:
            for c in range(N_PAIRS):
                sl = slice(c * V7X_LANES, (c + 1) * V7X_LANES)
                yc = y[:, sl]
                partner = jnp.where(first, pltpu.roll(yc, V7X_LANES - HD_A // 4, 1),
                                    pltpu.roll(yc, HD_A // 4, 1))
                o_ref[:, sl] = (yc * cos + partner * sin).astype(o_ref.dtype)
        else:
            o_ref[...] = y.astype(o_ref.dtype)


def _qkv(x, mods, w_bf16, mod_row, kv_dtype, rope_tables=None):
    B, T, _ = x.shape
    n = B * T
    tpb = T // TM
    rope = rope_tables is not None
    in_specs = [
        pl.BlockSpec((TM, D_MODEL), lambda i: (i, 0)),
        pl.BlockSpec((None, 6, D_MODEL), lambda i: (mod_row(i // tpb), 0, 0)),
        pl.BlockSpec((D_MODEL, 3 * D_MODEL), lambda i: (0, 0)),
    ]
    args = [x.reshape(n, D_MODEL), mods, w_bf16]
    if rope:
        in_specs += [pl.BlockSpec((TM, V7X_LANES), lambda i: (i % tpb, 0))] * 2
        args += list(rope_tables)
    q, k, v = pl.pallas_call(
        functools.partial(_qkv_kernel, rope=rope),
        out_shape=(jax.ShapeDtypeStruct((n, D_MODEL), BF16),
                   jax.ShapeDtypeStruct((n, D_MODEL), kv_dtype),
                   jax.ShapeDtypeStruct((n, D_MODEL), kv_dtype)),
        grid=(n // TM,),
        in_specs=in_specs,
        out_specs=(pl.BlockSpec((TM, D_MODEL), lambda i: (i, 0)),) * 3,
        compiler_params=_params(("arbitrary",)),
        name="qkv",
    )(*args)
    shp = (B, T, D_MODEL)
    return q.reshape(shp), k.reshape(shp), v.reshape(shp)


def _rope_tables(n_tok):
    t = np.arange(n_tok)
    pos = jnp.asarray(np.stack([t // GRID_W, t % GRID_W], -1), F32)
    quarter = HD_A // 4
    inv_freq = ROPE_BASE ** (-jnp.arange(quarter, dtype=F32) / quarter)
    ang = pos[:, :, None] * inv_freq
    cos, sin = jnp.cos(ang), jnp.sin(ang)
    cos_h = jnp.broadcast_to(cos[:, :, None, :], (n_tok, 2, 2, quarter)).reshape(n_tok, HD_A)
    sin_h = jnp.stack([-sin, sin], axis=2).reshape(n_tok, HD_A)
    reps = V7X_LANES // HD_A
    return jnp.tile(cos_h, (1, reps)), jnp.tile(sin_h, (1, reps))


def _pair_attention(qc, key_chunks, value_chunks, biases=None):
    lo = lax.broadcasted_iota(jnp.int32, qc.shape, 1) < V7X_LANES // 2
    zero = jnp.zeros_like(qc)
    qq = jnp.concatenate([jnp.where(lo, qc, zero), jnp.where(lo, zero, qc)], axis=0)
    scores = [_dot_nt(qq, kc) for kc in key_chunks]
    if biases is not None:
        scores = [s if b is None else s + b for s, b in zip(scores, biases)]
    m = functools.reduce(jnp.maximum, [jnp.max(s, -1, keepdims=True) for s in scores])
    acc = None
    for s, vc in zip(scores, value_chunks):
        p = jnp.exp(s - m).astype(BF16)
        v1 = jnp.concatenate([vc, jnp.ones_like(vc)], axis=1)
        part = jnp.dot(p, v1, preferred_element_type=F32)
        acc = part if acc is None else acc + part
    return acc[:, :V7X_LANES] / acc[:, V7X_LANES:]


def _diff_attn_kernel(*refs, nseg, lam_init):
    lam_ref, subln_ref, q_ref = refs[:3]
    seg_refs = refs[3:3 + 2 * nseg]
    o_ref = refs[3 + 2 * nseg]
    lp = lam_ref[...]
    lam = (jnp.exp(jnp.sum(lp[0:1] * lp[1:2], keepdims=True))
           - jnp.exp(jnp.sum(lp[2:3] * lp[3:4], keepdims=True)) + lam_init)
    scale = HD_A ** -0.5
    for h in range(H_A):
        sl = slice(h * V7X_LANES, (h + 1) * V7X_LANES)
        o2 = _pair_attention(q_ref[:, sl] * scale,
                             [seg_refs[2 * j][:, sl].astype(BF16) for j in range(nseg)],
                             [seg_refs[2 * j + 1][:, sl].astype(BF16) for j in range(nseg)])
        acc = o2[:TQ] - lam * o2[TQ:]
        ms = jnp.mean(acc * acc, -1, keepdims=True)
        o = acc * lax.rsqrt(ms + LN_EPS) * subln_ref[...] * (1.0 - lam_init)
        o_ref[:, sl] = o.astype(BF16)


def _diff_attention(q, segs, lam_p, subln, lam_init):
    B, T, _ = q.shape
    in_specs = [
        pl.BlockSpec((4, HD_A), lambda b, i: (0, 0)),
        pl.BlockSpec((1, 2 * HD_A), lambda b, i: (0, 0)),
        pl.BlockSpec((None, TQ, D_MODEL), lambda b, i: (b, i, 0)),
    ]
    args = [lam_p, subln.reshape(1, 2 * HD_A), q]
    for k, v in segs:
        tk = k.shape[1]
        in_specs += [pl.BlockSpec((None, tk, D_MODEL), lambda b, i: (b, 0, 0))] * 2
        args += [k, v]
    return pl.pallas_call(
        functools.partial(_diff_attn_kernel, nseg=len(segs), lam_init=lam_init),
        out_shape=jax.ShapeDtypeStruct((B, T, D_MODEL), BF16),
        grid=(B, T // TQ),
        in_specs=in_specs,
        out_specs=pl.BlockSpec((None, TQ, D_MODEL), lambda b, i: (b, i, 0)),
        compiler_params=_params(("arbitrary", "arbitrary")),
        name="diff_attn",
    )(*args)


def _merge_pair(o2):
    tq = o2.shape[0] // 2
    lo = lax.broadcasted_iota(jnp.int32, (tq, V7X_LANES), 1) < V7X_LANES // 2
    return jnp.where(lo, o2[:tq], o2[tq:])


def _dense_attn_kernel(q_ref, k_ref, v_ref, o_ref):
    scale = HD_B ** -0.5
    for c in range(N_PAIRS):
        sl = slice(c * V7X_LANES, (c + 1) * V7X_LANES)
        o2 = _pair_attention(q_ref[:, sl] * scale, [k_ref[:, sl].astype(BF16)],
                             [v_ref[:, sl].astype(BF16)])
        o_ref[:, sl] = _merge_pair(o2).astype(BF16)


def _dense_attention(q, k, v):
    B, T, _ = q.shape
    spec = pl.BlockSpec((None, T, D_MODEL), lambda b: (b, 0, 0))
    return pl.pallas_call(
        _dense_attn_kernel,
        out_shape=jax.ShapeDtypeStruct((B, T, D_MODEL), BF16),
        grid=(B,),
        in_specs=[spec, spec, spec],
        out_specs=spec,
        compiler_params=_params(("arbitrary",)),
        name="dense_attn",
    )(q, k, v)


NA_WIN_BLOCKS = NA_WIN // TQ


def _na_attn_kernel(*refs):
    q_ref = refs[0]
    k_refs = refs[1:1 + NA_WIN_BLOCKS]
    v_refs = refs[1 + NA_WIN_BLOCKS:1 + 2 * NA_WIN_BLOCKS]
    kc_ref, vc_ref, bias_ref, o_ref = refs[1 + 2 * NA_WIN_BLOCKS:]
    scale = HD_B ** -0.5
    for c in range(N_PAIRS):
        sl = slice(c * V7X_LANES, (c + 1) * V7X_LANES)
        kwin = jnp.concatenate([r[:, sl] for r in k_refs], axis=0)
        vwin = jnp.concatenate([r[:, sl] for r in v_refs], axis=0)
        bias = bias_ref[2 * c:2 * c + 2].reshape(2 * TQ, NA_WIN)
        o2 = _pair_attention(q_ref[:, sl] * scale,
                             [kwin, kc_ref[:, sl].astype(BF16)],
                             [vwin, vc_ref[:, sl].astype(BF16)],
                             biases=[bias, None])
        o_ref[:, sl] = _merge_pair(o2).astype(BF16)


def _na_bias_tables(rpb, rows):
    kh = min(NA_KH, rows)
    n_heads = rpb.shape[0]
    qcol = np.arange(GRID_W)[:, None]
    kcol = np.arange(GRID_W)[None, :]
    c_start = np.clip(qcol - NA_KW // 2, 0, GRID_W - NA_KW)
    col_ok = (kcol >= c_start) & (kcol < c_start + NA_KW)
    col_off = np.clip(kcol - qcol + NA_KW - 1, 0, 2 * NA_KW - 2)
    onehot = (col_off[None] == np.arange(2 * NA_KW - 1)[:, None, None]).astype(np.float32)
    cols = jnp.einsum('hrc,cqk->hqrk', rpb, jnp.asarray(onehot), precision=lax.Precision.HIGHEST)
    cols = jnp.where(col_ok[None, :, None, :], cols, MASK_VALUE)
    tables = []
    for r0 in (0, NA_QROWS, rows - NA_QROWS):
        ws = int(np.clip(r0 - NA_KH // 2, 0, rows - NA_WIN_ROWS))
        blocks = []
        for qrow in range(r0, r0 + NA_QROWS):
            r_start = int(np.clip(qrow - kh // 2, 0, rows - kh))
            off_lo = r_start - qrow + NA_KH - 1
            band = cols[:, :, off_lo:off_lo + kh, :].reshape(n_heads, GRID_W, kh * GRID_W)
            left = (r_start - ws) * GRID_W
            blocks.append(jnp.pad(band, ((0, 0), (0, 0), (left, NA_WIN - kh * GRID_W - left)),
                                  constant_values=MASK_VALUE))
        tables.append(jnp.concatenate(blocks, axis=1))
    return jnp.stack(tables)


def _neighborhood_attention(q, k, v, k_ctx, v_ctx, rpb):
    B, S, _ = q.shape
    P = k_ctx.shape[1]
    rows = S // GRID_W
    nqb = S // TQ
    assert rows >= NA_WIN_ROWS and nqb >= NA_WIN_BLOCKS
    bias = _na_bias_tables(rpb, rows)

    def bias_cfg(qb):
        return jnp.where(qb == 0, 0, jnp.where(qb == nqb - 1, 2, 1))

    def win_spec(j):
        return pl.BlockSpec((None, TQ, D_MODEL),
                            lambda qb, b: (b, jnp.clip(qb - 1, 0, nqb - NA_WIN_BLOCKS) + j, 0))

    win_specs = [win_spec(j) for j in range(NA_WIN_BLOCKS)]
    ctx_spec = pl.BlockSpec((None, P, D_MODEL), lambda qb, b: (b, 0, 0))
    q_spec = pl.BlockSpec((None, TQ, D_MODEL), lambda qb, b: (b, qb, 0))
    bias_spec = pl.BlockSpec((None, H_B, TQ, NA_WIN), lambda qb, b: (bias_cfg(qb), 0, 0, 0),
                             pipeline_mode=pl.Buffered(1))
    return pl.pallas_call(
        _na_attn_kernel,
        out_shape=jax.ShapeDtypeStruct((B, S, D_MODEL), BF16),
        grid=(nqb, B),
        in_specs=[q_spec] + win_specs + win_specs + [ctx_spec, ctx_spec, bias_spec],
        out_specs=q_spec,
        compiler_params=_params(("arbitrary", "arbitrary")),
        name="na_attn",
    )(q, *([k] * NA_WIN_BLOCKS), *([v] * NA_WIN_BLOCKS), k_ctx, v_ctx, bias)


INFO_E, INFO_W, INFO_RANK = 0, 2, 4


ROUTE_ROWS = 64


def _route(lt):
    sub = lax.broadcasted_iota(jnp.int32, lt.shape, 0).astype(F32)
    neg = -jnp.inf
    gmask = (sub >= N_EXPERTS) & (sub < N_EXPERTS + N_GROUPS)
    gl = jnp.where(gmask, lt, neg)
    gmax = jnp.max(gl, 0, keepdims=True)
    g_sel = jnp.min(jnp.where(gl == gmax, sub - N_EXPERTS, float(N_GROUPS)), 0, keepdims=True)
    p_g = 1.0 / jnp.sum(jnp.where(gmask, jnp.exp(gl - gmax), 0.0), 0, keepdims=True)
    emask = (sub >= g_sel * EXP_PER_GROUP) & (sub < (g_sel + 1.0) * EXP_PER_GROUP)
    el = jnp.where(emask, lt, neg)
    v1 = jnp.max(el, 0, keepdims=True)
    e1 = jnp.min(jnp.where(el == v1, sub, float(ROUTE_ROWS)), 0, keepdims=True)
    el2 = jnp.where(sub == e1, neg, el)
    v2 = jnp.max(el2, 0, keepdims=True)
    e2 = jnp.min(jnp.where(el2 == v2, sub, float(ROUTE_ROWS)), 0, keepdims=True)
    e21 = jnp.exp(v2 - v1)
    w1 = p_g / (1.0 + e21)
    w2 = p_g * e21 / (1.0 + e21)
    return sub, e1, e2, w1, w2


def _proj_route_kernel(o_ref, w_ref, x_ref, mod_ref, g_ref, b_ref, wr_ref, br_ref,
                       x1_ref, info_ref, cnt_ref, run):
    @pl.when(pl.program_id(0) == 0)
    def _():
        run[...] = jnp.zeros_like(run)

    out = jnp.dot(o_ref[...], w_ref[...], preferred_element_type=F32)
    x1 = _layer_norm(DEEPNORM_ALPHA * x_ref[...] + mod_ref[2:3, :] * out, g_ref[...], b_ref[...])
    x1_ref[...] = x1
    h2 = x1 * (1.0 + mod_ref[4:5, :]) + mod_ref[3:4, :]
    logits = jnp.dot(h2.astype(BF16), wr_ref[...], preferred_element_type=F32) + br_ref[...]
    sub, e1, e2, w1, w2 = _route(logits.T[:ROUTE_ROWS])

    oh1 = sub == e1
    oh2 = sub == e2
    oh = jnp.where(oh1 | oh2, 1.0, 0.0)
    t_row = lax.broadcasted_iota(jnp.int32, (TM, TM), 0)
    t_col = lax.broadcasted_iota(jnp.int32, (TM, TM), 1)
    earlier = jnp.where(t_row < t_col, 1.0, 0.0).astype(BF16)
    base = run[:, 0:1] + jnp.dot(oh.astype(BF16), earlier, preferred_element_type=F32)
    rank1 = jnp.sum(jnp.where(oh1, base, 0.0), 0, keepdims=True)
    rank2 = jnp.sum(jnp.where(oh2, base, 0.0), 0, keepdims=True)
    run[...] = run[...] + jnp.sum(oh, 1, keepdims=True)
    cnt_ref[...] = run[...]

    rec_row = lax.broadcasted_iota(jnp.int32, (ROUTE_LANES, TM), 0)
    rec = jnp.zeros((ROUTE_LANES, TM), F32)
    for row, val in ((INFO_E, e1), (INFO_E + 1, e2), (INFO_W, w1), (INFO_W + 1, w2),
                     (INFO_RANK, rank1), (INFO_RANK + 1, rank2)):
        rec = jnp.where(rec_row == row, val, rec)
    info_ref[...] = rec.T


def _proj_route(o, w_o_bf16, x, mods, mod_row, ln_g, ln_b, w_router, b_router):
    B, T, _ = x.shape
    n = B * T
    tpb = T // TM
    tile = pl.BlockSpec((TM, D_MODEL), lambda i: (i, 0))
    vec = pl.BlockSpec((1, D_MODEL), lambda i: (0, 0))
    return pl.pallas_call(
        _proj_route_kernel,
        out_shape=(jax.ShapeDtypeStruct((n, D_MODEL), F32),
                   jax.ShapeDtypeStruct((n, ROUTE_LANES), F32),
                   jax.ShapeDtypeStruct((ROUTE_ROWS, ROUTE_LANES), F32)),
        grid=(n // TM,),
        in_specs=[tile,
                  pl.BlockSpec((D_MODEL, D_MODEL), lambda i: (0, 0)),
                  tile,
                  pl.BlockSpec((None, 6, D_MODEL), lambda i: (mod_row(i // tpb), 0, 0)),
                  vec, vec,
                  pl.BlockSpec((D_MODEL, ROUTE_LANES), lambda i: (0, 0)),
                  pl.BlockSpec((1, ROUTE_LANES), lambda i: (0, 0))],
        out_specs=(tile,
                   pl.BlockSpec((TM, ROUTE_LANES), lambda i: (i, 0)),
                   pl.BlockSpec((ROUTE_ROWS, ROUTE_LANES), lambda i: (0, 0))),
        scratch_shapes=[pltpu.VMEM((ROUTE_ROWS, ROUTE_LANES), F32)],
        compiler_params=_params(("arbitrary",)),
        name="proj_route",
    )(o.reshape(n, D_MODEL), w_o_bf16, x.reshape(n, D_MODEL), mods,
      ln_g.reshape(1, D_MODEL), ln_b.reshape(1, D_MODEL), w_router, b_router)


def _route_tables(infos, counts, layer):
    cnts = [c[:N_EXPERTS, 0].astype(jnp.int32) for c in counts]
    cnt = functools.reduce(lambda a, b: a + b, cnts)
    padded = (cnt + MOE_BM - 1) // MOE_BM * MOE_BM
    pend = jnp.cumsum(padded)
    n_pairs = 2 * sum(info.shape[0] for info in infos)
    n_blocks = n_pairs // MOE_BM + N_EXPERTS
    n_used = pend[-1:] // MOE_BM
    experts = jnp.arange(N_EXPERTS, dtype=jnp.int32)
    pos = []
    first = pend - padded
    for info, c in zip(infos, cnts):
        e = info[:, INFO_E:INFO_E + 2].astype(jnp.int32)
        rank = info[:, INFO_RANK:INFO_RANK + 2].astype(jnp.int32)
        p = jnp.sum(jnp.where(e[..., None] == experts, first, 0), -1) + rank
        pos.append(p.reshape(-1).astype(jnp.int32))
        first = first + c
    blk = jnp.minimum(jnp.arange(n_blocks, dtype=jnp.int32), n_used - 1) * MOE_BM
    blk_e = jnp.sum((blk[:, None] >= pend[None, :]).astype(jnp.int32), axis=1) + layer * N_EXPERTS
    meta = jnp.concatenate([first, padded - cnt, n_used]).astype(jnp.int32)
    return pos, blk_e.astype(jnp.int32), meta, n_used, n_blocks


ROW_SLAB = D_MODEL // V7X_LANES


def _slab(row, n_rows=1):
    return pl.ds(pl.multiple_of(row * ROW_SLAB, ROW_SLAB), n_rows * ROW_SLAB)


def _store_rows(ref, x):
    rows = x.shape[0]
    for c in range(ROW_SLAB):
        ref[pl.ds(c, rows, stride=ROW_SLAB), :] = x[:, c * V7X_LANES:(c + 1) * V7X_LANES]


def _load_rows(ref, rows):
    return jnp.concatenate([ref[pl.ds(c, rows, stride=ROW_SLAB), :] for c in range(ROW_SLAB)], axis=1)


def _dispatch_kernel(pos_ref, meta_ref, x1a_ref, x1b_ref, mod_ref, xs_hbm, hbuf, zbuf, sem, sem_z, *,
                     n_blocks, tiles_a):
    i = pl.program_id(0)
    n_tiles = pl.num_programs(0)
    slot = i % 2

    @pl.when(i == 0)
    def _():
        zbuf[...] = jnp.zeros_like(zbuf)

    def wait_slot(s):
        for _ in range(2):
            pltpu.make_async_copy(hbuf.at[s], xs_hbm.at[pl.ds(0, TM * ROW_SLAB)], sem.at[s]).wait()

    @pl.when(i >= 2)
    def _():
        wait_slot(slot)

    @pl.when(i < tiles_a)
    def _():
        _store_rows(hbuf.at[slot], x1a_ref[...] * (1.0 + mod_ref[4:5, :]) + mod_ref[3:4, :])

    @pl.when(i >= tiles_a)
    def _():
        _store_rows(hbuf.at[slot], x1b_ref[...] * (1.0 + mod_ref[4:5, :]) + mod_ref[3:4, :])

    def scatter_row(r, carry):
        base = (i * TM + r) * 2
        src = hbuf.at[slot, _slab(r)]
        for k in range(2):
            pltpu.make_async_copy(src, xs_hbm.at[_slab(pos_ref[base + k])],
                                  sem.at[slot]).start(priority=k)
        return carry
    lax.fori_loop(0, TM, scatter_row, 0, unroll=8)

    @pl.when(i == n_tiles - 1)
    def _():
        def zero_copy(row, size):
            return pltpu.make_async_copy(zbuf.at[pl.ds(0, size * ROW_SLAB)],
                                         xs_hbm.at[_slab(row, size)], sem_z)

        def pad_expert(e, carry):
            start = meta_ref[e]
            length = meta_ref[N_EXPERTS + e]
            size = MOE_BM // 2
            while size >= 1:
                @pl.when((length & size) != 0)
                def _(size=size):
                    zero_copy(start + (length & ~(2 * size - 1)), size).start()
                size //= 2
            return carry
        lax.fori_loop(0, N_EXPERTS, pad_expert, 0)

        def tail_block(blk, carry):
            zero_copy(blk * MOE_BM, MOE_BM).start()
            return carry
        lax.fori_loop(meta_ref[2 * N_EXPERTS], n_blocks, tail_block, 0)

        wait_slot(slot)

        @pl.when(i >= 1)
        def _():
            wait_slot(1 - slot)

        for _ in range(N_EXPERTS):
            zero_copy(0, MOE_BM).wait()


def _dispatch(x1a, x1b, pos, meta, mods, mod_row_a, tpb_a, mod_row_b, tpb_b, n_blocks):
    tiles_a = x1a.shape[0] // TM
    tiles_b = x1b.shape[0] // TM
    assert n_blocks * MOE_BM - 2 * (tiles_a + tiles_b) * TM == N_EXPERTS * MOE_BM

    def mod_index(i, p, m):
        row = jnp.where(i < tiles_a, mod_row_a(i // tpb_a),
                        mod_row_b(jnp.maximum(i - tiles_a, 0) // tpb_b))
        return (row, 0, 0)

    grid_spec = pltpu.PrefetchScalarGridSpec(
        num_scalar_prefetch=2,
        grid=(tiles_a + tiles_b,),
        in_specs=[
            pl.BlockSpec((TM, D_MODEL), lambda i, p, m: (jnp.minimum(i, tiles_a - 1), 0)),
            pl.BlockSpec((TM, D_MODEL), lambda i, p, m: (jnp.maximum(i - tiles_a, 0), 0)),
            pl.BlockSpec((None, 6, D_MODEL), mod_index),
        ],
        out_specs=pl.BlockSpec(memory_space=pl.ANY),
        scratch_shapes=[
            pltpu.VMEM((2, TM * ROW_SLAB, V7X_LANES), F32),
            pltpu.VMEM((MOE_BM * ROW_SLAB, V7X_LANES), F32),
            pltpu.SemaphoreType.DMA((2,)),
            pltpu.SemaphoreType.DMA(()),
        ],
    )
    return pl.pallas_call(
        functools.partial(_dispatch_kernel, n_blocks=n_blocks, tiles_a=tiles_a),
        out_shape=jax.ShapeDtypeStruct((n_blocks * MOE_BM * ROW_SLAB, V7X_LANES), F32),
        grid_spec=grid_spec,
        compiler_params=_params(("arbitrary",)),
        name="dispatch",
    )(pos, meta, x1a, x1b, mods)


def _expert_kernel(be_ref, nused_ref, x_ref, wg_ref, wu_ref, wd_ref, o_ref, wg_s, wu_s, wd_s):
    i = pl.program_id(0)
    active = i < nused_ref[0]
    changed = (i == 0) | (be_ref[i] != be_ref[jnp.maximum(i - 1, 0)])

    @pl.when(active & changed)
    def _():
        wg_s[...] = wg_ref[...].astype(BF16)
        wu_s[...] = wu_ref[...].astype(BF16)
        wd_s[...] = wd_ref[...].astype(BF16)

    @pl.when(active)
    def _():
        xb = _load_rows(x_ref, MOE_BM).astype(BF16)
        g = jnp.dot(xb, wg_s[...], preferred_element_type=F32)
        u = jnp.dot(xb, wu_s[...], preferred_element_type=F32)
        hb = (g / (1.0 + jnp.exp(-g)) * u).astype(BF16)
        _store_rows(o_ref, jnp.dot(hb, wd_s[...], preferred_element_type=F32))

    @pl.when(jnp.logical_not(active))
    def _():
        o_ref[...] = jnp.zeros_like(o_ref)


def _experts(xs, blk_e, n_used, n_blocks, w_gate, w_up, w_down):
    w_gate = w_gate.reshape(DEPTH * N_EXPERTS, D_MODEL, D_EXPERT)
    w_up = w_up.reshape(DEPTH * N_EXPERTS, D_MODEL, D_EXPERT)
    w_down = w_down.reshape(DEPTH * N_EXPERTS, D_EXPERT, D_MODEL)
    rows_spec = pl.BlockSpec((MOE_BM * ROW_SLAB, V7X_LANES), lambda i, be, nu: (i, 0))
    grid_spec = pltpu.PrefetchScalarGridSpec(
        num_scalar_prefetch=2,
        grid=(n_blocks,),
        in_specs=[
            rows_spec,
            pl.BlockSpec((None, D_MODEL, D_EXPERT), lambda i, be, nu: (be[i], 0, 0)),
            pl.BlockSpec((None, D_MODEL, D_EXPERT), lambda i, be, nu: (be[i], 0, 0)),
            pl.BlockSpec((None, D_EXPERT, D_MODEL), lambda i, be, nu: (be[i], 0, 0)),
        ],
        out_specs=rows_spec,
        scratch_shapes=[
            pltpu.VMEM((D_MODEL, D_EXPERT), BF16),
            pltpu.VMEM((D_MODEL, D_EXPERT), BF16),
            pltpu.VMEM((D_EXPERT, D_MODEL), BF16),
        ],
    )
    return pl.pallas_call(
        _expert_kernel,
        out_shape=jax.ShapeDtypeStruct(xs.shape, F32),
        grid_spec=grid_spec,
        compiler_params=_params(("arbitrary",)),
        name="experts",
    )(blk_e, n_used, xs, w_gate, w_up, w_down)


def _combine_kernel(pos_ref, rows_hbm, x1_ref, route_ref, mod_ref, g_ref, b_ref, y_ref, buf, sem):
    i = pl.program_id(0)
    n_tiles = pl.num_programs(0)

    def issue(tile, slot):
        def body(r, carry):
            base = (tile * TM + r) * 2
            for k in range(2):
                pltpu.make_async_copy(rows_hbm.at[_slab(pos_ref[base + k])],
                                      buf.at[slot, k, _slab(r)], sem.at[slot]).start(priority=k)
            return carry
        lax.fori_loop(0, TM, body, 0, unroll=8)

    @pl.when(i == 0)
    def _():
        issue(0, 0)

    @pl.when(i + 1 < n_tiles)
    def _():
        issue(i + 1, (i + 1) % 2)

    slot = i % 2
    for k in range(2):
        pltpu.make_async_copy(rows_hbm.at[pl.ds(0, TM * ROW_SLAB)], buf.at[slot, k],
                              sem.at[slot]).wait()
    route = route_ref[...]
    moe = (route[:, INFO_W:INFO_W + 1] * _load_rows(buf.at[slot, 0], TM)
           + route[:, INFO_W + 1:INFO_W + 2] * _load_rows(buf.at[slot, 1], TM))
    y_ref[...] = _layer_norm(DEEPNORM_ALPHA * x1_ref[...] + mod_ref[5:6, :] * moe,
                             g_ref[...], b_ref[...])


def _combine(rows, pos, x1, route, mods, mod_row, tpb, ln_g, ln_b):
    n = x1.shape[0]
    tile = lambda i, p: (i, 0)
    grid_spec = pltpu.PrefetchScalarGridSpec(
        num_scalar_prefetch=1,
        grid=(n // TM,),
        in_specs=[
            pl.BlockSpec(memory_space=pl.ANY),
            pl.BlockSpec((TM, D_MODEL), tile),
            pl.BlockSpec((TM, ROUTE_LANES), tile),
            pl.BlockSpec((None, 6, D_MODEL), lambda i, p: (mod_row(i // tpb), 0, 0)),
            pl.BlockSpec((1, D_MODEL), lambda i, p: (0, 0)),
            pl.BlockSpec((1, D_MODEL), lambda i, p: (0, 0)),
        ],
        out_specs=pl.BlockSpec((TM, D_MODEL), tile),
        scratch_shapes=[pltpu.VMEM((2, 2, TM * ROW_SLAB, V7X_LANES), F32),
                        pltpu.SemaphoreType.DMA((2,))],
    )
    return pl.pallas_call(
        _combine_kernel,
        out_shape=jax.ShapeDtypeStruct((n, D_MODEL), F32),
        grid_spec=grid_spec,
        compiler_params=_params(("arbitrary",)),
        name="combine",
    )(pos, rows, x1, route, mods, ln_g.reshape(1, D_MODEL), ln_b.reshape(1, D_MODEL))


def _diff_lambda_init(layer_idx):
    return 0.8 - 0.6 * math.exp(-0.3 * layer_idx)


def _router_weights(w_rg, b_rg, w_re, b_re):
    pad = ROUTE_LANES - N_EXPERTS - N_GROUPS
    w = jnp.concatenate([w_re, w_rg, jnp.zeros((D_MODEL, pad), F32)], axis=1)
    b = jnp.concatenate([b_re, b_rg, jnp.zeros((pad,), F32)]).reshape(1, ROUTE_LANES)
    return w.astype(BF16), b


def kernel(x_prompt, x_sample, cache_a_k, cache_a_v, cache_b_k, cache_b_v, c, c_ctx, w_ada, b_ada, ln_g, ln_b, w_qkv_a, lam_a, subln_a, w_o_a, w_qkv_b, rpb_b, w_o_b, w_rg, b_rg, w_re, b_re, w_gate, w_up, w_down):
    n_mod_rows = 16
    dec_batch = x_sample.shape[0]
    cond = jnp.zeros((n_mod_rows, D_MODEL), F32).at[0].set(c_ctx).at[1:1 + dec_batch].set(c)
    mods_all = _adaln_all(cond, w_ada, b_ada).reshape(DEPTH, n_mod_rows, 6, D_MODEL)
    rope_tables = _rope_tables(x_sample.shape[1])

    xs_in = [x_prompt, x_sample]
    mod_rows = [lambda b: 0, lambda b: 1 + b]
    kv = []
    for i in range(DEPTH):
        mods = mods_all[i]
        j = i // 2
        diff_layer = i % 2 == 0
        if diff_layer:
            w_qkv, w_o = w_qkv_a[j].astype(BF16), w_o_a[j].astype(BF16)
            lam_init = _diff_lambda_init(i)
        else:
            w_qkv, w_o = w_qkv_b[j].astype(BF16), w_o_b[j].astype(BF16)
        w_router, b_router = _router_weights(w_rg[i], b_rg[i], w_re[i], b_re[i])
        routed = []
        for x, mod_row, latent in zip(xs_in, mod_rows, (False, True)):
            B = x.shape[0]
            if diff_layer and latent:
                q, k, v = _qkv(x, mods, w_qkv, mod_row, BF16, rope_tables)
                segs = [(cache_a_k[:, j].reshape(B, -1, D_MODEL),
                         cache_a_v[:, j].reshape(B, -1, D_MODEL)), (k, v)]
                o = _diff_attention(q, segs, lam_a[j], subln_a[j], lam_init)
            elif diff_layer:
                q, k, v = _qkv(x, mods, w_qkv, mod_row, F32)
                kv.append((k, v))
                o = _diff_attention(q, [(k, v)], lam_a[j], subln_a[j], lam_init)
            elif latent:
                q, k, v = _qkv(x, mods, w_qkv, mod_row, BF16)
                o = _neighborhood_attention(q, k, v, cache_b_k[:, j].reshape(B, -1, D_MODEL),
                                            cache_b_v[:, j].reshape(B, -1, D_MODEL), rpb_b[j])
            else:
                q, k, v = _qkv(x, mods, w_qkv, mod_row, F32)
                kv.append((k, v))
                o = _dense_attention(q, k, v)
            routed.append(_proj_route(o, w_o, x, mods, mod_row, ln_g[i, 0], ln_b[i, 0],
                                      w_router, b_router))
        x1s, infos, counts = zip(*routed)
        pos, blk_e, meta, n_used, n_blocks = _route_tables(infos, counts, i)
        tpbs = [x.shape[1] // TM for x in xs_in]
        rows_in = _dispatch(x1s[0], x1s[1], jnp.concatenate(pos), meta, mods,
                            mod_rows[0], tpbs[0], mod_rows[1], tpbs[1], n_blocks)
        rows = _experts(rows_in, blk_e, n_used, n_blocks, w_gate, w_up, w_down)
        xs_in = [_combine(rows, p, x1, info, mods, mod_row, tpb, ln_g[i, 1], ln_b[i, 1]).reshape(x.shape)
                 for p, x1, info, mod_row, tpb, x in zip(pos, x1s, infos, mod_rows, tpbs, xs_in)]
    y_prompt, y_sample = xs_in

    B, T, _ = x_prompt.shape
    new_a_k = kv[0][0].reshape(B, 1, T, 2 * H_A, HD_A)
    new_a_v = kv[0][1].reshape(B, 1, T, H_A, 2 * HD_A)
    new_b_k = kv[1][0].reshape(B, 1, T, H_B, HD_B)
    new_b_v = kv[1][1].reshape(B, 1, T, H_B, HD_B)
    return y_prompt, y_sample, new_a_k, new_a_v, new_b_k, new_b_v
```

```python
import functools
import math

import jax
import jax.numpy as jnp
import numpy as np
from jax import lax
from jax.experimental import pallas as pl
from jax.experimental.pallas import tpu as pltpu

D_MODEL = 1024
DEPTH = 2
GRID_W = 64
HD_A = 64
H_A = D_MODEL // (2 * HD_A)
H_B = 16
HD_B = D_MODEL // H_B
NA_KH = 8
NA_KW = 16
ROPE_BASE = 10000.0
N_GROUPS = 4
EXP_PER_GROUP = 8
N_EXPERTS = N_GROUPS * EXP_PER_GROUP
D_EXPERT = 512
LN_EPS = 1e-5
DEEPNORM_ALPHA = (2.0 * DEPTH) ** 0.25

F32 = jnp.float32
BF16 = jnp.bfloat16

V7X_LANES = 128
V7X_VMEM_BYTES = 64 * 1024 * 1024
VMEM_LIMIT = V7X_VMEM_BYTES * 7 // 8

TM = 256
TQ = 256
MOE_BM = 256
N_PAIRS = D_MODEL // V7X_LANES
NA_QROWS = TQ // GRID_W
NA_WIN_ROWS = NA_QROWS + NA_KH
NA_WIN = NA_WIN_ROWS * GRID_W
MASK_VALUE = -1e30
ROUTE_LANES = 128


def _params(sem):
    return pltpu.CompilerParams(dimension_semantics=sem, vmem_limit_bytes=VMEM_LIMIT)


def _layer_norm(y, g, b):
    mu = jnp.mean(y, -1, keepdims=True)
    d = y - mu
    var = jnp.mean(d * d, -1, keepdims=True)
    return d * lax.rsqrt(var + LN_EPS) * g + b


def _dot_nt(a, b):
    return lax.dot_general(a, b, (((1,), (1,)), ((), ())), preferred_element_type=F32)


def _ada_kernel(c_ref, w_ref, b_ref, o_ref):
    c = c_ref[...]
    a = c / (1.0 + jnp.exp(-c))
    o_ref[...] = jnp.dot(a, w_ref[...], preferred_element_type=F32,
                         precision=lax.Precision.HIGHEST) + b_ref[...]


def _adaln_all(cond, w_ada, b_ada):
    rows = cond.shape[0]
    tn = 1536
    n6 = 6 * D_MODEL
    return pl.pallas_call(
        _ada_kernel,
        out_shape=jax.ShapeDtypeStruct((DEPTH, rows, n6), F32),
        grid=(DEPTH, n6 // tn),
        in_specs=[
            pl.BlockSpec((rows, D_MODEL), lambda l, j: (0, 0)),
            pl.BlockSpec((None, D_MODEL, tn), lambda l, j: (l, 0, j)),
            pl.BlockSpec((None, 1, tn), lambda l, j: (l, 0, j)),
        ],
        out_specs=pl.BlockSpec((None, rows, tn), lambda l, j: (l, 0, j)),
        compiler_params=_params(("arbitrary", "arbitrary")),
        name="adaln",
    )(cond, w_ada, b_ada.reshape(DEPTH, 1, n6))


def _qkv_kernel(*refs, rope):
    if rope:
        x_ref, mod_ref, w_ref, cos_ref, sin_ref, q_ref, k_ref, v_ref = refs
    else:
        x_ref, mod_ref, w_ref, q_ref, k_ref, v_ref = refs
    h = (x_ref[...] * (1.0 + mod_ref[1:2, :]) + mod_ref[0:1, :]).astype(BF16)
    if rope:
        lane = lax.broadcasted_iota(jnp.int32, (TM, V7X_LANES), 1)
        first = (lane & (HD_A // 2 - 1)) < (HD_A // 4)
        cos = cos_ref[...]
        sin = sin_ref[...]
    for idx, o_ref in enumerate((q_ref, k_ref, v_ref)):
        y = jnp.dot(h, w_ref[:, idx * D_MODEL:(idx + 1) * D_MODEL], preferred_element_type=F32)
        if rope and idx < 2:
            for c in range(N_PAIRS):
                sl = slice(c * V7X_LANES, (c + 1) * V7X_LANES)
                yc = y[:, sl]
                partner = jnp.where(first, pltpu.roll(yc, V7X_LANES - HD_A // 4, 1),
                                    pltpu.roll(yc, HD_A // 4, 1))
                o_ref[:, sl] = (yc * cos + partner * sin).astype(o_ref.dtype)
        else:
            o_ref[...] = y.astype(o_ref.dtype)


def _qkv(x, mods, w_bf16, mod_row, kv_dtype, rope_tables=None):
    B, T, _ = x.shape
    n = B * T
    tpb = T // TM
    rope = rope_tables is not None
    in_specs = [
        pl.BlockSpec((TM, D_MODEL), lambda i: (i, 0)),
        pl.BlockSpec((None, 6, D_MODEL), lambda i: (mod_row(i // tpb), 0, 0)),
        pl.BlockSpec((D_MODEL, 3 * D_MODEL), lambda i: (0, 0)),
    ]
    args = [x.reshape(n, D_MODEL), mods, w_bf16]
    if rope:
        in_specs += [pl.BlockSpec((TM, V7X_LANES), lambda i: (i % tpb, 0))] * 2
        args += list(rope_tables)
    q, k, v = pl.pallas_call(
        functools.partial(_qkv_kernel, rope=rope),
        out_shape=(jax.ShapeDtypeStruct((n, D_MODEL), BF16),
                   jax.ShapeDtypeStruct((n, D_MODEL), kv_dtype),
                   jax.ShapeDtypeStruct((n, D_MODEL), kv_dtype)),
        grid=(n // TM,),
        in_specs=in_specs,
        out_specs=(pl.BlockSpec((TM, D_MODEL), lambda i: (i, 0)),) * 3,
        compiler_params=_params(("arbitrary",)),
        name="qkv",
    )(*args)
    shp = (B, T, D_MODEL)
    return q.reshape(shp), k.reshape(shp), v.reshape(shp)


def _rope_tables(n_tok):
    t = np.arange(n_tok)
    pos = jnp.asarray(np.stack([t // GRID_W, t % GRID_W], -1), F32)
    quarter = HD_A // 4
    inv_freq = ROPE_BASE ** (-jnp.arange(quarter, dtype=F32) / quarter)
    ang = pos[:, :, None] * inv_freq
    cos, sin = jnp.cos(ang), jnp.sin(ang)
    cos_h = jnp.broadcast_to(cos[:, :, None, :], (n_tok, 2, 2, quarter)).reshape(n_tok, HD_A)
    sin_h = jnp.stack([-sin, sin], axis=2).reshape(n_tok, HD_A)
    reps = V7X_LANES // HD_A
    return jnp.tile(cos_h, (1, reps)), jnp.tile(sin_h, (1, reps))


def _pair_attention(qc, key_chunks, value_chunks, biases=None):
    lo = lax.broadcasted_iota(jnp.int32, qc.shape, 1) < V7X_LANES // 2
    zero = jnp.zeros_like(qc)
    qq = jnp.concatenate([jnp.where(lo, qc, zero), jnp.where(lo, zero, qc)], axis=0)
    scores = [_dot_nt(qq, kc) for kc in key_chunks]
    if biases is not None:
        scores = [s if b is None else s + b for s, b in zip(scores, biases)]
    m = functools.reduce(jnp.maximum, [jnp.max(s, -1, keepdims=True) for s in scores])
    acc = None
    for s, vc in zip(scores, value_chunks):
        p = jnp.exp(s - m).astype(BF16)
        v1 = jnp.concatenate([vc, jnp.ones_like(vc)], axis=1)
        part = jnp.dot(p, v1, preferred_element_type=F32)
        acc = part if acc is None else acc + part
    return acc[:, :V7X_LANES] / acc[:, V7X_LANES:]


def _diff_attn_kernel(*refs, nseg, lam_init):
    lam_ref, subln_ref, q_ref = refs[:3]
    seg_refs = refs[3:3 + 2 * nseg]
    o_ref = refs[3 + 2 * nseg]
    lp = lam_ref[...]
    lam = (jnp.exp(jnp.sum(lp[0:1] * lp[1:2], keepdims=True))
           - jnp.exp(jnp.sum(lp[2:3] * lp[3:4], keepdims=True)) + lam_init)
    scale = HD_A ** -0.5
    for h in range(H_A):
        sl = slice(h * V7X_LANES, (h + 1) * V7X_LANES)
        o2 = _pair_attention(q_ref[:, sl] * scale,
                             [seg_refs[2 * j][:, sl].astype(BF16) for j in range(nseg)],
                             [seg_refs[2 * j + 1][:, sl].astype(BF16) for j in range(nseg)])
        acc = o2[:TQ] - lam * o2[TQ:]
        ms = jnp.mean(acc * acc, -1, keepdims=True)
        o = acc * lax.rsqrt(ms + LN_EPS) * subln_ref[...] * (1.0 - lam_init)
        o_ref[:, sl] = o.astype(BF16)


def _diff_attention(q, segs, lam_p, subln, lam_init):
    B, T, _ = q.shape
    in_specs = [
        pl.BlockSpec((4, HD_A), lambda b, i: (0, 0)),
        pl.BlockSpec((1, 2 * HD_A), lambda b, i: (0, 0)),
        pl.BlockSpec((None, TQ, D_MODEL), lambda b, i: (b, i, 0)),
    ]
    args = [lam_p, subln.reshape(1, 2 * HD_A), q]
    for k, v in segs:
        tk = k.shape[1]
        in_specs += [pl.BlockSpec((None, tk, D_MODEL), lambda b, i: (b, 0, 0))] * 2
        args += [k, v]
    return pl.pallas_call(
        functools.partial(_diff_attn_kernel, nseg=len(segs), lam_init=lam_init),
        out_shape=jax.ShapeDtypeStruct((B, T, D_MODEL), BF16),
        grid=(B, T // TQ),
        in_specs=in_specs,
        out_specs=pl.BlockSpec((None, TQ, D_MODEL), lambda b, i: (b, i, 0)),
        compiler_params=_params(("arbitrary", "arbitrary")),
        name="diff_attn",
    )(*args)


def _merge_pair(o2):
    tq = o2.shape[0] // 2
    lo = lax.broadcasted_iota(jnp.int32, (tq, V7X_LANES), 1) < V7X_LANES // 2
    return jnp.where(lo, o2[:tq], o2[tq:])


def _dense_attn_kernel(q_ref, k_ref, v_ref, o_ref):
    scale = HD_B ** -0.5
    for c in range(N_PAIRS):
        sl = slice(c * V7X_LANES, (c + 1) * V7X_LANES)
        o2 = _pair_attention(q_ref[:, sl] * scale, [k_ref[:, sl].astype(BF16)],
                             [v_ref[:, sl].astype(BF16)])
        o_ref[:, sl] = _merge_pair(o2).astype(BF16)


def _dense_attention(q, k, v):
    B, T, _ = q.shape
    spec = pl.BlockSpec((None, T, D_MODEL), lambda b: (b, 0, 0))
    return pl.pallas_call(
        _dense_attn_kernel,
        out_shape=jax.ShapeDtypeStruct((B, T, D_MODEL), BF16),
        grid=(B,),
        in_specs=[spec, spec, spec],
        out_specs=spec,
        compiler_params=_params(("arbitrary",)),
        name="dense_attn",
    )(q, k, v)


NA_WIN_BLOCKS = NA_WIN // TQ


def _na_attn_kernel(*refs):
    q_ref = refs[0]
    k_refs = refs[1:1 + NA_WIN_BLOCKS]
    v_refs = refs[1 + NA_WIN_BLOCKS:1 + 2 * NA_WIN_BLOCKS]
    kc_ref, vc_ref, bias_ref, o_ref = refs[1 + 2 * NA_WIN_BLOCKS:]
    scale = HD_B ** -0.5
    for c in range(N_PAIRS):
        sl = slice(c * V7X_LANES, (c + 1) * V7X_LANES)
        kwin = jnp.concatenate([r[:, sl] for r in k_refs], axis=0)
        vwin = jnp.concatenate([r[:, sl] for r in v_refs], axis=0)
        bias = bias_ref[2 * c:2 * c + 2].reshape(2 * TQ, NA_WIN)
        o2 = _pair_attention(q_ref[:, sl] * scale,
                             [kwin, kc_ref[:, sl].astype(BF16)],
                             [vwin, vc_ref[:, sl].astype(BF16)],
                             biases=[bias, None])
        o_ref[:, sl] = _merge_pair(o2).astype(BF16)


def _na_bias_tables(rpb, rows):
    kh = min(NA_KH, rows)
    n_heads = rpb.shape[0]
    qcol = np.arange(GRID_W)[:, None]
    kcol = np.arange(GRID_W)[None, :]
    c_start = np.clip(qcol - NA_KW // 2, 0, GRID_W - NA_KW)
    col_ok = (kcol >= c_start) & (kcol < c_start + NA_KW)
    col_off = np.clip(kcol - qcol + NA_KW - 1, 0, 2 * NA_KW - 2)
    onehot = (col_off[None] == np.arange(2 * NA_KW - 1)[:, None, None]).astype(np.float32)
    cols = jnp.einsum('hrc,cqk->hrqk', rpb, jnp.asarray(onehot), precision=lax.Precision.HIGHEST)
    cols = jnp.where(col_ok[None, None], cols, MASK_VALUE)
    padded = jnp.pad(cols, ((0, 0), (1, 1), (0, 0), (0, 0)), constant_values=MASK_VALUE)
    pairs = jnp.concatenate([padded[:, :-1], padded[:, 1:]], axis=-1)
    plan = []
    for r0 in (0, NA_QROWS, rows - NA_QROWS):
        ws = int(np.clip(r0 - NA_KH // 2, 0, rows - NA_WIN_ROWS))
        for qrow in range(r0, r0 + NA_QROWS):
            r_start = int(np.clip(qrow - kh // 2, 0, rows - kh))
            chunks = []
            for c in range(NA_WIN_ROWS // 2):
                krow = ws + 2 * c
                in_band = [r_start <= krow + s < r_start + kh for s in (0, 1)]
                chunks.append((krow - qrow + NA_KH - 1, in_band[0], in_band[1]))
            plan.append(chunks)
    return pl.pallas_call(
        functools.partial(_na_bias_kernel, plan=plan),
        out_shape=jax.ShapeDtypeStruct((3, n_heads, TQ, NA_WIN), F32),
        grid=(n_heads,),
        in_specs=[pl.BlockSpec((None, 2 * NA_KH, GRID_W, 2 * GRID_W), lambda h: (h, 0, 0, 0))],
        out_specs=pl.BlockSpec((3, None, TQ, NA_WIN), lambda h: (0, h, 0, 0)),
        compiler_params=_params(("arbitrary",)),
        name="na_bias",
    )(pairs)


def _na_bias_kernel(pairs_ref, o_ref, *, plan):
    left = lax.broadcasted_iota(jnp.int32, (GRID_W, 2 * GRID_W), 1) < GRID_W
    masked = jnp.full((GRID_W, 2 * GRID_W), MASK_VALUE, F32)
    for t, chunks in enumerate(plan):
        table, qrow = divmod(t, NA_QROWS)
        for c, (row_off, left_ok, right_ok) in enumerate(chunks):
            if left_ok or right_ok:
                piece = pairs_ref[row_off + 1]
                if not left_ok:
                    piece = jnp.where(left, masked, piece)
                if not right_ok:
                    piece = jnp.where(left, piece, masked)
            else:
                piece = masked
            o_ref[table, qrow * GRID_W:(qrow + 1) * GRID_W,
                  c * 2 * GRID_W:(c + 1) * 2 * GRID_W] = piece


def _neighborhood_attention(q, k, v, k_ctx, v_ctx, rpb):
    B, S, _ = q.shape
    P = k_ctx.shape[1]
    rows = S // GRID_W
    nqb = S // TQ
    assert rows >= NA_WIN_ROWS and nqb >= NA_WIN_BLOCKS
    bias = _na_bias_tables(rpb, rows)

    def bias_cfg(qb):
        return jnp.where(qb == 0, 0, jnp.where(qb == nqb - 1, 2, 1))

    def win_spec(j):
        return pl.BlockSpec((None, TQ, D_MODEL),
                            lambda qb, b: (b, jnp.clip(qb - 1, 0, nqb - NA_WIN_BLOCKS) + j, 0))

    win_specs = [win_spec(j) for j in range(NA_WIN_BLOCKS)]
    ctx_spec = pl.BlockSpec((None, P, D_MODEL), lambda qb, b: (b, 0, 0))
    q_spec = pl.BlockSpec((None, TQ, D_MODEL), lambda qb, b: (b, qb, 0))
    bias_spec = pl.BlockSpec((None, H_B, TQ, NA_WIN), lambda qb, b: (bias_cfg(qb), 0, 0, 0),
                             pipeline_mode=pl.Buffered(1))
    return pl.pallas_call(
        _na_attn_kernel,
        out_shape=jax.ShapeDtypeStruct((B, S, D_MODEL), BF16),
        grid=(nqb, B),
        in_specs=[q_spec] + win_specs + win_specs + [ctx_spec, ctx_spec, bias_spec],
        out_specs=q_spec,
        compiler_params=_params(("arbitrary", "arbitrary")),
        name="na_attn",
    )(q, *([k] * NA_WIN_BLOCKS), *([v] * NA_WIN_BLOCKS), k_ctx, v_ctx, bias)


INFO_E, INFO_W, INFO_RANK = 0, 2, 4


ROUTE_ROWS = 64


def _route(lt):
    sub = lax.broadcasted_iota(jnp.int32, lt.shape, 0).astype(F32)
    neg = -jnp.inf
    gmask = (sub >= N_EXPERTS) & (sub < N_EXPERTS + N_GROUPS)
    gl = jnp.where(gmask, lt, neg)
    gmax = jnp.max(gl, 0, keepdims=True)
    g_sel = jnp.min(jnp.where(gl == gmax, sub - N_EXPERTS, float(N_GROUPS)), 0, keepdims=True)
    p_g = 1.0 / jnp.sum(jnp.where(gmask, jnp.exp(gl - gmax), 0.0), 0, keepdims=True)
    emask = (sub >= g_sel * EXP_PER_GROUP) & (sub < (g_sel + 1.0) * EXP_PER_GROUP)
    el = jnp.where(emask, lt, neg)
    v1 = jnp.max(el, 0, keepdims=True)
    e1 = jnp.min(jnp.where(el == v1, sub, float(ROUTE_ROWS)), 0, keepdims=True)
    el2 = jnp.where(sub == e1, neg, el)
    v2 = jnp.max(el2, 0, keepdims=True)
    e2 = jnp.min(jnp.where(el2 == v2, sub, float(ROUTE_ROWS)), 0, keepdims=True)
    e21 = jnp.exp(v2 - v1)
    w1 = p_g / (1.0 + e21)
    w2 = p_g * e21 / (1.0 + e21)
    return sub, e1, e2, w1, w2


def _proj_route_kernel(o_ref, w_ref, x_ref, mod_ref, g_ref, b_ref, wr_ref, br_ref,
                       x1_ref, info_ref, cnt_ref, run):
    @pl.when(pl.program_id(0) == 0)
    def _():
        run[...] = jnp.zeros_like(run)

    out = jnp.dot(o_ref[...], w_ref[...], preferred_element_type=F32)
    x1 = _layer_norm(DEEPNORM_ALPHA * x_ref[...] + mod_ref[2:3, :] * out, g_ref[...], b_ref[...])
    x1_ref[...] = x1
    h2 = x1 * (1.0 + mod_ref[4:5, :]) + mod_ref[3:4, :]
    logits = jnp.dot(h2.astype(BF16), wr_ref[...], preferred_element_type=F32) + br_ref[...]
    sub, e1, e2, w1, w2 = _route(logits.T[:ROUTE_ROWS])

    oh1 = sub == e1
    oh2 = sub == e2
    oh = jnp.where(oh1 | oh2, 1.0, 0.0)
    t_row = lax.broadcasted_iota(jnp.int32, (TM, TM), 0)
    t_col = lax.broadcasted_iota(jnp.int32, (TM, TM), 1)
    earlier = jnp.where(t_row < t_col, 1.0, 0.0).astype(BF16)
    base = run[:, 0:1] + jnp.dot(oh.astype(BF16), earlier, preferred_element_type=F32)
    rank1 = jnp.sum(jnp.where(oh1, base, 0.0), 0, keepdims=True)
    rank2 = jnp.sum(jnp.where(oh2, base, 0.0), 0, keepdims=True)
    run[...] = run[...] + jnp.sum(oh, 1, keepdims=True)
    cnt_ref[...] = run[...]

    rec_row = lax.broadcasted_iota(jnp.int32, (ROUTE_LANES, TM), 0)
    rec = jnp.zeros((ROUTE_LANES, TM), F32)
    for row, val in ((INFO_E, e1), (INFO_E + 1, e2), (INFO_W, w1), (INFO_W + 1, w2),
                     (INFO_RANK, rank1), (INFO_RANK + 1, rank2)):
        rec = jnp.where(rec_row == row, val, rec)
    info_ref[...] = rec.T


def _proj_route(o, w_o_bf16, x, mods, mod_row, ln_g, ln_b, w_router, b_router):
    B, T, _ = x.shape
    n = B * T
    tpb = T // TM
    tile = pl.BlockSpec((TM, D_MODEL), lambda i: (i, 0))
    vec = pl.BlockSpec((1, D_MODEL), lambda i: (0, 0))
    return pl.pallas_call(
        _proj_route_kernel,
        out_shape=(jax.ShapeDtypeStruct((n, D_MODEL), F32),
                   jax.ShapeDtypeStruct((n, ROUTE_LANES), F32),
                   jax.ShapeDtypeStruct((ROUTE_ROWS, ROUTE_LANES), F32)),
        grid=(n // TM,),
        in_specs=[tile,
                  pl.BlockSpec((D_MODEL, D_MODEL), lambda i: (0, 0)),
                  tile,
                  pl.BlockSpec((None, 6, D_MODEL), lambda i: (mod_row(i // tpb), 0, 0)),
                  vec, vec,
                  pl.BlockSpec((D_MODEL, ROUTE_LANES), lambda i: (0, 0)),
                  pl.BlockSpec((1, ROUTE_LANES), lambda i: (0, 0))],
        out_specs=(tile,
                   pl.BlockSpec((TM, ROUTE_LANES), lambda i: (i, 0)),
                   pl.BlockSpec((ROUTE_ROWS, ROUTE_LANES), lambda i: (0, 0))),
        scratch_shapes=[pltpu.VMEM((ROUTE_ROWS, ROUTE_LANES), F32)],
        compiler_params=_params(("arbitrary",)),
        name="proj_route",
    )(o.reshape(n, D_MODEL), w_o_bf16, x.reshape(n, D_MODEL), mods,
      ln_g.reshape(1, D_MODEL), ln_b.reshape(1, D_MODEL), w_router, b_router)


def _route_tables(infos, counts, layer):
    cnts = [c[:N_EXPERTS, 0].astype(jnp.int32) for c in counts]
    cnt = functools.reduce(lambda a, b: a + b, cnts)
    padded = (cnt + MOE_BM - 1) // MOE_BM * MOE_BM
    pend = jnp.cumsum(padded)
    n_pairs = 2 * sum(info.shape[0] for info in infos)
    n_blocks = n_pairs // MOE_BM + N_EXPERTS
    n_used = pend[-1:] // MOE_BM
    experts = jnp.arange(N_EXPERTS, dtype=jnp.int32)
    pos = []
    first = pend - padded
    for info, c in zip(infos, cnts):
        e = info[:, INFO_E:INFO_E + 2].astype(jnp.int32)
        rank = info[:, INFO_RANK:INFO_RANK + 2].astype(jnp.int32)
        p = jnp.sum(jnp.where(e[..., None] == experts, first, 0), -1) + rank
        pos.append(p.reshape(-1).astype(jnp.int32))
        first = first + c
    blk = jnp.minimum(jnp.arange(n_blocks, dtype=jnp.int32), n_used - 1) * MOE_BM
    blk_e = jnp.sum((blk[:, None] >= pend[None, :]).astype(jnp.int32), axis=1) + layer * N_EXPERTS
    meta = jnp.concatenate([first, padded - cnt, n_used]).astype(jnp.int32)
    return pos, blk_e.astype(jnp.int32), meta, n_used, n_blocks


ROW_SLAB = D_MODEL // V7X_LANES


def _slab(row, n_rows=1):
    return pl.ds(pl.multiple_of(row * ROW_SLAB, ROW_SLAB), n_rows * ROW_SLAB)


def _store_rows(ref, x):
    rows = x.shape[0]
    for c in range(ROW_SLAB):
        ref[pl.ds(c, rows, stride=ROW_SLAB), :] = x[:, c * V7X_LANES:(c + 1) * V7X_LANES]


def _load_rows(ref, rows):
    return jnp.concatenate([ref[pl.ds(c, rows, stride=ROW_SLAB), :] for c in range(ROW_SLAB)], axis=1)


def _dispatch_kernel(pos_ref, meta_ref, x1a_ref, x1b_ref, mod_ref, xs_hbm, hbuf, zbuf, sem, sem_z, *,
                     n_blocks, tiles_a):
    i = pl.program_id(0)
    n_tiles = pl.num_programs(0)
    slot = i % 2

    @pl.when(i == 0)
    def _():
        zbuf[...] = jnp.zeros_like(zbuf)

    def wait_slot(s):
        for _ in range(2):
            pltpu.make_async_copy(hbuf.at[s], xs_hbm.at[pl.ds(0, TM * ROW_SLAB)], sem.at[s]).wait()

    @pl.when(i >= 2)
    def _():
        wait_slot(slot)

    @pl.when(i < tiles_a)
    def _():
        _store_rows(hbuf.at[slot], x1a_ref[...] * (1.0 + mod_ref[4:5, :]) + mod_ref[3:4, :])

    @pl.when(i >= tiles_a)
    def _():
        _store_rows(hbuf.at[slot], x1b_ref[...] * (1.0 + mod_ref[4:5, :]) + mod_ref[3:4, :])

    def scatter_row(r, carry):
        base = (i * TM + r) * 2
        src = hbuf.at[slot, _slab(r)]
        for k in range(2):
            pltpu.make_async_copy(src, xs_hbm.at[_slab(pos_ref[base + k])],
                                  sem.at[slot]).start(priority=k)
        return carry
    lax.fori_loop(0, TM, scatter_row, 0, unroll=8)

    @pl.when(i == n_tiles - 1)
    def _():
        def zero_copy(row, size):
            return pltpu.make_async_copy(zbuf.at[pl.ds(0, size * ROW_SLAB)],
                                         xs_hbm.at[_slab(row, size)], sem_z)

        def pad_expert(e, carry):
            start = meta_ref[e]
            length = meta_ref[N_EXPERTS + e]
            size = MOE_BM // 2
            while size >= 1:
                @pl.when((length & size) != 0)
                def _(size=size):
                    zero_copy(start + (length & ~(2 * size - 1)), size).start()
                size //= 2
            return carry
        lax.fori_loop(0, N_EXPERTS, pad_expert, 0)

        def tail_block(blk, carry):
            zero_copy(blk * MOE_BM, MOE_BM).start()
            return carry
        lax.fori_loop(meta_ref[2 * N_EXPERTS], n_blocks, tail_block, 0)

        wait_slot(slot)

        @pl.when(i >= 1)
        def _():
            wait_slot(1 - slot)

        for _ in range(N_EXPERTS):
            zero_copy(0, MOE_BM).wait()


def _dispatch(x1a, x1b, pos, meta, mods, mod_row_a, tpb_a, mod_row_b, tpb_b, n_blocks):
    tiles_a = x1a.shape[0] // TM
    tiles_b = x1b.shape[0] // TM
    assert n_blocks * MOE_BM - 2 * (tiles_a + tiles_b) * TM == N_EXPERTS * MOE_BM

    def mod_index(i, p, m):
        row = jnp.where(i < tiles_a, mod_row_a(i // tpb_a),
                        mod_row_b(jnp.maximum(i - tiles_a, 0) // tpb_b))
        return (row, 0, 0)

    grid_spec = pltpu.PrefetchScalarGridSpec(
        num_scalar_prefetch=2,
        grid=(tiles_a + tiles_b,),
        in_specs=[
            pl.BlockSpec((TM, D_MODEL), lambda i, p, m: (jnp.minimum(i, tiles_a - 1), 0)),
            pl.BlockSpec((TM, D_MODEL), lambda i, p, m: (jnp.maximum(i - tiles_a, 0), 0)),
            pl.BlockSpec((None, 6, D_MODEL), mod_index),
        ],
        out_specs=pl.BlockSpec(memory_space=pl.ANY),
        scratch_shapes=[
            pltpu.VMEM((2, TM * ROW_SLAB, V7X_LANES), F32),
            pltpu.VMEM((MOE_BM * ROW_SLAB, V7X_LANES), F32),
            pltpu.SemaphoreType.DMA((2,)),
            pltpu.SemaphoreType.DMA(()),
        ],
    )
    return pl.pallas_call(
        functools.partial(_dispatch_kernel, n_blocks=n_blocks, tiles_a=tiles_a),
        out_shape=jax.ShapeDtypeStruct((n_blocks * MOE_BM * ROW_SLAB, V7X_LANES), F32),
        grid_spec=grid_spec,
        compiler_params=_params(("arbitrary",)),
        name="dispatch",
    )(pos, meta, x1a, x1b, mods)


def _expert_kernel(be_ref, nused_ref, x_ref, wg_ref, wu_ref, wd_ref, o_ref, wg_s, wu_s, wd_s):
    i = pl.program_id(0)
    active = i < nused_ref[0]
    changed = (i == 0) | (be_ref[i] != be_ref[jnp.maximum(i - 1, 0)])

    @pl.when(active & changed)
    def _():
        wg_s[...] = wg_ref[...].astype(BF16)
        wu_s[...] = wu_ref[...].astype(BF16)
        wd_s[...] = wd_ref[...].astype(BF16)

    @pl.when(active)
    def _():
        xb = _load_rows(x_ref, MOE_BM).astype(BF16)
        g = jnp.dot(xb, wg_s[...], preferred_element_type=F32)
        u = jnp.dot(xb, wu_s[...], preferred_element_type=F32)
        hb = (g / (1.0 + jnp.exp(-g)) * u).astype(BF16)
        _store_rows(o_ref, jnp.dot(hb, wd_s[...], preferred_element_type=F32))

    @pl.when(jnp.logical_not(active))
    def _():
        o_ref[...] = jnp.zeros_like(o_ref)


def _experts(xs, blk_e, n_used, n_blocks, w_gate, w_up, w_down):
    w_gate = w_gate.reshape(DEPTH * N_EXPERTS, D_MODEL, D_EXPERT)
    w_up = w_up.reshape(DEPTH * N_EXPERTS, D_MODEL, D_EXPERT)
    w_down = w_down.reshape(DEPTH * N_EXPERTS, D_EXPERT, D_MODEL)
    rows_spec = pl.BlockSpec((MOE_BM * ROW_SLAB, V7X_LANES), lambda i, be, nu: (i, 0))
    grid_spec = pltpu.PrefetchScalarGridSpec(
        num_scalar_prefetch=2,
        grid=(n_blocks,),
        in_specs=[
            rows_spec,
            pl.BlockSpec((None, D_MODEL, D_EXPERT), lambda i, be, nu: (be[i], 0, 0)),
            pl.BlockSpec((None, D_MODEL, D_EXPERT), lambda i, be, nu: (be[i], 0, 0)),
            pl.BlockSpec((None, D_EXPERT, D_MODEL), lambda i, be, nu: (be[i], 0, 0)),
        ],
        out_specs=rows_spec,
        scratch_shapes=[
            pltpu.VMEM((D_MODEL, D_EXPERT), BF16),
            pltpu.VMEM((D_MODEL, D_EXPERT), BF16),
            pltpu.VMEM((D_EXPERT, D_MODEL), BF16),
        ],
    )
    return pl.pallas_call(
        _expert_kernel,
        out_shape=jax.ShapeDtypeStruct(xs.shape, F32),
        grid_spec=grid_spec,
        compiler_params=_params(("arbitrary",)),
        name="experts",
    )(blk_e, n_used, xs, w_gate, w_up, w_down)


def _combine_kernel(pos_ref, rows_hbm, x1_ref, route_ref, mod_ref, g_ref, b_ref, y_ref, buf, sem):
    i = pl.program_id(0)
    n_tiles = pl.num_programs(0)

    def issue_row(tile, slot, r):
        base = (tile * TM + r) * 2
        for k in range(2):
            pltpu.make_async_copy(rows_hbm.at[_slab(pos_ref[base + k])],
                                  buf.at[slot, k, _slab(r)], sem.at[slot]).start(priority=k)

    def wait_slot(s):
        for k in range(2):
            pltpu.make_async_copy(rows_hbm.at[pl.ds(0, TM * ROW_SLAB)], buf.at[s, k],
                                  sem.at[s]).wait()

    @pl.when(i == 0)
    def _():
        def body(r, carry):
            issue_row(0, 0, r)
            return carry
        lax.fori_loop(0, TM, body, 0, unroll=8)

    slot = i % 2
    wait_slot(slot)
    route = route_ref[...]
    moe = (route[:, INFO_W:INFO_W + 1] * _load_rows(buf.at[slot, 0], TM)
           + route[:, INFO_W + 1:INFO_W + 2] * _load_rows(buf.at[slot, 1], TM))
    y_ref[...] = _layer_norm(DEEPNORM_ALPHA * x1_ref[...] + mod_ref[5:6, :] * moe,
                             g_ref[...], b_ref[...])

    nxt = jnp.minimum(i + 1, n_tiles - 1)
    for r in range(TM):
        issue_row(nxt, 1 - slot, r)

    @pl.when(i == n_tiles - 1)
    def _():
        wait_slot(1 - slot)


def _combine(rows, pos, x1, route, mods, mod_row, tpb, ln_g, ln_b):
    n = x1.shape[0]
    tile = lambda i, p: (i, 0)
    grid_spec = pltpu.PrefetchScalarGridSpec(
        num_scalar_prefetch=1,
        grid=(n // TM,),
        in_specs=[
            pl.BlockSpec(memory_space=pl.ANY),
            pl.BlockSpec((TM, D_MODEL), tile),
            pl.BlockSpec((TM, ROUTE_LANES), tile),
            pl.BlockSpec((None, 6, D_MODEL), lambda i, p: (mod_row(i // tpb), 0, 0)),
            pl.BlockSpec((1, D_MODEL), lambda i, p: (0, 0)),
            pl.BlockSpec((1, D_MODEL), lambda i, p: (0, 0)),
        ],
        out_specs=pl.BlockSpec((TM, D_MODEL), tile),
        scratch_shapes=[pltpu.VMEM((2, 2, TM * ROW_SLAB, V7X_LANES), F32),
                        pltpu.SemaphoreType.DMA((2,))],
    )
    return pl.pallas_call(
        _combine_kernel,
        out_shape=jax.ShapeDtypeStruct((n, D_MODEL), F32),
        grid_spec=grid_spec,
        compiler_params=_params(("arbitrary",)),
        name="combine",
    )(pos, rows, x1, route, mods, ln_g.reshape(1, D_MODEL), ln_b.reshape(1, D_MODEL))


def _diff_lambda_init(layer_idx):
    return 0.8 - 0.6 * math.exp(-0.3 * layer_idx)


def _router_weights(w_rg, b_rg, w_re, b_re):
    pad = ROUTE_LANES - N_EXPERTS - N_GROUPS
    w = jnp.concatenate([w_re, w_rg, jnp.zeros((D_MODEL, pad), F32)], axis=1)
    b = jnp.concatenate([b_re, b_rg, jnp.zeros((pad,), F32)]).reshape(1, ROUTE_LANES)
    return w.astype(BF16), b


def kernel(x_prompt, x_sample, cache_a_k, cache_a_v, cache_b_k, cache_b_v, c, c_ctx, w_ada, b_ada, ln_g, ln_b, w_qkv_a, lam_a, subln_a, w_o_a, w_qkv_b, rpb_b, w_o_b, w_rg, b_rg, w_re, b_re, w_gate, w_up, w_down):
    n_mod_rows = 16
    dec_batch = x_sample.shape[0]
    cond = jnp.zeros((n_mod_rows, D_MODEL), F32).at[0].set(c_ctx).at[1:1 + dec_batch].set(c)
    mods_all = _adaln_all(cond, w_ada, b_ada).reshape(DEPTH, n_mod_rows, 6, D_MODEL)
    rope_tables = _rope_tables(x_sample.shape[1])

    xs_in = [x_prompt, x_sample]
    mod_rows = [lambda b: 0, lambda b: 1 + b]
    kv = []
    for i in range(DEPTH):
        mods = mods_all[i]
        j = i // 2
        diff_layer = i % 2 == 0
        if diff_layer:
            w_qkv, w_o = w_qkv_a[j].astype(BF16), w_o_a[j].astype(BF16)
            lam_init = _diff_lambda_init(i)
        else:
            w_qkv, w_o = w_qkv_b[j].astype(BF16), w_o_b[j].astype(BF16)
        w_router, b_router = _router_weights(w_rg[i], b_rg[i], w_re[i], b_re[i])
        routed = []
        for x, mod_row, latent in zip(xs_in, mod_rows, (False, True)):
            B = x.shape[0]
            if diff_layer and latent:
                q, k, v = _qkv(x, mods, w_qkv, mod_row, BF16, rope_tables)
                segs = [(cache_a_k[:, j].reshape(B, -1, D_MODEL),
                         cache_a_v[:, j].reshape(B, -1, D_MODEL)), (k, v)]
                o = _diff_attention(q, segs, lam_a[j], subln_a[j], lam_init)
            elif diff_layer:
                q, k, v = _qkv(x, mods, w_qkv, mod_row, F32)
                kv.append((k, v))
                o = _diff_attention(q, [(k, v)], lam_a[j], subln_a[j], lam_init)
            elif latent:
                q, k, v = _qkv(x, mods, w_qkv, mod_row, BF16)
                o = _neighborhood_attention(q, k, v, cache_b_k[:, j].reshape(B, -1, D_MODEL),
                                            cache_b_v[:, j].reshape(B, -1, D_MODEL), rpb_b[j])
            else:
                q, k, v = _qkv(x, mods, w_qkv, mod_row, F32)
                kv.append((k, v))
                o = _dense_attention(q, k, v)
            routed.append(_proj_route(o, w_o, x, mods, mod_row, ln_g[i, 0], ln_b[i, 0],
                                      w_router, b_router))
        x1s, infos, counts = zip(*routed)
        pos, blk_e, meta, n_used, n_blocks = _route_tables(infos, counts, i)
        tpbs = [x.shape[1] // TM for x in xs_in]
        rows_in = _dispatch(x1s[0], x1s[1], jnp.concatenate(pos), meta, mods,
                            mod_rows[0], tpbs[0], mod_rows[1], tpbs[1], n_blocks)
        rows = _experts(rows_in, blk_e, n_used, n_blocks, w_gate, w_up, w_down)
        xs_in = [_combine(rows, p, x1, info, mods, mod_row, tpb, ln_g[i, 1], ln_b[i, 1]).reshape(x.shape)
                 for p, x1, info, mod_row, tpb, x in zip(pos, x1s, infos, mod_rows, tpbs, xs_in)]
    y_prompt, y_sample = xs_in

    B, T, _ = x_prompt.shape
    new_a_k = kv[0][0].reshape(B, 1, T, 2 * H_A, HD_A)
    new_a_v = kv[0][1].reshape(B, 1, T, H_A, 2 * HD_A)
    new_b_k = kv[1][0].reshape(B, 1, T, H_B, HD_B)
    new_b_v = kv[1][1].reshape(B, 1, T, H_B, HD_B)
    return y_prompt, y_sample, new_a_k, new_a_v, new_b_k, new_b_v
```

```python
import functools
import math

import jax
import jax.numpy as jnp
import numpy as np
from jax import lax
from jax.experimental import pallas as pl
from jax.experimental.pallas import tpu as pltpu

D_MODEL = 1024
DEPTH = 2
GRID_W = 64
HD_A = 64
H_A = D_MODEL // (2 * HD_A)
H_B = 16
HD_B = D_MODEL // H_B
NA_KH = 8
NA_KW = 16
ROPE_BASE = 10000.0
N_GROUPS = 4
EXP_PER_GROUP = 8
N_EXPERTS = N_GROUPS * EXP_PER_GROUP
D_EXPERT = 512
LN_EPS = 1e-5
DEEPNORM_ALPHA = (2.0 * DEPTH) ** 0.25

F32 = jnp.float32
BF16 = jnp.bfloat16

V7X_LANES = 128
V7X_VMEM_BYTES = 64 * 1024 * 1024
VMEM_LIMIT = V7X_VMEM_BYTES * 7 // 8

TM = 256
TQ = 256
MOE_BM = 256
N_PAIRS = D_MODEL // V7X_LANES
NA_QROWS = TQ // GRID_W
NA_WIN_ROWS = NA_QROWS + NA_KH
NA_WIN = NA_WIN_ROWS * GRID_W
MASK_VALUE = -1e30
ROUTE_LANES = 128


def _params(sem):
    return pltpu.CompilerParams(dimension_semantics=sem, vmem_limit_bytes=VMEM_LIMIT)


def _layer_norm(y, g, b):
    mu = jnp.mean(y, -1, keepdims=True)
    d = y - mu
    var = jnp.mean(d * d, -1, keepdims=True)
    return d * lax.rsqrt(var + LN_EPS) * g + b


def _dot_nt(a, b):
    return lax.dot_general(a, b, (((1,), (1,)), ((), ())), preferred_element_type=F32)


def _ada_kernel(c_ref, w_ref, b_ref, o_ref):
    c = c_ref[...]
    a = c / (1.0 + jnp.exp(-c))
    o_ref[...] = jnp.dot(a, w_ref[...], preferred_element_type=F32,
                         precision=lax.Precision.HIGHEST) + b_ref[...]


def _adaln_all(cond, w_ada, b_ada):
    rows = cond.shape[0]
    tn = 1536
    n6 = 6 * D_MODEL
    return pl.pallas_call(
        _ada_kernel,
        out_shape=jax.ShapeDtypeStruct((DEPTH, rows, n6), F32),
        grid=(DEPTH, n6 // tn),
        in_specs=[
            pl.BlockSpec((rows, D_MODEL), lambda l, j: (0, 0)),
            pl.BlockSpec((None, D_MODEL, tn), lambda l, j: (l, 0, j)),
            pl.BlockSpec((None, 1, tn), lambda l, j: (l, 0, j)),
        ],
        out_specs=pl.BlockSpec((None, rows, tn), lambda l, j: (l, 0, j)),
        compiler_params=_params(("arbitrary", "arbitrary")),
        name="adaln",
    )(cond, w_ada, b_ada.reshape(DEPTH, 1, n6))


def _qkv_kernel(*refs, rope):
    if rope:
        x_ref, mod_ref, w_ref, cos_ref, sin_ref, q_ref, k_ref, v_ref = refs
    else:
        x_ref, mod_ref, w_ref, q_ref, k_ref, v_ref = refs
    h = (x_ref[...] * (1.0 + mod_ref[1:2, :]) + mod_ref[0:1, :]).astype(BF16)
    if rope:
        lane = lax.broadcasted_iota(jnp.int32, (TM, V7X_LANES), 1)
        first = (lane & (HD_A // 2 - 1)) < (HD_A // 4)
        cos = cos_ref[...]
        sin = sin_ref[...]
    for idx, o_ref in enumerate((q_ref, k_ref, v_ref)):
        y = jnp.dot(h, w_ref[:, idx * D_MODEL:(idx + 1) * D_MODEL], preferred_element_type=F32)
        if rope and idx < 2:
            for c in range(N_PAIRS):
                sl = slice(c * V7X_LANES, (c + 1) * V7X_LANES)
                yc = y[:, sl]
                partner = jnp.where(first, pltpu.roll(yc, V7X_LANES - HD_A // 4, 1),
                                    pltpu.roll(yc, HD_A // 4, 1))
                o_ref[:, sl] = (yc * cos + partner * sin).astype(o_ref.dtype)
        else:
            o_ref[...] = y.astype(o_ref.dtype)


def _qkv(x, mods, w_bf16, mod_row, kv_dtype, rope_tables=None):
    B, T, _ = x.shape
    n = B * T
    tpb = T // TM
    rope = rope_tables is not None
    in_specs = [
        pl.BlockSpec((TM, D_MODEL), lambda i: (i, 0)),
        pl.BlockSpec((None, 6, D_MODEL), lambda i: (mod_row(i // tpb), 0, 0)),
        pl.BlockSpec((D_MODEL, 3 * D_MODEL), lambda i: (0, 0)),
    ]
    args = [x.reshape(n, D_MODEL), mods, w_bf16]
    if rope:
        in_specs += [pl.BlockSpec((TM, V7X_LANES), lambda i: (i % tpb, 0))] * 2
        args += list(rope_tables)
    q, k, v = pl.pallas_call(
        functools.partial(_qkv_kernel, rope=rope),
        out_shape=(jax.ShapeDtypeStruct((n, D_MODEL), BF16),
                   jax.ShapeDtypeStruct((n, D_MODEL), kv_dtype),
                   jax.ShapeDtypeStruct((n, D_MODEL), kv_dtype)),
        grid=(n // TM,),
        in_specs=in_specs,
        out_specs=(pl.BlockSpec((TM, D_MODEL), lambda i: (i, 0)),) * 3,
        compiler_params=_params(("arbitrary",)),
        name="qkv",
    )(*args)
    shp = (B, T, D_MODEL)
    return q.reshape(shp), k.reshape(shp), v.reshape(shp)


def _rope_tables(n_tok):
    t = np.arange(n_tok)
    pos = jnp.asarray(np.stack([t // GRID_W, t % GRID_W], -1), F32)
    quarter = HD_A // 4
    inv_freq = ROPE_BASE ** (-jnp.arange(quarter, dtype=F32) / quarter)
    ang = pos[:, :, None] * inv_freq
    cos, sin = jnp.cos(ang), jnp.sin(ang)
    cos_h = jnp.broadcast_to(cos[:, :, None, :], (n_tok, 2, 2, quarter)).reshape(n_tok, HD_A)
    sin_h = jnp.stack([-sin, sin], axis=2).reshape(n_tok, HD_A)
    reps = V7X_LANES // HD_A
    return jnp.tile(cos_h, (1, reps)), jnp.tile(sin_h, (1, reps))


def _pair_scores(qc, key_chunks, biases=None):
    lo = lax.broadcasted_iota(jnp.int32, qc.shape, 1) < V7X_LANES // 2
    zero = jnp.zeros_like(qc)
    qq = jnp.concatenate([jnp.where(lo, qc, zero), jnp.where(lo, zero, qc)], axis=0)
    scores = [_dot_nt(qq, kc) for kc in key_chunks]
    if biases is not None:
        scores = [s if b is None else s + b for s, b in zip(scores, biases)]
    return scores


def _pair_softmax_pv(scores, value_chunks):
    m = functools.reduce(jnp.maximum, [jnp.max(s, -1, keepdims=True) for s in scores])
    acc = None
    for s, vc in zip(scores, value_chunks):
        p = jnp.exp(s - m).astype(BF16)
        v1 = jnp.concatenate([vc, jnp.ones_like(vc)], axis=1)
        part = jnp.dot(p, v1, preferred_element_type=F32)
        acc = part if acc is None else acc + part
    return acc[:, :V7X_LANES] / acc[:, V7X_LANES:]


PAIR_LOOKAHEAD = 1


def _pipelined_pairs(n_chunks, scores_fn, finish_fn):
    ready = [scores_fn(c) for c in range(min(PAIR_LOOKAHEAD, n_chunks))]
    for c in range(n_chunks):
        if c + PAIR_LOOKAHEAD < n_chunks:
            ready.append(scores_fn(c + PAIR_LOOKAHEAD))
        finish_fn(c, ready.pop(0))


def _diff_attn_kernel(*refs, nseg, lam_init):
    lam_ref, subln_ref, q_ref = refs[:3]
    seg_refs = refs[3:3 + 2 * nseg]
    o_ref = refs[3 + 2 * nseg]
    lp = lam_ref[...]
    lam = (jnp.exp(jnp.sum(lp[0:1] * lp[1:2], keepdims=True))
           - jnp.exp(jnp.sum(lp[2:3] * lp[3:4], keepdims=True)) + lam_init)
    scale = HD_A ** -0.5
    def lanes(h):
        return slice(h * V7X_LANES, (h + 1) * V7X_LANES)

    def scores(h):
        return _pair_scores(q_ref[:, lanes(h)] * scale,
                            [seg_refs[2 * j][:, lanes(h)].astype(BF16) for j in range(nseg)])

    def finish(h, s):
        o2 = _pair_softmax_pv(s, [seg_refs[2 * j + 1][:, lanes(h)].astype(BF16)
                                  for j in range(nseg)])
        acc = o2[:TQ] - lam * o2[TQ:]
        ms = jnp.mean(acc * acc, -1, keepdims=True)
        o = acc * lax.rsqrt(ms + LN_EPS) * subln_ref[...] * (1.0 - lam_init)
        o_ref[:, lanes(h)] = o.astype(BF16)

    _pipelined_pairs(H_A, scores, finish)


def _diff_attention(q, segs, lam_p, subln, lam_init):
    B, T, _ = q.shape
    in_specs = [
        pl.BlockSpec((4, HD_A), lambda b, i: (0, 0)),
        pl.BlockSpec((1, 2 * HD_A), lambda b, i: (0, 0)),
        pl.BlockSpec((None, TQ, D_MODEL), lambda b, i: (b, i, 0)),
    ]
    args = [lam_p, subln.reshape(1, 2 * HD_A), q]
    for k, v in segs:
        tk = k.shape[1]
        in_specs += [pl.BlockSpec((None, tk, D_MODEL), lambda b, i: (b, 0, 0))] * 2
        args += [k, v]
    return pl.pallas_call(
        functools.partial(_diff_attn_kernel, nseg=len(segs), lam_init=lam_init),
        out_shape=jax.ShapeDtypeStruct((B, T, D_MODEL), BF16),
        grid=(B, T // TQ),
        in_specs=in_specs,
        out_specs=pl.BlockSpec((None, TQ, D_MODEL), lambda b, i: (b, i, 0)),
        compiler_params=_params(("arbitrary", "arbitrary")),
        name="diff_attn",
    )(*args)


def _merge_pair(o2):
    tq = o2.shape[0] // 2
    lo = lax.broadcasted_iota(jnp.int32, (tq, V7X_LANES), 1) < V7X_LANES // 2
    return jnp.where(lo, o2[:tq], o2[tq:])


def _dense_attn_kernel(q_ref, k_ref, v_ref, o_ref):
    scale = HD_B ** -0.5
    for c in range(N_PAIRS):
        sl = slice(c * V7X_LANES, (c + 1) * V7X_LANES)
        s = _pair_scores(q_ref[:, sl] * scale, [k_ref[:, sl].astype(BF16)])
        o2 = _pair_softmax_pv(s, [v_ref[:, sl].astype(BF16)])
        o_ref[:, sl] = _merge_pair(o2).astype(BF16)


def _dense_attention(q, k, v):
    B, T, _ = q.shape
    spec = pl.BlockSpec((None, T, D_MODEL), lambda b: (b, 0, 0))
    return pl.pallas_call(
        _dense_attn_kernel,
        out_shape=jax.ShapeDtypeStruct((B, T, D_MODEL), BF16),
        grid=(B,),
        in_specs=[spec, spec, spec],
        out_specs=spec,
        compiler_params=_params(("arbitrary",)),
        name="dense_attn",
    )(q, k, v)


NA_WIN_BLOCKS = NA_WIN // TQ


def _na_attn_kernel(*refs):
    q_ref = refs[0]
    k_refs = refs[1:1 + NA_WIN_BLOCKS]
    v_refs = refs[1 + NA_WIN_BLOCKS:1 + 2 * NA_WIN_BLOCKS]
    kc_ref, vc_ref, bias_ref, o_ref = refs[1 + 2 * NA_WIN_BLOCKS:]
    scale = HD_B ** -0.5
    def lanes(c):
        return slice(c * V7X_LANES, (c + 1) * V7X_LANES)

    def scores(c):
        kwin = jnp.concatenate([r[:, lanes(c)] for r in k_refs], axis=0)
        bias = bias_ref[2 * c:2 * c + 2].reshape(2 * TQ, NA_WIN)
        return _pair_scores(q_ref[:, lanes(c)] * scale, [kwin, kc_ref[:, lanes(c)].astype(BF16)],
                            biases=[bias, None])

    def finish(c, s):
        vwin = jnp.concatenate([r[:, lanes(c)] for r in v_refs], axis=0)
        o2 = _pair_softmax_pv(s, [vwin, vc_ref[:, lanes(c)].astype(BF16)])
        o_ref[:, lanes(c)] = _merge_pair(o2).astype(BF16)

    _pipelined_pairs(N_PAIRS, scores, finish)


def _na_bias_tables(rpb, rows):
    kh = min(NA_KH, rows)
    n_heads = rpb.shape[0]
    qcol = np.arange(GRID_W)[:, None]
    kcol = np.arange(GRID_W)[None, :]
    c_start = np.clip(qcol - NA_KW // 2, 0, GRID_W - NA_KW)
    col_ok = (kcol >= c_start) & (kcol < c_start + NA_KW)
    col_off = np.clip(kcol - qcol + NA_KW - 1, 0, 2 * NA_KW - 2)
    onehot = (col_off[None] == np.arange(2 * NA_KW - 1)[:, None, None]).astype(np.float32)
    cols = jnp.einsum('hrc,cqk->hrqk', rpb, jnp.asarray(onehot), precision=lax.Precision.HIGHEST)
    cols = jnp.where(col_ok[None, None], cols, MASK_VALUE)
    padded = jnp.pad(cols, ((0, 0), (1, 1), (0, 0), (0, 0)), constant_values=MASK_VALUE)
    pairs = jnp.concatenate([padded[:, :-1], padded[:, 1:]], axis=-1)
    plan = []
    for r0 in (0, NA_QROWS, rows - NA_QROWS):
        ws = int(np.clip(r0 - NA_KH // 2, 0, rows - NA_WIN_ROWS))
        for qrow in range(r0, r0 + NA_QROWS):
            r_start = int(np.clip(qrow - kh // 2, 0, rows - kh))
            chunks = []
            for c in range(NA_WIN_ROWS // 2):
                krow = ws + 2 * c
                in_band = [r_start <= krow + s < r_start + kh for s in (0, 1)]
                chunks.append((krow - qrow + NA_KH - 1, in_band[0], in_band[1]))
            plan.append(chunks)
    return pl.pallas_call(
        functools.partial(_na_bias_kernel, plan=plan),
        out_shape=jax.ShapeDtypeStruct((3, n_heads, TQ, NA_WIN), F32),
        grid=(n_heads,),
        in_specs=[pl.BlockSpec((None, 2 * NA_KH, GRID_W, 2 * GRID_W), lambda h: (h, 0, 0, 0))],
        out_specs=pl.BlockSpec((3, None, TQ, NA_WIN), lambda h: (0, h, 0, 0)),
        compiler_params=_params(("arbitrary",)),
        name="na_bias",
    )(pairs)


def _na_bias_kernel(pairs_ref, o_ref, *, plan):
    left = lax.broadcasted_iota(jnp.int32, (GRID_W, 2 * GRID_W), 1) < GRID_W
    masked = jnp.full((GRID_W, 2 * GRID_W), MASK_VALUE, F32)
    for t, chunks in enumerate(plan):
        table, qrow = divmod(t, NA_QROWS)
        for c, (row_off, left_ok, right_ok) in enumerate(chunks):
            if left_ok or right_ok:
                piece = pairs_ref[row_off + 1]
                if not left_ok:
                    piece = jnp.where(left, masked, piece)
                if not right_ok:
                    piece = jnp.where(left, piece, masked)
            else:
                piece = masked
            o_ref[table, qrow * GRID_W:(qrow + 1) * GRID_W,
                  c * 2 * GRID_W:(c + 1) * 2 * GRID_W] = piece


def _neighborhood_attention(q, k, v, k_ctx, v_ctx, rpb):
    B, S, _ = q.shape
    P = k_ctx.shape[1]
    rows = S // GRID_W
    nqb = S // TQ
    assert rows >= NA_WIN_ROWS and nqb >= NA_WIN_BLOCKS
    bias = _na_bias_tables(rpb, rows)

    def bias_cfg(qb):
        return jnp.where(qb == 0, 0, jnp.where(qb == nqb - 1, 2, 1))

    def win_spec(j):
        return pl.BlockSpec((None, TQ, D_MODEL),
                            lambda qb, b: (b, jnp.clip(qb - 1, 0, nqb - NA_WIN_BLOCKS) + j, 0))

    win_specs = [win_spec(j) for j in range(NA_WIN_BLOCKS)]
    ctx_spec = pl.BlockSpec((None, P, D_MODEL), lambda qb, b: (b, 0, 0))
    q_spec = pl.BlockSpec((None, TQ, D_MODEL), lambda qb, b: (b, qb, 0))
    bias_spec = pl.BlockSpec((None, H_B, TQ, NA_WIN), lambda qb, b: (bias_cfg(qb), 0, 0, 0),
                             pipeline_mode=pl.Buffered(1))
    return pl.pallas_call(
        _na_attn_kernel,
        out_shape=jax.ShapeDtypeStruct((B, S, D_MODEL), BF16),
        grid=(nqb, B),
        in_specs=[q_spec] + win_specs + win_specs + [ctx_spec, ctx_spec, bias_spec],
        out_specs=q_spec,
        compiler_params=_params(("arbitrary", "arbitrary")),
        name="na_attn",
    )(q, *([k] * NA_WIN_BLOCKS), *([v] * NA_WIN_BLOCKS), k_ctx, v_ctx, bias)


INFO_E, INFO_W, INFO_RANK = 0, 2, 4


ROUTE_ROWS = 64


def _route(lt):
    sub = lax.broadcasted_iota(jnp.int32, lt.shape, 0).astype(F32)
    neg = -jnp.inf
    gmask = (sub >= N_EXPERTS) & (sub < N_EXPERTS + N_GROUPS)
    gl = jnp.where(gmask, lt, neg)
    gmax = jnp.max(gl, 0, keepdims=True)
    g_sel = jnp.min(jnp.where(gl == gmax, sub - N_EXPERTS, float(N_GROUPS)), 0, keepdims=True)
    p_g = 1.0 / jnp.sum(jnp.where(gmask, jnp.exp(gl - gmax), 0.0), 0, keepdims=True)
    emask = (sub >= g_sel * EXP_PER_GROUP) & (sub < (g_sel + 1.0) * EXP_PER_GROUP)
    el = jnp.where(emask, lt, neg)
    v1 = jnp.max(el, 0, keepdims=True)
    e1 = jnp.min(jnp.where(el == v1, sub, float(ROUTE_ROWS)), 0, keepdims=True)
    el2 = jnp.where(sub == e1, neg, el)
    v2 = jnp.max(el2, 0, keepdims=True)
    e2 = jnp.min(jnp.where(el2 == v2, sub, float(ROUTE_ROWS)), 0, keepdims=True)
    e21 = jnp.exp(v2 - v1)
    w1 = p_g / (1.0 + e21)
    w2 = p_g * e21 / (1.0 + e21)
    return sub, e1, e2, w1, w2


def _proj_route_kernel(o_ref, w_ref, x_ref, mod_ref, g_ref, b_ref, wr_ref, br_ref,
                       x1_ref, info_ref, cnt_ref, run):
    @pl.when(pl.program_id(0) == 0)
    def _():
        run[...] = jnp.zeros_like(run)

    out = jnp.dot(o_ref[...], w_ref[...], preferred_element_type=F32)
    x1 = _layer_norm(DEEPNORM_ALPHA * x_ref[...] + mod_ref[2:3, :] * out, g_ref[...], b_ref[...])
    x1_ref[...] = x1
    h2 = x1 * (1.0 + mod_ref[4:5, :]) + mod_ref[3:4, :]
    logits = jnp.dot(h2.astype(BF16), wr_ref[...], preferred_element_type=F32) + br_ref[...]
    sub, e1, e2, w1, w2 = _route(logits.T[:ROUTE_ROWS])

    oh1 = sub == e1
    oh2 = sub == e2
    oh = jnp.where(oh1 | oh2, 1.0, 0.0)
    t_row = lax.broadcasted_iota(jnp.int32, (TM, TM), 0)
    t_col = lax.broadcasted_iota(jnp.int32, (TM, TM), 1)
    earlier = jnp.where(t_row < t_col, 1.0, 0.0).astype(BF16)
    base = run[:, 0:1] + jnp.dot(oh.astype(BF16), earlier, preferred_element_type=F32)
    rank1 = jnp.sum(jnp.where(oh1, base, 0.0), 0, keepdims=True)
    rank2 = jnp.sum(jnp.where(oh2, base, 0.0), 0, keepdims=True)
    run[...] = run[...] + jnp.sum(oh, 1, keepdims=True)
    cnt_ref[...] = run[...]

    rec_row = lax.broadcasted_iota(jnp.int32, (ROUTE_LANES, TM), 0)
    rec = jnp.zeros((ROUTE_LANES, TM), F32)
    for row, val in ((INFO_E, e1), (INFO_E + 1, e2), (INFO_W, w1), (INFO_W + 1, w2),
                     (INFO_RANK, rank1), (INFO_RANK + 1, rank2)):
        rec = jnp.where(rec_row == row, val, rec)
    info_ref[...] = rec.T


def _proj_route(o, w_o_bf16, x, mods, mod_row, ln_g, ln_b, w_router, b_router):
    B, T, _ = x.shape
    n = B * T
    tpb = T // TM
    tile = pl.BlockSpec((TM, D_MODEL), lambda i: (i, 0))
    vec = pl.BlockSpec((1, D_MODEL), lambda i: (0, 0))
    return pl.pallas_call(
        _proj_route_kernel,
        out_shape=(jax.ShapeDtypeStruct((n, D_MODEL), F32),
                   jax.ShapeDtypeStruct((n, ROUTE_LANES), F32),
                   jax.ShapeDtypeStruct((ROUTE_ROWS, ROUTE_LANES), F32)),
        grid=(n // TM,),
        in_specs=[tile,
                  pl.BlockSpec((D_MODEL, D_MODEL), lambda i: (0, 0)),
                  tile,
                  pl.BlockSpec((None, 6, D_MODEL), lambda i: (mod_row(i // tpb), 0, 0)),
                  vec, vec,
                  pl.BlockSpec((D_MODEL, ROUTE_LANES), lambda i: (0, 0)),
                  pl.BlockSpec((1, ROUTE_LANES), lambda i: (0, 0))],
        out_specs=(tile,
                   pl.BlockSpec((TM, ROUTE_LANES), lambda i: (i, 0)),
                   pl.BlockSpec((ROUTE_ROWS, ROUTE_LANES), lambda i: (0, 0))),
        scratch_shapes=[pltpu.VMEM((ROUTE_ROWS, ROUTE_LANES), F32)],
        compiler_params=_params(("arbitrary",)),
        name="proj_route",
    )(o.reshape(n, D_MODEL), w_o_bf16, x.reshape(n, D_MODEL), mods,
      ln_g.reshape(1, D_MODEL), ln_b.reshape(1, D_MODEL), w_router, b_router)


def _route_tables(infos, counts, layer):
    cnts = [c[:N_EXPERTS, 0].astype(jnp.int32) for c in counts]
    cnt = functools.reduce(lambda a, b: a + b, cnts)
    padded = (cnt + MOE_BM - 1) // MOE_BM * MOE_BM
    pend = jnp.cumsum(padded)
    n_pairs = 2 * sum(info.shape[0] for info in infos)
    n_blocks = n_pairs // MOE_BM + N_EXPERTS
    n_used = pend[-1:] // MOE_BM
    experts = jnp.arange(N_EXPERTS, dtype=jnp.int32)
    pos = []
    first = pend - padded
    for info, c in zip(infos, cnts):
        e = info[:, INFO_E:INFO_E + 2].astype(jnp.int32)
        rank = info[:, INFO_RANK:INFO_RANK + 2].astype(jnp.int32)
        p = jnp.sum(jnp.where(e[..., None] == experts, first, 0), -1) + rank
        pos.append(p.reshape(-1).astype(jnp.int32))
        first = first + c
    blk = jnp.minimum(jnp.arange(n_blocks, dtype=jnp.int32), n_used - 1) * MOE_BM
    blk_e = jnp.sum((blk[:, None] >= pend[None, :]).astype(jnp.int32), axis=1) + layer * N_EXPERTS
    meta = jnp.concatenate([first, padded - cnt, n_used]).astype(jnp.int32)
    return pos, blk_e.astype(jnp.int32), meta, n_used, n_blocks


ROW_SLAB = D_MODEL // V7X_LANES


def _slab(row, n_rows=1):
    return pl.ds(pl.multiple_of(row * ROW_SLAB, ROW_SLAB), n_rows * ROW_SLAB)


def _store_rows(ref, x):
    rows = x.shape[0]
    for c in range(ROW_SLAB):
        ref[pl.ds(c, rows, stride=ROW_SLAB), :] = x[:, c * V7X_LANES:(c + 1) * V7X_LANES]


def _load_rows(ref, rows):
    return jnp.concatenate([ref[pl.ds(c, rows, stride=ROW_SLAB), :] for c in range(ROW_SLAB)], axis=1)


def _dispatch_kernel(pos_ref, meta_ref, x1a_ref, x1b_ref, mod_ref, xs_hbm, hbuf, zbuf, sem, sem_z, *,
                     n_blocks, tiles_a):
    i = pl.program_id(0)
    n_tiles = pl.num_programs(0)
    slot = i % 2

    @pl.when(i == 0)
    def _():
        zbuf[...] = jnp.zeros_like(zbuf)

    def wait_slot(s):
        for _ in range(2):
            pltpu.make_async_copy(hbuf.at[s], xs_hbm.at[pl.ds(0, TM * ROW_SLAB)], sem.at[s]).wait()

    @pl.when(i >= 2)
    def _():
        wait_slot(slot)

    @pl.when(i < tiles_a)
    def _():
        _store_rows(hbuf.at[slot], x1a_ref[...] * (1.0 + mod_ref[4:5, :]) + mod_ref[3:4, :])

    @pl.when(i >= tiles_a)
    def _():
        _store_rows(hbuf.at[slot], x1b_ref[...] * (1.0 + mod_ref[4:5, :]) + mod_ref[3:4, :])

    def scatter_row(r, carry):
        base = (i * TM + r) * 2
        src = hbuf.at[slot, _slab(r)]
        for k in range(2):
            pltpu.make_async_copy(src, xs_hbm.at[_slab(pos_ref[base + k])],
                                  sem.at[slot]).start(priority=k)
        return carry
    lax.fori_loop(0, TM, scatter_row, 0, unroll=8)

    @pl.when(i == n_tiles - 1)
    def _():
        def zero_copy(row, size):
            return pltpu.make_async_copy(zbuf.at[pl.ds(0, size * ROW_SLAB)],
                                         xs_hbm.at[_slab(row, size)], sem_z)

        def pad_expert(e, carry):
            start = meta_ref[e]
            length = meta_ref[N_EXPERTS + e]
            size = MOE_BM // 2
            while size >= 1:
                @pl.when((length & size) != 0)
                def _(size=size):
                    zero_copy(start + (length & ~(2 * size - 1)), size).start()
                size //= 2
            return carry
        lax.fori_loop(0, N_EXPERTS, pad_expert, 0)

        def tail_block(blk, carry):
            zero_copy(blk * MOE_BM, MOE_BM).start()
            return carry
        lax.fori_loop(meta_ref[2 * N_EXPERTS], n_blocks, tail_block, 0)

        wait_slot(slot)

        @pl.when(i >= 1)
        def _():
            wait_slot(1 - slot)

        for _ in range(N_EXPERTS):
            zero_copy(0, MOE_BM).wait()


def _dispatch(x1a, x1b, pos, meta, mods, mod_row_a, tpb_a, mod_row_b, tpb_b, n_blocks):
    tiles_a = x1a.shape[0] // TM
    tiles_b = x1b.shape[0] // TM
    assert n_blocks * MOE_BM - 2 * (tiles_a + tiles_b) * TM == N_EXPERTS * MOE_BM

    def mod_index(i, p, m):
        row = jnp.where(i < tiles_a, mod_row_a(i // tpb_a),
                        mod_row_b(jnp.maximum(i - tiles_a, 0) // tpb_b))
        return (row, 0, 0)

    grid_spec = pltpu.PrefetchScalarGridSpec(
        num_scalar_prefetch=2,
        grid=(tiles_a + tiles_b,),
        in_specs=[
            pl.BlockSpec((TM, D_MODEL), lambda i, p, m: (jnp.minimum(i, tiles_a - 1), 0)),
            pl.BlockSpec((TM, D_MODEL), lambda i, p, m: (jnp.maximum(i - tiles_a, 0), 0)),
            pl.BlockSpec((None, 6, D_MODEL), mod_index),
        ],
        out_specs=pl.BlockSpec(memory_space=pl.ANY),
        scratch_shapes=[
            pltpu.VMEM((2, TM * ROW_SLAB, V7X_LANES), F32),
            pltpu.VMEM((MOE_BM * ROW_SLAB, V7X_LANES), F32),
            pltpu.SemaphoreType.DMA((2,)),
            pltpu.SemaphoreType.DMA(()),
        ],
    )
    return pl.pallas_call(
        functools.partial(_dispatch_kernel, n_blocks=n_blocks, tiles_a=tiles_a),
        out_shape=jax.ShapeDtypeStruct((n_blocks * MOE_BM * ROW_SLAB, V7X_LANES), F32),
        grid_spec=grid_spec,
        compiler_params=_params(("arbitrary",)),
        name="dispatch",
    )(pos, meta, x1a, x1b, mods)


def _expert_kernel(be_ref, nused_ref, x_ref, wg_ref, wu_ref, wd_ref, o_ref, wg_s, wu_s, wd_s):
    i = pl.program_id(0)
    active = i < nused_ref[0]
    changed = (i == 0) | (be_ref[i] != be_ref[jnp.maximum(i - 1, 0)])

    @pl.when(active & changed)
    def _():
        wg_s[...] = wg_ref[...].astype(BF16)
        wu_s[...] = wu_ref[...].astype(BF16)
        wd_s[...] = wd_ref[...].astype(BF16)

    @pl.when(active)
    def _():
        xb = _load_rows(x_ref, MOE_BM).astype(BF16)
        g = jnp.dot(xb, wg_s[...], preferred_element_type=F32)
        u = jnp.dot(xb, wu_s[...], preferred_element_type=F32)
        hb = (g / (1.0 + jnp.exp(-g)) * u).astype(BF16)
        _store_rows(o_ref, jnp.dot(hb, wd_s[...], preferred_element_type=F32))

    @pl.when(jnp.logical_not(active))
    def _():
        o_ref[...] = jnp.zeros_like(o_ref)


def _experts(xs, blk_e, n_used, n_blocks, w_gate, w_up, w_down):
    w_gate = w_gate.reshape(DEPTH * N_EXPERTS, D_MODEL, D_EXPERT)
    w_up = w_up.reshape(DEPTH * N_EXPERTS, D_MODEL, D_EXPERT)
    w_down = w_down.reshape(DEPTH * N_EXPERTS, D_EXPERT, D_MODEL)
    rows_spec = pl.BlockSpec((MOE_BM * ROW_SLAB, V7X_LANES), lambda i, be, nu: (i, 0))
    grid_spec = pltpu.PrefetchScalarGridSpec(
        num_scalar_prefetch=2,
        grid=(n_blocks,),
        in_specs=[
            rows_spec,
            pl.BlockSpec((None, D_MODEL, D_EXPERT), lambda i, be, nu: (be[i], 0, 0)),
            pl.BlockSpec((None, D_MODEL, D_EXPERT), lambda i, be, nu: (be[i], 0, 0)),
            pl.BlockSpec((None, D_EXPERT, D_MODEL), lambda i, be, nu: (be[i], 0, 0)),
        ],
        out_specs=rows_spec,
        scratch_shapes=[
            pltpu.VMEM((D_MODEL, D_EXPERT), BF16),
            pltpu.VMEM((D_MODEL, D_EXPERT), BF16),
            pltpu.VMEM((D_EXPERT, D_MODEL), BF16),
        ],
    )
    return pl.pallas_call(
        _expert_kernel,
        out_shape=jax.ShapeDtypeStruct(xs.shape, F32),
        grid_spec=grid_spec,
        compiler_params=_params(("arbitrary",)),
        name="experts",
    )(blk_e, n_used, xs, w_gate, w_up, w_down)


def _combine_kernel(pos_ref, rows_hbm, x1_ref, route_ref, mod_ref, g_ref, b_ref, y_ref, buf, sem):
    i = pl.program_id(0)
    n_tiles = pl.num_programs(0)

    def issue_row(tile, slot, r):
        base = (tile * TM + r) * 2
        for k in range(2):
            pltpu.make_async_copy(rows_hbm.at[_slab(pos_ref[base + k])],
                                  buf.at[slot, k, _slab(r)], sem.at[slot]).start(priority=k)

    def wait_slot(s):
        for k in range(2):
            pltpu.make_async_copy(rows_hbm.at[pl.ds(0, TM * ROW_SLAB)], buf.at[s, k],
                                  sem.at[s]).wait()

    def issue(tile, slot):
        def body(r, carry):
            issue_row(tile, slot, r)
            return carry
        lax.fori_loop(0, TM, body, 0, unroll=8)

    @pl.when(i == 0)
    def _():
        issue(0, 0)

    @pl.when(i + 1 < n_tiles)
    def _():
        issue(i + 1, (i + 1) % 2)

    slot = i % 2
    wait_slot(slot)
    route = route_ref[...]
    moe = (route[:, INFO_W:INFO_W + 1] * _load_rows(buf.at[slot, 0], TM)
           + route[:, INFO_W + 1:INFO_W + 2] * _load_rows(buf.at[slot, 1], TM))
    y_ref[...] = _layer_norm(DEEPNORM_ALPHA * x1_ref[...] + mod_ref[5:6, :] * moe,
                             g_ref[...], b_ref[...])


def _combine(rows, pos, x1, route, mods, mod_row, tpb, ln_g, ln_b):
    n = x1.shape[0]
    tile = lambda i, p: (i, 0)
    grid_spec = pltpu.PrefetchScalarGridSpec(
        num_scalar_prefetch=1,
        grid=(n // TM,),
        in_specs=[
            pl.BlockSpec(memory_space=pl.ANY),
            pl.BlockSpec((TM, D_MODEL), tile),
            pl.BlockSpec((TM, ROUTE_LANES), tile),
            pl.BlockSpec((None, 6, D_MODEL), lambda i, p: (mod_row(i // tpb), 0, 0)),
            pl.BlockSpec((1, D_MODEL), lambda i, p: (0, 0)),
            pl.BlockSpec((1, D_MODEL), lambda i, p: (0, 0)),
        ],
        out_specs=pl.BlockSpec((TM, D_MODEL), tile),
        scratch_shapes=[pltpu.VMEM((2, 2, TM * ROW_SLAB, V7X_LANES), F32),
                        pltpu.SemaphoreType.DMA((2,))],
    )
    return pl.pallas_call(
        _combine_kernel,
        out_shape=jax.ShapeDtypeStruct((n, D_MODEL), F32),
        grid_spec=grid_spec,
        compiler_params=_params(("arbitrary",)),
        name="combine",
    )(pos, rows, x1, route, mods, ln_g.reshape(1, D_MODEL), ln_b.reshape(1, D_MODEL))


def _diff_lambda_init(layer_idx):
    return 0.8 - 0.6 * math.exp(-0.3 * layer_idx)


def _router_weights(w_rg, b_rg, w_re, b_re):
    pad = ROUTE_LANES - N_EXPERTS - N_GROUPS
    w = jnp.concatenate([w_re, w_rg, jnp.zeros((D_MODEL, pad), F32)], axis=1)
    b = jnp.concatenate([b_re, b_rg, jnp.zeros((pad,), F32)]).reshape(1, ROUTE_LANES)
    return w.astype(BF16), b


def kernel(x_prompt, x_sample, cache_a_k, cache_a_v, cache_b_k, cache_b_v, c, c_ctx, w_ada, b_ada, ln_g, ln_b, w_qkv_a, lam_a, subln_a, w_o_a, w_qkv_b, rpb_b, w_o_b, w_rg, b_rg, w_re, b_re, w_gate, w_up, w_down):
    n_mod_rows = 16
    dec_batch = x_sample.shape[0]
    cond = jnp.zeros((n_mod_rows, D_MODEL), F32).at[0].set(c_ctx).at[1:1 + dec_batch].set(c)
    mods_all = _adaln_all(cond, w_ada, b_ada).reshape(DEPTH, n_mod_rows, 6, D_MODEL)
    rope_tables = _rope_tables(x_sample.shape[1])

    xs_in = [x_prompt, x_sample]
    mod_rows = [lambda b: 0, lambda b: 1 + b]
    kv = []
    for i in range(DEPTH):
        mods = mods_all[i]
        j = i // 2
        diff_layer = i % 2 == 0
        if diff_layer:
            w_qkv, w_o = w_qkv_a[j].astype(BF16), w_o_a[j].astype(BF16)
            lam_init = _diff_lambda_init(i)
        else:
            w_qkv, w_o = w_qkv_b[j].astype(BF16), w_o_b[j].astype(BF16)
        w_router, b_router = _router_weights(w_rg[i], b_rg[i], w_re[i], b_re[i])
        routed = []
        for x, mod_row, latent in zip(xs_in, mod_rows, (False, True)):
            B = x.shape[0]
            if diff_layer and latent:
                q, k, v = _qkv(x, mods, w_qkv, mod_row, BF16, rope_tables)
                segs = [(cache_a_k[:, j].reshape(B, -1, D_MODEL),
                         cache_a_v[:, j].reshape(B, -1, D_MODEL)), (k, v)]
                o = _diff_attention(q, segs, lam_a[j], subln_a[j], lam_init)
            elif diff_layer:
                q, k, v = _qkv(x, mods, w_qkv, mod_row, F32)
                kv.append((k, v))
                o = _diff_attention(q, [(k, v)], lam_a[j], subln_a[j], lam_init)
            elif latent:
                q, k, v = _qkv(x, mods, w_qkv, mod_row, BF16)
                o = _neighborhood_attention(q, k, v, cache_b_k[:, j].reshape(B, -1, D_MODEL),
                                            cache_b_v[:, j].reshape(B, -1, D_MODEL), rpb_b[j])
            else:
                q, k, v = _qkv(x, mods, w_qkv, mod_row, F32)
                kv.append((k, v))
                o = _dense_attention(q, k, v)
            routed.append(_proj_route(o, w_o, x, mods, mod_row, ln_g[i, 0], ln_b[i, 0],
                                      w_router, b_router))
        x1s, infos, counts = zip(*routed)
        pos, blk_e, meta, n_used, n_blocks = _route_tables(infos, counts, i)
        tpbs = [x.shape[1] // TM for x in xs_in]
        rows_in = _dispatch(x1s[0], x1s[1], jnp.concatenate(pos), meta, mods,
                            mod_rows[0], tpbs[0], mod_rows[1], tpbs[1], n_blocks)
        rows = _experts(rows_in, blk_e, n_used, n_blocks, w_gate, w_up, w_down)
        xs_in = [_combine(rows, p, x1, info, mods, mod_row, tpb, ln_g[i, 1], ln_b[i, 1]).reshape(x.shape)
                 for p, x1, info, mod_row, tpb, x in zip(pos, x1s, infos, mod_rows, tpbs, xs_in)]
    y_prompt, y_sample = xs_in

    B, T, _ = x_prompt.shape
    new_a_k = kv[0][0].reshape(B, 1, T, 2 * H_A, HD_A)
    new_a_v = kv[0][1].reshape(B, 1, T, H_A, 2 * HD_A)
    new_b_k = kv[1][0].reshape(B, 1, T, H_B, HD_B)
    new_b_v = kv[1][1].reshape(B, 1, T, H_B, HD_B)
    return y_prompt, y_sample, new_a_k, new_a_v, new_b_k, new_b_v
```

```python
import functools
import math

import jax
import jax.numpy as jnp
import numpy as np
from jax import lax
from jax.experimental import pallas as pl
from jax.experimental.pallas import tpu as pltpu

D_MODEL = 1024
DEPTH = 2
GRID_W = 64
HD_A = 64
H_A = D_MODEL // (2 * HD_A)
H_B = 16
HD_B = D_MODEL // H_B
NA_KH = 8
NA_KW = 16
ROPE_BASE = 10000.0
N_GROUPS = 4
EXP_PER_GROUP = 8
N_EXPERTS = N_GROUPS * EXP_PER_GROUP
D_EXPERT = 512
LN_EPS = 1e-5
DEEPNORM_ALPHA = (2.0 * DEPTH) ** 0.25

F32 = jnp.float32
BF16 = jnp.bfloat16

V7X_LANES = 128
V7X_VMEM_BYTES = 64 * 1024 * 1024
VMEM_LIMIT = V7X_VMEM_BYTES * 7 // 8

TM = 256
TQ = 256
TQ_DIFF = 512
MOE_BM = 256
N_PAIRS = D_MODEL // V7X_LANES
NA_QROWS = TQ // GRID_W
NA_WIN_ROWS = NA_QROWS + NA_KH
NA_WIN = NA_WIN_ROWS * GRID_W
MASK_VALUE = -1e30
ROUTE_LANES = 128


def _params(sem):
    return pltpu.CompilerParams(dimension_semantics=sem, vmem_limit_bytes=VMEM_LIMIT)


def _layer_norm(y, g, b):
    mu = jnp.mean(y, -1, keepdims=True)
    d = y - mu
    var = jnp.mean(d * d, -1, keepdims=True)
    return d * lax.rsqrt(var + LN_EPS) * g + b


def _dot_nt(a, b):
    return lax.dot_general(a, b, (((1,), (1,)), ((), ())), preferred_element_type=F32)


def _ada_kernel(c_ref, w_ref, b_ref, o_ref):
    c = c_ref[...]
    a = c / (1.0 + jnp.exp(-c))
    o_ref[...] = jnp.dot(a, w_ref[...], preferred_element_type=F32,
                         precision=lax.Precision.HIGHEST) + b_ref[...]


def _adaln_all(cond, w_ada, b_ada):
    rows = cond.shape[0]
    tn = 1536
    n6 = 6 * D_MODEL
    return pl.pallas_call(
        _ada_kernel,
        out_shape=jax.ShapeDtypeStruct((DEPTH, rows, n6), F32),
        grid=(DEPTH, n6 // tn),
        in_specs=[
            pl.BlockSpec((rows, D_MODEL), lambda l, j: (0, 0)),
            pl.BlockSpec((None, D_MODEL, tn), lambda l, j: (l, 0, j)),
            pl.BlockSpec((None, 1, tn), lambda l, j: (l, 0, j)),
        ],
        out_specs=pl.BlockSpec((None, rows, tn), lambda l, j: (l, 0, j)),
        compiler_params=_params(("arbitrary", "arbitrary")),
        name="adaln",
    )(cond, w_ada, b_ada.reshape(DEPTH, 1, n6))


def _qkv_kernel(*refs, rope):
    if rope:
        x_ref, mod_ref, w_ref, cos_ref, sin_ref, q_ref, k_ref, v_ref = refs
    else:
        x_ref, mod_ref, w_ref, q_ref, k_ref, v_ref = refs
    h = (x_ref[...] * (1.0 + mod_ref[1:2, :]) + mod_ref[0:1, :]).astype(BF16)
    if rope:
        lane = lax.broadcasted_iota(jnp.int32, (TM, V7X_LANES), 1)
        first = (lane & (HD_A // 2 - 1)) < (HD_A // 4)
        cos = cos_ref[...]
        sin = sin_ref[...]
    for idx, o_ref in enumerate((q_ref, k_ref, v_ref)):
        y = jnp.dot(h, w_ref[:, idx * D_MODEL:(idx + 1) * D_MODEL], preferred_element_type=F32)
        if rope and idx < 2:
            for c in range(N_PAIRS):
                sl = slice(c * V7X_LANES, (c + 1) * V7X_LANES)
                yc = y[:, sl]
                partner = jnp.where(first, pltpu.roll(yc, V7X_LANES - HD_A // 4, 1),
                                    pltpu.roll(yc, HD_A // 4, 1))
                o_ref[:, sl] = (yc * cos + partner * sin).astype(o_ref.dtype)
        else:
            o_ref[...] = y.astype(o_ref.dtype)


def _qkv(x, mods, w_bf16, mod_row, kv_dtype, rope_tables=None):
    B, T, _ = x.shape
    n = B * T
    tpb = T // TM
    rope = rope_tables is not None
    in_specs = [
        pl.BlockSpec((TM, D_MODEL), lambda i: (i, 0)),
        pl.BlockSpec((None, 6, D_MODEL), lambda i: (mod_row(i // tpb), 0, 0)),
        pl.BlockSpec((D_MODEL, 3 * D_MODEL), lambda i: (0, 0)),
    ]
    args = [x.reshape(n, D_MODEL), mods, w_bf16]
    if rope:
        in_specs += [pl.BlockSpec((TM, V7X_LANES), lambda i: (i % tpb, 0))] * 2
        args += list(rope_tables)
    q, k, v = pl.pallas_call(
        functools.partial(_qkv_kernel, rope=rope),
        out_shape=(jax.ShapeDtypeStruct((n, D_MODEL), BF16),
                   jax.ShapeDtypeStruct((n, D_MODEL), kv_dtype),
                   jax.ShapeDtypeStruct((n, D_MODEL), kv_dtype)),
        grid=(n // TM,),
        in_specs=in_specs,
        out_specs=(pl.BlockSpec((TM, D_MODEL), lambda i: (i, 0)),) * 3,
        compiler_params=_params(("arbitrary",)),
        name="qkv",
    )(*args)
    shp = (B, T, D_MODEL)
    return q.reshape(shp), k.reshape(shp), v.reshape(shp)


def _rope_tables(n_tok):
    t = np.arange(n_tok)
    pos = jnp.asarray(np.stack([t // GRID_W, t % GRID_W], -1), F32)
    quarter = HD_A // 4
    inv_freq = ROPE_BASE ** (-jnp.arange(quarter, dtype=F32) / quarter)
    ang = pos[:, :, None] * inv_freq
    cos, sin = jnp.cos(ang), jnp.sin(ang)
    cos_h = jnp.broadcast_to(cos[:, :, None, :], (n_tok, 2, 2, quarter)).reshape(n_tok, HD_A)
    sin_h = jnp.stack([-sin, sin], axis=2).reshape(n_tok, HD_A)
    reps = V7X_LANES // HD_A
    return jnp.tile(cos_h, (1, reps)), jnp.tile(sin_h, (1, reps))


def _pair_scores(qc, key_chunks, biases=None):
    lo = lax.broadcasted_iota(jnp.int32, qc.shape, 1) < V7X_LANES // 2
    zero = jnp.zeros_like(qc)
    qq = jnp.concatenate([jnp.where(lo, qc, zero), jnp.where(lo, zero, qc)], axis=0)
    scores = [_dot_nt(qq, kc) for kc in key_chunks]
    if biases is not None:
        scores = [s if b is None else s + b for s, b in zip(scores, biases)]
    return scores


def _pair_softmax_pv(scores, value_chunks):
    m = functools.reduce(jnp.maximum, [jnp.max(s, -1, keepdims=True) for s in scores])
    acc = None
    for s, vc in zip(scores, value_chunks):
        p = jnp.exp(s - m).astype(BF16)
        v1 = jnp.concatenate([vc, jnp.ones_like(vc)], axis=1)
        part = jnp.dot(p, v1, preferred_element_type=F32)
        acc = part if acc is None else acc + part
    return acc[:, :V7X_LANES] / acc[:, V7X_LANES:]


PAIR_LOOKAHEAD = 1


def _pipelined_pairs(n_chunks, scores_fn, finish_fn):
    ready = [scores_fn(c) for c in range(min(PAIR_LOOKAHEAD, n_chunks))]
    for c in range(n_chunks):
        if c + PAIR_LOOKAHEAD < n_chunks:
            ready.append(scores_fn(c + PAIR_LOOKAHEAD))
        finish_fn(c, ready.pop(0))


def _diff_attn_kernel(*refs, nseg, lam_init):
    lam_ref, subln_ref, q_ref = refs[:3]
    seg_refs = refs[3:3 + 2 * nseg]
    o_ref = refs[3 + 2 * nseg]
    lp = lam_ref[...]
    lam = (jnp.exp(jnp.sum(lp[0:1] * lp[1:2], keepdims=True))
           - jnp.exp(jnp.sum(lp[2:3] * lp[3:4], keepdims=True)) + lam_init)
    scale = HD_A ** -0.5
    def lanes(h):
        return slice(h * V7X_LANES, (h + 1) * V7X_LANES)

    def scores(h):
        return _pair_scores(q_ref[:, lanes(h)] * scale,
                            [seg_refs[2 * j][:, lanes(h)].astype(BF16) for j in range(nseg)])

    def finish(h, s):
        o2 = _pair_softmax_pv(s, [seg_refs[2 * j + 1][:, lanes(h)].astype(BF16)
                                  for j in range(nseg)])
        tq = o2.shape[0] // 2
        acc = o2[:tq] - lam * o2[tq:]
        ms = jnp.mean(acc * acc, -1, keepdims=True)
        o = acc * lax.rsqrt(ms + LN_EPS) * subln_ref[...] * (1.0 - lam_init)
        o_ref[:, lanes(h)] = o.astype(BF16)

    _pipelined_pairs(H_A, scores, finish)


def _diff_attention(q, segs, lam_p, subln, lam_init):
    B, T, _ = q.shape
    tq = min(TQ_DIFF, T)
    in_specs = [
        pl.BlockSpec((4, HD_A), lambda b, i: (0, 0)),
        pl.BlockSpec((1, 2 * HD_A), lambda b, i: (0, 0)),
        pl.BlockSpec((None, tq, D_MODEL), lambda b, i: (b, i, 0)),
    ]
    args = [lam_p, subln.reshape(1, 2 * HD_A), q]
    for k, v in segs:
        tk = k.shape[1]
        in_specs += [pl.BlockSpec((None, tk, D_MODEL), lambda b, i: (b, 0, 0))] * 2
        args += [k, v]
    return pl.pallas_call(
        functools.partial(_diff_attn_kernel, nseg=len(segs), lam_init=lam_init),
        out_shape=jax.ShapeDtypeStruct((B, T, D_MODEL), BF16),
        grid=(B, T // tq),
        in_specs=in_specs,
        out_specs=pl.BlockSpec((None, tq, D_MODEL), lambda b, i: (b, i, 0)),
        compiler_params=_params(("arbitrary", "arbitrary")),
        name="diff_attn",
    )(*args)


def _merge_pair(o2):
    tq = o2.shape[0] // 2
    lo = lax.broadcasted_iota(jnp.int32, (tq, V7X_LANES), 1) < V7X_LANES // 2
    return jnp.where(lo, o2[:tq], o2[tq:])


def _dense_attn_kernel(q_ref, k_ref, v_ref, o_ref):
    scale = HD_B ** -0.5
    for c in range(N_PAIRS):
        sl = slice(c * V7X_LANES, (c + 1) * V7X_LANES)
        s = _pair_scores(q_ref[:, sl] * scale, [k_ref[:, sl].astype(BF16)])
        o2 = _pair_softmax_pv(s, [v_ref[:, sl].astype(BF16)])
        o_ref[:, sl] = _merge_pair(o2).astype(BF16)


def _dense_attention(q, k, v):
    B, T, _ = q.shape
    spec = pl.BlockSpec((None, T, D_MODEL), lambda b: (b, 0, 0))
    return pl.pallas_call(
        _dense_attn_kernel,
        out_shape=jax.ShapeDtypeStruct((B, T, D_MODEL), BF16),
        grid=(B,),
        in_specs=[spec, spec, spec],
        out_specs=spec,
        compiler_params=_params(("arbitrary",)),
        name="dense_attn",
    )(q, k, v)


NA_WIN_BLOCKS = NA_WIN // TQ


def _na_attn_kernel(*refs):
    q_ref = refs[0]
    k_refs = refs[1:1 + NA_WIN_BLOCKS]
    v_refs = refs[1 + NA_WIN_BLOCKS:1 + 2 * NA_WIN_BLOCKS]
    kc_ref, vc_ref, bias_ref, o_ref = refs[1 + 2 * NA_WIN_BLOCKS:]
    scale = HD_B ** -0.5
    def lanes(c):
        return slice(c * V7X_LANES, (c + 1) * V7X_LANES)

    def scores(c):
        kwin = jnp.concatenate([r[:, lanes(c)] for r in k_refs], axis=0)
        bias = bias_ref[2 * c:2 * c + 2].reshape(2 * TQ, NA_WIN)
        return _pair_scores(q_ref[:, lanes(c)] * scale, [kwin, kc_ref[:, lanes(c)].astype(BF16)],
                            biases=[bias, None])

    def finish(c, s):
        vwin = jnp.concatenate([r[:, lanes(c)] for r in v_refs], axis=0)
        o2 = _pair_softmax_pv(s, [vwin, vc_ref[:, lanes(c)].astype(BF16)])
        o_ref[:, lanes(c)] = _merge_pair(o2).astype(BF16)

    _pipelined_pairs(N_PAIRS, scores, finish)


def _na_bias_tables(rpb, rows):
    kh = min(NA_KH, rows)
    n_heads = rpb.shape[0]
    qcol = np.arange(GRID_W)[:, None]
    kcol = np.arange(GRID_W)[None, :]
    c_start = np.clip(qcol - NA_KW // 2, 0, GRID_W - NA_KW)
    col_ok = (kcol >= c_start) & (kcol < c_start + NA_KW)
    col_off = np.clip(kcol - qcol + NA_KW - 1, 0, 2 * NA_KW - 2)
    onehot = (col_off[None] == np.arange(2 * NA_KW - 1)[:, None, None]).astype(np.float32)
    cols = jnp.einsum('hrc,cqk->hrqk', rpb, jnp.asarray(onehot), precision=lax.Precision.HIGHEST)
    cols = jnp.where(col_ok[None, None], cols, MASK_VALUE)
    padded = jnp.pad(cols, ((0, 0), (1, 1), (0, 0), (0, 0)), constant_values=MASK_VALUE)
    pairs = jnp.concatenate([padded[:, :-1], padded[:, 1:]], axis=-1)
    plan = []
    for r0 in (0, NA_QROWS, rows - NA_QROWS):
        ws = int(np.clip(r0 - NA_KH // 2, 0, rows - NA_WIN_ROWS))
        for qrow in range(r0, r0 + NA_QROWS):
            r_start = int(np.clip(qrow - kh // 2, 0, rows - kh))
            chunks = []
            for c in range(NA_WIN_ROWS // 2):
                krow = ws + 2 * c
                in_band = [r_start <= krow + s < r_start + kh for s in (0, 1)]
                chunks.append((krow - qrow + NA_KH - 1, in_band[0], in_band[1]))
            plan.append(chunks)
    return pl.pallas_call(
        functools.partial(_na_bias_kernel, plan=plan),
        out_shape=jax.ShapeDtypeStruct((3, n_heads, TQ, NA_WIN), F32),
        grid=(n_heads,),
        in_specs=[pl.BlockSpec((None, 2 * NA_KH, GRID_W, 2 * GRID_W), lambda h: (h, 0, 0, 0))],
        out_specs=pl.BlockSpec((3, None, TQ, NA_WIN), lambda h: (0, h, 0, 0)),
        compiler_params=_params(("arbitrary",)),
        name="na_bias",
    )(pairs)


def _na_bias_kernel(pairs_ref, o_ref, *, plan):
    left = lax.broadcasted_iota(jnp.int32, (GRID_W, 2 * GRID_W), 1) < GRID_W
    masked = jnp.full((GRID_W, 2 * GRID_W), MASK_VALUE, F32)
    for t, chunks in enumerate(plan):
        table, qrow = divmod(t, NA_QROWS)
        for c, (row_off, left_ok, right_ok) in enumerate(chunks):
            if left_ok or right_ok:
                piece = pairs_ref[row_off + 1]
                if not left_ok:
                    piece = jnp.where(left, masked, piece)
                if not right_ok:
                    piece = jnp.where(left, piece, masked)
            else:
                piece = masked
            o_ref[table, qrow * GRID_W:(qrow + 1) * GRID_W,
                  c * 2 * GRID_W:(c + 1) * 2 * GRID_W] = piece


def _neighborhood_attention(q, k, v, k_ctx, v_ctx, rpb):
    B, S, _ = q.shape
    P = k_ctx.shape[1]
    rows = S // GRID_W
    nqb = S // TQ
    assert rows >= NA_WIN_ROWS and nqb >= NA_WIN_BLOCKS
    bias = _na_bias_tables(rpb, rows)

    def bias_cfg(qb):
        return jnp.where(qb == 0, 0, jnp.where(qb == nqb - 1, 2, 1))

    def win_spec(j):
        return pl.BlockSpec((None, TQ, D_MODEL),
                            lambda qb, b: (b, jnp.clip(qb - 1, 0, nqb - NA_WIN_BLOCKS) + j, 0))

    win_specs = [win_spec(j) for j in range(NA_WIN_BLOCKS)]
    ctx_spec = pl.BlockSpec((None, P, D_MODEL), lambda qb, b: (b, 0, 0))
    q_spec = pl.BlockSpec((None, TQ, D_MODEL), lambda qb, b: (b, qb, 0))
    bias_spec = pl.BlockSpec((None, H_B, TQ, NA_WIN), lambda qb, b: (bias_cfg(qb), 0, 0, 0),
                             pipeline_mode=pl.Buffered(1))
    return pl.pallas_call(
        _na_attn_kernel,
        out_shape=jax.ShapeDtypeStruct((B, S, D_MODEL), BF16),
        grid=(nqb, B),
        in_specs=[q_spec] + win_specs + win_specs + [ctx_spec, ctx_spec, bias_spec],
        out_specs=q_spec,
        compiler_params=_params(("arbitrary", "arbitrary")),
        name="na_attn",
    )(q, *([k] * NA_WIN_BLOCKS), *([v] * NA_WIN_BLOCKS), k_ctx, v_ctx, bias)


INFO_E, INFO_W, INFO_RANK = 0, 2, 4


ROUTE_ROWS = 64


def _route(lt):
    sub = lax.broadcasted_iota(jnp.int32, lt.shape, 0).astype(F32)
    neg = -jnp.inf
    gmask = (sub >= N_EXPERTS) & (sub < N_EXPERTS + N_GROUPS)
    gl = jnp.where(gmask, lt, neg)
    gmax = jnp.max(gl, 0, keepdims=True)
    g_sel = jnp.min(jnp.where(gl == gmax, sub - N_EXPERTS, float(N_GROUPS)), 0, keepdims=True)
    p_g = 1.0 / jnp.sum(jnp.where(gmask, jnp.exp(gl - gmax), 0.0), 0, keepdims=True)
    emask = (sub >= g_sel * EXP_PER_GROUP) & (sub < (g_sel + 1.0) * EXP_PER_GROUP)
    el = jnp.where(emask, lt, neg)
    v1 = jnp.max(el, 0, keepdims=True)
    e1 = jnp.min(jnp.where(el == v1, sub, float(ROUTE_ROWS)), 0, keepdims=True)
    el2 = jnp.where(sub == e1, neg, el)
    v2 = jnp.max(el2, 0, keepdims=True)
    e2 = jnp.min(jnp.where(el2 == v2, sub, float(ROUTE_ROWS)), 0, keepdims=True)
    e21 = jnp.exp(v2 - v1)
    w1 = p_g / (1.0 + e21)
    w2 = p_g * e21 / (1.0 + e21)
    return sub, e1, e2, w1, w2


def _proj_route_kernel(o_ref, w_ref, x_ref, mod_ref, g_ref, b_ref, wr_ref, br_ref,
                       x1_ref, info_ref, cnt_ref, run):
    @pl.when(pl.program_id(0) == 0)
    def _():
        run[...] = jnp.zeros_like(run)

    out = jnp.dot(o_ref[...], w_ref[...], preferred_element_type=F32)
    x1 = _layer_norm(DEEPNORM_ALPHA * x_ref[...] + mod_ref[2:3, :] * out, g_ref[...], b_ref[...])
    x1_ref[...] = x1
    h2 = x1 * (1.0 + mod_ref[4:5, :]) + mod_ref[3:4, :]
    logits = jnp.dot(h2.astype(BF16), wr_ref[...], preferred_element_type=F32) + br_ref[...]
    sub, e1, e2, w1, w2 = _route(logits.T[:ROUTE_ROWS])

    oh1 = sub == e1
    oh2 = sub == e2
    oh = jnp.where(oh1 | oh2, 1.0, 0.0)
    t_row = lax.broadcasted_iota(jnp.int32, (TM, TM), 0)
    t_col = lax.broadcasted_iota(jnp.int32, (TM, TM), 1)
    earlier = jnp.where(t_row < t_col, 1.0, 0.0).astype(BF16)
    base = run[:, 0:1] + jnp.dot(oh.astype(BF16), earlier, preferred_element_type=F32)
    rank1 = jnp.sum(jnp.where(oh1, base, 0.0), 0, keepdims=True)
    rank2 = jnp.sum(jnp.where(oh2, base, 0.0), 0, keepdims=True)
    run[...] = run[...] + jnp.sum(oh, 1, keepdims=True)
    cnt_ref[...] = run[...]

    rec_row = lax.broadcasted_iota(jnp.int32, (ROUTE_LANES, TM), 0)
    rec = jnp.zeros((ROUTE_LANES, TM), F32)
    for row, val in ((INFO_E, e1), (INFO_E + 1, e2), (INFO_W, w1), (INFO_W + 1, w2),
                     (INFO_RANK, rank1), (INFO_RANK + 1, rank2)):
        rec = jnp.where(rec_row == row, val, rec)
    info_ref[...] = rec.T


def _proj_route(o, w_o_bf16, x, mods, mod_row, ln_g, ln_b, w_router, b_router):
    B, T, _ = x.shape
    n = B * T
    tpb = T // TM
    tile = pl.BlockSpec((TM, D_MODEL), lambda i: (i, 0))
    vec = pl.BlockSpec((1, D_MODEL), lambda i: (0, 0))
    return pl.pallas_call(
        _proj_route_kernel,
        out_shape=(jax.ShapeDtypeStruct((n, D_MODEL), F32),
                   jax.ShapeDtypeStruct((n, ROUTE_LANES), F32),
                   jax.ShapeDtypeStruct((ROUTE_ROWS, ROUTE_LANES), F32)),
        grid=(n // TM,),
        in_specs=[tile,
                  pl.BlockSpec((D_MODEL, D_MODEL), lambda i: (0, 0)),
                  tile,
                  pl.BlockSpec((None, 6, D_MODEL), lambda i: (mod_row(i // tpb), 0, 0)),
                  vec, vec,
                  pl.BlockSpec((D_MODEL, ROUTE_LANES), lambda i: (0, 0)),
                  pl.BlockSpec((1, ROUTE_LANES), lambda i: (0, 0))],
        out_specs=(tile,
                   pl.BlockSpec((TM, ROUTE_LANES), lambda i: (i, 0)),
                   pl.BlockSpec((ROUTE_ROWS, ROUTE_LANES), lambda i: (0, 0))),
        scratch_shapes=[pltpu.VMEM((ROUTE_ROWS, ROUTE_LANES), F32)],
        compiler_params=_params(("arbitrary",)),
        name="proj_route",
    )(o.reshape(n, D_MODEL), w_o_bf16, x.reshape(n, D_MODEL), mods,
      ln_g.reshape(1, D_MODEL), ln_b.reshape(1, D_MODEL), w_router, b_router)


def _route_tables(infos, counts, layer):
    cnts = [c[:N_EXPERTS, 0].astype(jnp.int32) for c in counts]
    cnt = functools.reduce(lambda a, b: a + b, cnts)
    padded = (cnt + MOE_BM - 1) // MOE_BM * MOE_BM
    pend = jnp.cumsum(padded)
    n_pairs = 2 * sum(info.shape[0] for info in infos)
    n_blocks = n_pairs // MOE_BM + N_EXPERTS
    n_used = pend[-1:] // MOE_BM
    experts = jnp.arange(N_EXPERTS, dtype=jnp.int32)
    pos = []
    first = pend - padded
    for info, c in zip(infos, cnts):
        e = info[:, INFO_E:INFO_E + 2].astype(jnp.int32)
        rank = info[:, INFO_RANK:INFO_RANK + 2].astype(jnp.int32)
        p = jnp.sum(jnp.where(e[..., None] == experts, first, 0), -1) + rank
        pos.append(p.reshape(-1).astype(jnp.int32))
        first = first + c
    blk = jnp.minimum(jnp.arange(n_blocks, dtype=jnp.int32), n_used - 1) * MOE_BM
    blk_e = jnp.sum((blk[:, None] >= pend[None, :]).astype(jnp.int32), axis=1) + layer * N_EXPERTS
    meta = jnp.concatenate([first, padded - cnt, n_used]).astype(jnp.int32)
    return pos, blk_e.astype(jnp.int32), meta, n_used, n_blocks


ROW_SLAB = D_MODEL // V7X_LANES


def _slab(row, n_rows=1):
    return pl.ds(pl.multiple_of(row * ROW_SLAB, ROW_SLAB), n_rows * ROW_SLAB)


def _store_rows(ref, x):
    rows = x.shape[0]
    for c in range(ROW_SLAB):
        ref[pl.ds(c, rows, stride=ROW_SLAB), :] = x[:, c * V7X_LANES:(c + 1) * V7X_LANES]


def _load_rows(ref, rows):
    return jnp.concatenate([ref[pl.ds(c, rows, stride=ROW_SLAB), :] for c in range(ROW_SLAB)], axis=1)


def _dispatch_kernel(pos_ref, meta_ref, x1a_ref, x1b_ref, mod_ref, xs_hbm, hbuf, zbuf, sem, sem_z, *,
                     n_blocks, tiles_a):
    i = pl.program_id(0)
    n_tiles = pl.num_programs(0)
    slot = i % 2

    @pl.when(i == 0)
    def _():
        zbuf[...] = jnp.zeros_like(zbuf)

    def wait_slot(s):
        for _ in range(2):
            pltpu.make_async_copy(hbuf.at[s], xs_hbm.at[pl.ds(0, TM * ROW_SLAB)], sem.at[s]).wait()

    @pl.when(i >= 2)
    def _():
        wait_slot(slot)

    @pl.when(i < tiles_a)
    def _():
        _store_rows(hbuf.at[slot], x1a_ref[...] * (1.0 + mod_ref[4:5, :]) + mod_ref[3:4, :])

    @pl.when(i >= tiles_a)
    def _():
        _store_rows(hbuf.at[slot], x1b_ref[...] * (1.0 + mod_ref[4:5, :]) + mod_ref[3:4, :])

    def scatter_row(r, carry):
        base = (i * TM + r) * 2
        src = hbuf.at[slot, _slab(r)]
        for k in range(2):
            pltpu.make_async_copy(src, xs_hbm.at[_slab(pos_ref[base + k])],
                                  sem.at[slot]).start(priority=k)
        return carry
    lax.fori_loop(0, TM, scatter_row, 0, unroll=8)

    @pl.when(i == n_tiles - 1)
    def _():
        def zero_copy(row, size):
            return pltpu.make_async_copy(zbuf.at[pl.ds(0, size * ROW_SLAB)],
                                         xs_hbm.at[_slab(row, size)], sem_z)

        def pad_expert(e, carry):
            start = meta_ref[e]
            length = meta_ref[N_EXPERTS + e]
            size = MOE_BM // 2
            while size >= 1:
                @pl.when((length & size) != 0)
                def _(size=size):
                    zero_copy(start + (length & ~(2 * size - 1)), size).start()
                size //= 2
            return carry
        lax.fori_loop(0, N_EXPERTS, pad_expert, 0)

        def tail_block(blk, carry):
            zero_copy(blk * MOE_BM, MOE_BM).start()
            return carry
        lax.fori_loop(meta_ref[2 * N_EXPERTS], n_blocks, tail_block, 0)

        wait_slot(slot)

        @pl.when(i >= 1)
        def _():
            wait_slot(1 - slot)

        for _ in range(N_EXPERTS):
            zero_copy(0, MOE_BM).wait()


def _dispatch(x1a, x1b, pos, meta, mods, mod_row_a, tpb_a, mod_row_b, tpb_b, n_blocks):
    tiles_a = x1a.shape[0] // TM
    tiles_b = x1b.shape[0] // TM
    assert n_blocks * MOE_BM - 2 * (tiles_a + tiles_b) * TM == N_EXPERTS * MOE_BM

    def mod_index(i, p, m):
        row = jnp.where(i < tiles_a, mod_row_a(i // tpb_a),
                        mod_row_b(jnp.maximum(i - tiles_a, 0) // tpb_b))
        return (row, 0, 0)

    grid_spec = pltpu.PrefetchScalarGridSpec(
        num_scalar_prefetch=2,
        grid=(tiles_a + tiles_b,),
        in_specs=[
            pl.BlockSpec((TM, D_MODEL), lambda i, p, m: (jnp.minimum(i, tiles_a - 1), 0)),
            pl.BlockSpec((TM, D_MODEL), lambda i, p, m: (jnp.maximum(i - tiles_a, 0), 0)),
            pl.BlockSpec((None, 6, D_MODEL), mod_index),
        ],
        out_specs=pl.BlockSpec(memory_space=pl.ANY),
        scratch_shapes=[
            pltpu.VMEM((2, TM * ROW_SLAB, V7X_LANES), F32),
            pltpu.VMEM((MOE_BM * ROW_SLAB, V7X_LANES), F32),
            pltpu.SemaphoreType.DMA((2,)),
            pltpu.SemaphoreType.DMA(()),
        ],
    )
    return pl.pallas_call(
        functools.partial(_dispatch_kernel, n_blocks=n_blocks, tiles_a=tiles_a),
        out_shape=jax.ShapeDtypeStruct((n_blocks * MOE_BM * ROW_SLAB, V7X_LANES), F32),
        grid_spec=grid_spec,
        compiler_params=_params(("arbitrary",)),
        name="dispatch",
    )(pos, meta, x1a, x1b, mods)


def _expert_kernel(be_ref, nused_ref, x_ref, wg_ref, wu_ref, wd_ref, o_ref, wg_s, wu_s, wd_s):
    i = pl.program_id(0)
    active = i < nused_ref[0]
    changed = (i == 0) | (be_ref[i] != be_ref[jnp.maximum(i - 1, 0)])

    @pl.when(active & changed)
    def _():
        wg_s[...] = wg_ref[...].astype(BF16)
        wu_s[...] = wu_ref[...].astype(BF16)
        wd_s[...] = wd_ref[...].astype(BF16)

    @pl.when(active)
    def _():
        xb = _load_rows(x_ref, MOE_BM).astype(BF16)
        g = jnp.dot(xb, wg_s[...], preferred_element_type=F32)
        u = jnp.dot(xb, wu_s[...], preferred_element_type=F32)
        hb = (g / (1.0 + jnp.exp(-g)) * u).astype(BF16)
        _store_rows(o_ref, jnp.dot(hb, wd_s[...], preferred_element_type=F32))

    @pl.when(jnp.logical_not(active))
    def _():
        o_ref[...] = jnp.zeros_like(o_ref)


def _experts(xs, blk_e, n_used, n_blocks, w_gate, w_up, w_down):
    w_gate = w_gate.reshape(DEPTH * N_EXPERTS, D_MODEL, D_EXPERT)
    w_up = w_up.reshape(DEPTH * N_EXPERTS, D_MODEL, D_EXPERT)
    w_down = w_down.reshape(DEPTH * N_EXPERTS, D_EXPERT, D_MODEL)
    rows_spec = pl.BlockSpec((MOE_BM * ROW_SLAB, V7X_LANES), lambda i, be, nu: (i, 0))
    grid_spec = pltpu.PrefetchScalarGridSpec(
        num_scalar_prefetch=2,
        grid=(n_blocks,),
        in_specs=[
            pl.BlockSpec((MOE_BM * ROW_SLAB, V7X_LANES),
                         lambda i, be, nu: (jnp.minimum(i, nu[0] - 1), 0)),
            pl.BlockSpec((None, D_MODEL, D_EXPERT), lambda i, be, nu: (be[i], 0, 0)),
            pl.BlockSpec((None, D_MODEL, D_EXPERT), lambda i, be, nu: (be[i], 0, 0)),
            pl.BlockSpec((None, D_EXPERT, D_MODEL), lambda i, be, nu: (be[i], 0, 0)),
        ],
        out_specs=rows_spec,
        scratch_shapes=[
            pltpu.VMEM((D_MODEL, D_EXPERT), BF16),
            pltpu.VMEM((D_MODEL, D_EXPERT), BF16),
            pltpu.VMEM((D_EXPERT, D_MODEL), BF16),
        ],
    )
    return pl.pallas_call(
        _expert_kernel,
        out_shape=jax.ShapeDtypeStruct(xs.shape, F32),
        grid_spec=grid_spec,
        compiler_params=_params(("arbitrary",)),
        name="experts",
    )(blk_e, n_used, xs, w_gate, w_up, w_down)


def _combine_kernel(pos_ref, rows_hbm, x1_ref, route_ref, mod_ref, g_ref, b_ref, y_ref, buf, sem):
    i = pl.program_id(0)
    n_tiles = pl.num_programs(0)

    def issue_row(tile, slot, r):
        base = (tile * TM + r) * 2
        for k in range(2):
            pltpu.make_async_copy(rows_hbm.at[_slab(pos_ref[base + k])],
                                  buf.at[slot, k, _slab(r)], sem.at[slot]).start(priority=k)

    def wait_slot(s):
        for k in range(2):
            pltpu.make_async_copy(rows_hbm.at[pl.ds(0, TM * ROW_SLAB)], buf.at[s, k],
                                  sem.at[s]).wait()

    def issue(tile, slot):
        def body(r, carry):
            issue_row(tile, slot, r)
            return carry
        lax.fori_loop(0, TM, body, 0, unroll=8)

    @pl.when(i == 0)
    def _():
        issue(0, 0)

    @pl.when(i + 1 < n_tiles)
    def _():
        issue(i + 1, (i + 1) % 2)

    slot = i % 2
    wait_slot(slot)
    route = route_ref[...]
    moe = (route[:, INFO_W:INFO_W + 1] * _load_rows(buf.at[slot, 0], TM)
           + route[:, INFO_W + 1:INFO_W + 2] * _load_rows(buf.at[slot, 1], TM))
    y_ref[...] = _layer_norm(DEEPNORM_ALPHA * x1_ref[...] + mod_ref[5:6, :] * moe,
                             g_ref[...], b_ref[...])


def _combine(rows, pos, x1, route, mods, mod_row, tpb, ln_g, ln_b):
    n = x1.shape[0]
    tile = lambda i, p: (i, 0)
    grid_spec = pltpu.PrefetchScalarGridSpec(
        num_scalar_prefetch=1,
        grid=(n // TM,),
        in_specs=[
            pl.BlockSpec(memory_space=pl.ANY),
            pl.BlockSpec((TM, D_MODEL), tile),
            pl.BlockSpec((TM, ROUTE_LANES), tile),
            pl.BlockSpec((None, 6, D_MODEL), lambda i, p: (mod_row(i // tpb), 0, 0)),
            pl.BlockSpec((1, D_MODEL), lambda i, p: (0, 0)),
            pl.BlockSpec((1, D_MODEL), lambda i, p: (0, 0)),
        ],
        out_specs=pl.BlockSpec((TM, D_MODEL), tile),
        scratch_shapes=[pltpu.VMEM((2, 2, TM * ROW_SLAB, V7X_LANES), F32),
                        pltpu.SemaphoreType.DMA((2,))],
    )
    return pl.pallas_call(
        _combine_kernel,
        out_shape=jax.ShapeDtypeStruct((n, D_MODEL), F32),
        grid_spec=grid_spec,
        compiler_params=_params(("arbitrary",)),
        name="combine",
    )(pos, rows, x1, route, mods, ln_g.reshape(1, D_MODEL), ln_b.reshape(1, D_MODEL))


def _diff_lambda_init(layer_idx):
    return 0.8 - 0.6 * math.exp(-0.3 * layer_idx)


def _router_weights(w_rg, b_rg, w_re, b_re):
    pad = ROUTE_LANES - N_EXPERTS - N_GROUPS
    w = jnp.concatenate([w_re, w_rg, jnp.zeros((D_MODEL, pad), F32)], axis=1)
    b = jnp.concatenate([b_re, b_rg, jnp.zeros((pad,), F32)]).reshape(1, ROUTE_LANES)
    return w.astype(BF16), b


def kernel(x_prompt, x_sample, cache_a_k, cache_a_v, cache_b_k, cache_b_v, c, c_ctx, w_ada, b_ada, ln_g, ln_b, w_qkv_a, lam_a, subln_a, w_o_a, w_qkv_b, rpb_b, w_o_b, w_rg, b_rg, w_re, b_re, w_gate, w_up, w_down):
    n_mod_rows = 16
    dec_batch = x_sample.shape[0]
    cond = jnp.zeros((n_mod_rows, D_MODEL), F32).at[0].set(c_ctx).at[1:1 + dec_batch].set(c)
    mods_all = _adaln_all(cond, w_ada, b_ada).reshape(DEPTH, n_mod_rows, 6, D_MODEL)
    rope_tables = _rope_tables(x_sample.shape[1])

    xs_in = [x_prompt, x_sample]
    mod_rows = [lambda b: 0, lambda b: 1 + b]
    kv = []
    for i in range(DEPTH):
        mods = mods_all[i]
        j = i // 2
        diff_layer = i % 2 == 0
        if diff_layer:
            w_qkv, w_o = w_qkv_a[j].astype(BF16), w_o_a[j].astype(BF16)
            lam_init = _diff_lambda_init(i)
        else:
            w_qkv, w_o = w_qkv_b[j].astype(BF16), w_o_b[j].astype(BF16)
        w_router, b_router = _router_weights(w_rg[i], b_rg[i], w_re[i], b_re[i])
        routed = []
        for x, mod_row, latent in zip(xs_in, mod_rows, (False, True)):
            B = x.shape[0]
            if diff_layer and latent:
                q, k, v = _qkv(x, mods, w_qkv, mod_row, BF16, rope_tables)
                segs = [(cache_a_k[:, j].reshape(B, -1, D_MODEL),
                         cache_a_v[:, j].reshape(B, -1, D_MODEL)), (k, v)]
                o = _diff_attention(q, segs, lam_a[j], subln_a[j], lam_init)
            elif diff_layer:
                q, k, v = _qkv(x, mods, w_qkv, mod_row, F32)
                kv.append((k, v))
                o = _diff_attention(q, [(k, v)], lam_a[j], subln_a[j], lam_init)
            elif latent:
                q, k, v = _qkv(x, mods, w_qkv, mod_row, BF16)
                o = _neighborhood_attention(q, k, v, cache_b_k[:, j].reshape(B, -1, D_MODEL),
                                            cache_b_v[:, j].reshape(B, -1, D_MODEL), rpb_b[j])
            else:
                q, k, v = _qkv(x, mods, w_qkv, mod_row, F32)
                kv.append((k, v))
                o = _dense_attention(q, k, v)
            routed.append(_proj_route(o, w_o, x, mods, mod_row, ln_g[i, 0], ln_b[i, 0],
                                      w_router, b_router))
        x1s, infos, counts = zip(*routed)
        pos, blk_e, meta, n_used, n_blocks = _route_tables(infos, counts, i)
        tpbs = [x.shape[1] // TM for x in xs_in]
        rows_in = _dispatch(x1s[0], x1s[1], jnp.concatenate(pos), meta, mods,
                            mod_rows[0], tpbs[0], mod_rows[1], tpbs[1], n_blocks)
        rows = _experts(rows_in, blk_e, n_used, n_blocks, w_gate, w_up, w_down)
        xs_in = [_combine(rows, p, x1, info, mods, mod_row, tpb, ln_g[i, 1], ln_b[i, 1]).reshape(x.shape)
                 for p, x1, info, mod_row, tpb, x in zip(pos, x1s, infos, mod_rows, tpbs, xs_in)]
    y_prompt, y_sample = xs_in

    B, T, _ = x_prompt.shape
    new_a_k = kv[0][0].reshape(B, 1, T, 2 * H_A, HD_A)
    new_a_v = kv[0][1].reshape(B, 1, T, H_A, 2 * HD_A)
    new_b_k = kv[1][0].reshape(B, 1, T, H_B, HD_B)
    new_b_v = kv[1][1].reshape(B, 1, T, H_B, HD_B)
    return y_prompt, y_sample, new_a_k, new_a_v, new_b_k, new_b_v
```

```python
import functools
import math

import jax
import jax.numpy as jnp
import numpy as np
from jax import lax
from jax.experimental import pallas as pl
from jax.experimental.pallas import tpu as pltpu

D_MODEL = 1024
DEPTH = 2
GRID_W = 64
HD_A = 64
H_A = D_MODEL // (2 * HD_A)
H_B = 16
HD_B = D_MODEL // H_B
NA_KH = 8
NA_KW = 16
ROPE_BASE = 10000.0
N_GROUPS = 4
EXP_PER_GROUP = 8
N_EXPERTS = N_GROUPS * EXP_PER_GROUP
D_EXPERT = 512
LN_EPS = 1e-5
DEEPNORM_ALPHA = (2.0 * DEPTH) ** 0.25

F32 = jnp.float32
BF16 = jnp.bfloat16

V7X_LANES = 128
V7X_VMEM_BYTES = 64 * 1024 * 1024
VMEM_LIMIT = V7X_VMEM_BYTES * 7 // 8

TM = 256
TM_QKV = 512
TQ = 256
TQ_DIFF = 512
MOE_BM = 256
N_PAIRS = D_MODEL // V7X_LANES
NA_QROWS = TQ // GRID_W
NA_WIN_ROWS = NA_QROWS + NA_KH
NA_WIN = NA_WIN_ROWS * GRID_W
MASK_VALUE = -1e30
ROUTE_LANES = 128


def _params(sem):
    return pltpu.CompilerParams(dimension_semantics=sem, vmem_limit_bytes=VMEM_LIMIT)


def _layer_norm(y, g, b):
    mu = jnp.mean(y, -1, keepdims=True)
    d = y - mu
    var = jnp.mean(d * d, -1, keepdims=True)
    return d * lax.rsqrt(var + LN_EPS) * g + b


def _dot_nt(a, b):
    return lax.dot_general(a, b, (((1,), (1,)), ((), ())), preferred_element_type=F32)


def _ada_kernel(c_ref, w_ref, b_ref, o_ref):
    c = c_ref[...]
    a = c / (1.0 + jnp.exp(-c))
    o_ref[...] = jnp.dot(a, w_ref[...], preferred_element_type=F32,
                         precision=lax.Precision.HIGHEST) + b_ref[...]


def _adaln_all(cond, w_ada, b_ada):
    rows = cond.shape[0]
    tn = 1536
    n6 = 6 * D_MODEL
    return pl.pallas_call(
        _ada_kernel,
        out_shape=jax.ShapeDtypeStruct((DEPTH, rows, n6), F32),
        grid=(DEPTH, n6 // tn),
        in_specs=[
            pl.BlockSpec((rows, D_MODEL), lambda l, j: (0, 0)),
            pl.BlockSpec((None, D_MODEL, tn), lambda l, j: (l, 0, j)),
            pl.BlockSpec((None, 1, tn), lambda l, j: (l, 0, j)),
        ],
        out_specs=pl.BlockSpec((None, rows, tn), lambda l, j: (l, 0, j)),
        compiler_params=_params(("arbitrary", "arbitrary")),
        name="adaln",
    )(cond, w_ada, b_ada.reshape(DEPTH, 1, n6))


def _qkv_kernel(*refs, rope):
    if rope:
        x_ref, mod_ref, w_ref, cos_ref, sin_ref, q_ref, k_ref, v_ref = refs
    else:
        x_ref, mod_ref, w_ref, q_ref, k_ref, v_ref = refs
    h = (x_ref[...] * (1.0 + mod_ref[1:2, :]) + mod_ref[0:1, :]).astype(BF16)
    if rope:
        lane = lax.broadcasted_iota(jnp.int32, (x_ref.shape[0], V7X_LANES), 1)
        first = (lane & (HD_A // 2 - 1)) < (HD_A // 4)
        cos = cos_ref[...]
        sin = sin_ref[...]
    for idx, o_ref in enumerate((q_ref, k_ref, v_ref)):
        y = jnp.dot(h, w_ref[:, idx * D_MODEL:(idx + 1) * D_MODEL], preferred_element_type=F32)
        if rope and idx < 2:
            for c in range(N_PAIRS):
                sl = slice(c * V7X_LANES, (c + 1) * V7X_LANES)
                yc = y[:, sl]
                partner = jnp.where(first, pltpu.roll(yc, V7X_LANES - HD_A // 4, 1),
                                    pltpu.roll(yc, HD_A // 4, 1))
                o_ref[:, sl] = (yc * cos + partner * sin).astype(o_ref.dtype)
        else:
            o_ref[...] = y.astype(o_ref.dtype)


def _qkv(x, mods, w_bf16, mod_row, kv_dtype, rope_tables=None):
    B, T, _ = x.shape
    n = B * T
    tm = min(TM_QKV, T)
    tpb = T // tm
    rope = rope_tables is not None
    in_specs = [
        pl.BlockSpec((tm, D_MODEL), lambda i: (i, 0)),
        pl.BlockSpec((None, 6, D_MODEL), lambda i: (mod_row(i // tpb), 0, 0)),
        pl.BlockSpec((D_MODEL, 3 * D_MODEL), lambda i: (0, 0)),
    ]
    args = [x.reshape(n, D_MODEL), mods, w_bf16]
    if rope:
        in_specs += [pl.BlockSpec((tm, V7X_LANES), lambda i: (i % tpb, 0))] * 2
        args += list(rope_tables)
    q, k, v = pl.pallas_call(
        functools.partial(_qkv_kernel, rope=rope),
        out_shape=(jax.ShapeDtypeStruct((n, D_MODEL), BF16),
                   jax.ShapeDtypeStruct((n, D_MODEL), kv_dtype),
                   jax.ShapeDtypeStruct((n, D_MODEL), kv_dtype)),
        grid=(n // tm,),
        in_specs=in_specs,
        out_specs=(pl.BlockSpec((tm, D_MODEL), lambda i: (i, 0)),) * 3,
        compiler_params=_params(("arbitrary",)),
        name="qkv",
    )(*args)
    shp = (B, T, D_MODEL)
    return q.reshape(shp), k.reshape(shp), v.reshape(shp)


def _rope_tables(n_tok):
    t = np.arange(n_tok)
    pos = jnp.asarray(np.stack([t // GRID_W, t % GRID_W], -1), F32)
    quarter = HD_A // 4
    inv_freq = ROPE_BASE ** (-jnp.arange(quarter, dtype=F32) / quarter)
    ang = pos[:, :, None] * inv_freq
    cos, sin = jnp.cos(ang), jnp.sin(ang)
    cos_h = jnp.broadcast_to(cos[:, :, None, :], (n_tok, 2, 2, quarter)).reshape(n_tok, HD_A)
    sin_h = jnp.stack([-sin, sin], axis=2).reshape(n_tok, HD_A)
    reps = V7X_LANES // HD_A
    return jnp.tile(cos_h, (1, reps)), jnp.tile(sin_h, (1, reps))


def _pair_scores(qc, key_chunks, biases=None):
    lo = lax.broadcasted_iota(jnp.int32, qc.shape, 1) < V7X_LANES // 2
    zero = jnp.zeros_like(qc)
    qq = jnp.concatenate([jnp.where(lo, qc, zero), jnp.where(lo, zero, qc)], axis=0)
    scores = [_dot_nt(qq, kc) for kc in key_chunks]
    if biases is not None:
        scores = [s if b is None else s + b for s, b in zip(scores, biases)]
    return scores


def _pair_softmax_pv(scores, value_chunks):
    m = functools.reduce(jnp.maximum, [jnp.max(s, -1, keepdims=True) for s in scores])
    acc = None
    for s, vc in zip(scores, value_chunks):
        p = jnp.exp(s - m).astype(BF16)
        v1 = jnp.concatenate([vc, jnp.ones_like(vc)], axis=1)
        part = jnp.dot(p, v1, preferred_element_type=F32)
        acc = part if acc is None else acc + part
    return acc[:, :V7X_LANES] / acc[:, V7X_LANES:]


PAIR_LOOKAHEAD = 1


def _pipelined_pairs(n_chunks, scores_fn, finish_fn):
    ready = [scores_fn(c) for c in range(min(PAIR_LOOKAHEAD, n_chunks))]
    for c in range(n_chunks):
        if c + PAIR_LOOKAHEAD < n_chunks:
            ready.append(scores_fn(c + PAIR_LOOKAHEAD))
        finish_fn(c, ready.pop(0))


def _diff_attn_kernel(*refs, nseg, lam_init):
    lam_ref, subln_ref, q_ref = refs[:3]
    seg_refs = refs[3:3 + 2 * nseg]
    o_ref = refs[3 + 2 * nseg]
    lp = lam_ref[...]
    lam = (jnp.exp(jnp.sum(lp[0:1] * lp[1:2], keepdims=True))
           - jnp.exp(jnp.sum(lp[2:3] * lp[3:4], keepdims=True)) + lam_init)
    scale = HD_A ** -0.5
    def lanes(h):
        return slice(h * V7X_LANES, (h + 1) * V7X_LANES)

    def scores(h):
        return _pair_scores(q_ref[:, lanes(h)] * scale,
                            [seg_refs[2 * j][:, lanes(h)].astype(BF16) for j in range(nseg)])

    def finish(h, s):
        o2 = _pair_softmax_pv(s, [seg_refs[2 * j + 1][:, lanes(h)].astype(BF16)
                                  for j in range(nseg)])
        tq = o2.shape[0] // 2
        acc = o2[:tq] - lam * o2[tq:]
        ms = jnp.mean(acc * acc, -1, keepdims=True)
        o = acc * lax.rsqrt(ms + LN_EPS) * subln_ref[...] * (1.0 - lam_init)
        o_ref[:, lanes(h)] = o.astype(BF16)

    _pipelined_pairs(H_A, scores, finish)


def _diff_attention(q, segs, lam_p, subln, lam_init):
    B, T, _ = q.shape
    tq = min(TQ_DIFF, T)
    in_specs = [
        pl.BlockSpec((4, HD_A), lambda b, i: (0, 0)),
        pl.BlockSpec((1, 2 * HD_A), lambda b, i: (0, 0)),
        pl.BlockSpec((None, tq, D_MODEL), lambda b, i: (b, i, 0)),
    ]
    args = [lam_p, subln.reshape(1, 2 * HD_A), q]
    for k, v in segs:
        tk = k.shape[1]
        in_specs += [pl.BlockSpec((None, tk, D_MODEL), lambda b, i: (b, 0, 0))] * 2
        args += [k, v]
    return pl.pallas_call(
        functools.partial(_diff_attn_kernel, nseg=len(segs), lam_init=lam_init),
        out_shape=jax.ShapeDtypeStruct((B, T, D_MODEL), BF16),
        grid=(B, T // tq),
        in_specs=in_specs,
        out_specs=pl.BlockSpec((None, tq, D_MODEL), lambda b, i: (b, i, 0)),
        compiler_params=_params(("arbitrary", "arbitrary")),
        name="diff_attn",
    )(*args)


def _merge_pair(o2):
    tq = o2.shape[0] // 2
    lo = lax.broadcasted_iota(jnp.int32, (tq, V7X_LANES), 1) < V7X_LANES // 2
    return jnp.where(lo, o2[:tq], o2[tq:])


def _dense_attn_kernel(q_ref, k_ref, v_ref, o_ref):
    scale = HD_B ** -0.5
    for c in range(N_PAIRS):
        sl = slice(c * V7X_LANES, (c + 1) * V7X_LANES)
        s = _pair_scores(q_ref[:, sl] * scale, [k_ref[:, sl].astype(BF16)])
        o2 = _pair_softmax_pv(s, [v_ref[:, sl].astype(BF16)])
        o_ref[:, sl] = _merge_pair(o2).astype(BF16)


def _dense_attention(q, k, v):
    B, T, _ = q.shape
    spec = pl.BlockSpec((None, T, D_MODEL), lambda b: (b, 0, 0))
    return pl.pallas_call(
        _dense_attn_kernel,
        out_shape=jax.ShapeDtypeStruct((B, T, D_MODEL), BF16),
        grid=(B,),
        in_specs=[spec, spec, spec],
        out_specs=spec,
        compiler_params=_params(("arbitrary",)),
        name="dense_attn",
    )(q, k, v)


NA_WIN_BLOCKS = NA_WIN // TQ


def _na_attn_kernel(*refs):
    q_ref = refs[0]
    k_refs = refs[1:1 + NA_WIN_BLOCKS]
    v_refs = refs[1 + NA_WIN_BLOCKS:1 + 2 * NA_WIN_BLOCKS]
    kc_ref, vc_ref, bias_ref, o_ref = refs[1 + 2 * NA_WIN_BLOCKS:]
    scale = HD_B ** -0.5
    def lanes(c):
        return slice(c * V7X_LANES, (c + 1) * V7X_LANES)

    def scores(c):
        kwin = jnp.concatenate([r[:, lanes(c)] for r in k_refs], axis=0)
        bias = bias_ref[2 * c:2 * c + 2].reshape(2 * TQ, NA_WIN)
        return _pair_scores(q_ref[:, lanes(c)] * scale, [kwin, kc_ref[:, lanes(c)].astype(BF16)],
                            biases=[bias, None])

    def finish(c, s):
        vwin = jnp.concatenate([r[:, lanes(c)] for r in v_refs], axis=0)
        o2 = _pair_softmax_pv(s, [vwin, vc_ref[:, lanes(c)].astype(BF16)])
        o_ref[:, lanes(c)] = _merge_pair(o2).astype(BF16)

    _pipelined_pairs(N_PAIRS, scores, finish)


def _na_bias_tables(rpb, rows):
    kh = min(NA_KH, rows)
    n_heads = rpb.shape[0]
    qcol = np.arange(GRID_W)[:, None]
    kcol = np.arange(GRID_W)[None, :]
    c_start = np.clip(qcol - NA_KW // 2, 0, GRID_W - NA_KW)
    col_ok = (kcol >= c_start) & (kcol < c_start + NA_KW)
    col_off = np.clip(kcol - qcol + NA_KW - 1, 0, 2 * NA_KW - 2)
    onehot = (col_off[None] == np.arange(2 * NA_KW - 1)[:, None, None]).astype(np.float32)
    cols = jnp.einsum('hrc,cqk->hrqk', rpb, jnp.asarray(onehot), precision=lax.Precision.HIGHEST)
    cols = jnp.where(col_ok[None, None], cols, MASK_VALUE)
    padded = jnp.pad(cols, ((0, 0), (1, 1), (0, 0), (0, 0)), constant_values=MASK_VALUE)
    pairs = jnp.concatenate([padded[:, :-1], padded[:, 1:]], axis=-1)
    plan = []
    for r0 in (0, NA_QROWS, rows - NA_QROWS):
        ws = int(np.clip(r0 - NA_KH // 2, 0, rows - NA_WIN_ROWS))
        for qrow in range(r0, r0 + NA_QROWS):
            r_start = int(np.clip(qrow - kh // 2, 0, rows - kh))
            chunks = []
            for c in range(NA_WIN_ROWS // 2):
                krow = ws + 2 * c
                in_band = [r_start <= krow + s < r_start + kh for s in (0, 1)]
                chunks.append((krow - qrow + NA_KH - 1, in_band[0], in_band[1]))
            plan.append(chunks)
    return pl.pallas_call(
        functools.partial(_na_bias_kernel, plan=plan),
        out_shape=jax.ShapeDtypeStruct((3, n_heads, TQ, NA_WIN), F32),
        grid=(n_heads,),
        in_specs=[pl.BlockSpec((None, 2 * NA_KH, GRID_W, 2 * GRID_W), lambda h: (h, 0, 0, 0))],
        out_specs=pl.BlockSpec((3, None, TQ, NA_WIN), lambda h: (0, h, 0, 0)),
        compiler_params=_params(("arbitrary",)),
        name="na_bias",
    )(pairs)


def _na_bias_kernel(pairs_ref, o_ref, *, plan):
    left = lax.broadcasted_iota(jnp.int32, (GRID_W, 2 * GRID_W), 1) < GRID_W
    masked = jnp.full((GRID_W, 2 * GRID_W), MASK_VALUE, F32)
    for t, chunks in enumerate(plan):
        table, qrow = divmod(t, NA_QROWS)
        for c, (row_off, left_ok, right_ok) in enumerate(chunks):
            if left_ok or right_ok:
                piece = pairs_ref[row_off + 1]
                if not left_ok:
                    piece = jnp.where(left, masked, piece)
                if not right_ok:
                    piece = jnp.where(left, piece, masked)
            else:
                piece = masked
            o_ref[table, qrow * GRID_W:(qrow + 1) * GRID_W,
                  c * 2 * GRID_W:(c + 1) * 2 * GRID_W] = piece


def _neighborhood_attention(q, k, v, k_ctx, v_ctx, rpb):
    B, S, _ = q.shape
    P = k_ctx.shape[1]
    rows = S // GRID_W
    nqb = S // TQ
    assert rows >= NA_WIN_ROWS and nqb >= NA_WIN_BLOCKS
    bias = _na_bias_tables(rpb, rows)

    def bias_cfg(qb):
        return jnp.where(qb == 0, 0, jnp.where(qb == nqb - 1, 2, 1))

    def win_spec(j):
        return pl.BlockSpec((None, TQ, D_MODEL),
                            lambda qb, b: (b, jnp.clip(qb - 1, 0, nqb - NA_WIN_BLOCKS) + j, 0))

    win_specs = [win_spec(j) for j in range(NA_WIN_BLOCKS)]
    ctx_spec = pl.BlockSpec((None, P, D_MODEL), lambda qb, b: (b, 0, 0))
    q_spec = pl.BlockSpec((None, TQ, D_MODEL), lambda qb, b: (b, qb, 0))
    bias_spec = pl.BlockSpec((None, H_B, TQ, NA_WIN), lambda qb, b: (bias_cfg(qb), 0, 0, 0),
                             pipeline_mode=pl.Buffered(1))
    return pl.pallas_call(
        _na_attn_kernel,
        out_shape=jax.ShapeDtypeStruct((B, S, D_MODEL), BF16),
        grid=(nqb, B),
        in_specs=[q_spec] + win_specs + win_specs + [ctx_spec, ctx_spec, bias_spec],
        out_specs=q_spec,
        compiler_params=_params(("arbitrary", "arbitrary")),
        name="na_attn",
    )(q, *([k] * NA_WIN_BLOCKS), *([v] * NA_WIN_BLOCKS), k_ctx, v_ctx, bias)


INFO_E, INFO_W, INFO_RANK = 0, 2, 4


ROUTE_ROWS = 64


def _route(lt):
    sub = lax.broadcasted_iota(jnp.int32, lt.shape, 0).astype(F32)
    neg = -jnp.inf
    gmask = (sub >= N_EXPERTS) & (sub < N_EXPERTS + N_GROUPS)
    gl = jnp.where(gmask, lt, neg)
    gmax = jnp.max(gl, 0, keepdims=True)
    g_sel = jnp.min(jnp.where(gl == gmax, sub - N_EXPERTS, float(N_GROUPS)), 0, keepdims=True)
    p_g = 1.0 / jnp.sum(jnp.where(gmask, jnp.exp(gl - gmax), 0.0), 0, keepdims=True)
    emask = (sub >= g_sel * EXP_PER_GROUP) & (sub < (g_sel + 1.0) * EXP_PER_GROUP)
    el = jnp.where(emask, lt, neg)
    v1 = jnp.max(el, 0, keepdims=True)
    e1 = jnp.min(jnp.where(el == v1, sub, float(ROUTE_ROWS)), 0, keepdims=True)
    el2 = jnp.where(sub == e1, neg, el)
    v2 = jnp.max(el2, 0, keepdims=True)
    e2 = jnp.min(jnp.where(el2 == v2, sub, float(ROUTE_ROWS)), 0, keepdims=True)
    e21 = jnp.exp(v2 - v1)
    w1 = p_g / (1.0 + e21)
    w2 = p_g * e21 / (1.0 + e21)
    return sub, e1, e2, w1, w2


def _proj_route_kernel(o_ref, w_ref, x_ref, mod_ref, g_ref, b_ref, wr_ref, br_ref,
                       x1_ref, info_ref, cnt_ref, run):
    @pl.when(pl.program_id(0) == 0)
    def _():
        run[...] = jnp.zeros_like(run)

    out = jnp.dot(o_ref[...], w_ref[...], preferred_element_type=F32)
    x1 = _layer_norm(DEEPNORM_ALPHA * x_ref[...] + mod_ref[2:3, :] * out, g_ref[...], b_ref[...])
    x1_ref[...] = x1
    h2 = x1 * (1.0 + mod_ref[4:5, :]) + mod_ref[3:4, :]
    logits = jnp.dot(h2.astype(BF16), wr_ref[...], preferred_element_type=F32) + br_ref[...]
    sub, e1, e2, w1, w2 = _route(logits.T[:ROUTE_ROWS])

    oh1 = sub == e1
    oh2 = sub == e2
    oh = jnp.where(oh1 | oh2, 1.0, 0.0)
    t_row = lax.broadcasted_iota(jnp.int32, (TM, TM), 0)
    t_col = lax.broadcasted_iota(jnp.int32, (TM, TM), 1)
    earlier = jnp.where(t_row < t_col, 1.0, 0.0).astype(BF16)
    base = run[:, 0:1] + jnp.dot(oh.astype(BF16), earlier, preferred_element_type=F32)
    rank1 = jnp.sum(jnp.where(oh1, base, 0.0), 0, keepdims=True)
    rank2 = jnp.sum(jnp.where(oh2, base, 0.0), 0, keepdims=True)
    run[...] = run[...] + jnp.sum(oh, 1, keepdims=True)
    cnt_ref[...] = run[...]

    rec_row = lax.broadcasted_iota(jnp.int32, (ROUTE_LANES, TM), 0)
    rec = jnp.zeros((ROUTE_LANES, TM), F32)
    for row, val in ((INFO_E, e1), (INFO_E + 1, e2), (INFO_W, w1), (INFO_W + 1, w2),
                     (INFO_RANK, rank1), (INFO_RANK + 1, rank2)):
        rec = jnp.where(rec_row == row, val, rec)
    info_ref[...] = rec.T


def _proj_route(o, w_o_bf16, x, mods, mod_row, ln_g, ln_b, w_router, b_router):
    B, T, _ = x.shape
    n = B * T
    tpb = T // TM
    tile = pl.BlockSpec((TM, D_MODEL), lambda i: (i, 0))
    vec = pl.BlockSpec((1, D_MODEL), lambda i: (0, 0))
    return pl.pallas_call(
        _proj_route_kernel,
        out_shape=(jax.ShapeDtypeStruct((n, D_MODEL), F32),
                   jax.ShapeDtypeStruct((n, ROUTE_LANES), F32),
                   jax.ShapeDtypeStruct((ROUTE_ROWS, ROUTE_LANES), F32)),
        grid=(n // TM,),
        in_specs=[tile,
                  pl.BlockSpec((D_MODEL, D_MODEL), lambda i: (0, 0)),
                  tile,
                  pl.BlockSpec((None, 6, D_MODEL), lambda i: (mod_row(i // tpb), 0, 0)),
                  vec, vec,
                  pl.BlockSpec((D_MODEL, ROUTE_LANES), lambda i: (0, 0)),
                  pl.BlockSpec((1, ROUTE_LANES), lambda i: (0, 0))],
        out_specs=(tile,
                   pl.BlockSpec((TM, ROUTE_LANES), lambda i: (i, 0)),
                   pl.BlockSpec((ROUTE_ROWS, ROUTE_LANES), lambda i: (0, 0))),
        scratch_shapes=[pltpu.VMEM((ROUTE_ROWS, ROUTE_LANES), F32)],
        compiler_params=_params(("arbitrary",)),
        name="proj_route",
    )(o.reshape(n, D_MODEL), w_o_bf16, x.reshape(n, D_MODEL), mods,
      ln_g.reshape(1, D_MODEL), ln_b.reshape(1, D_MODEL), w_router, b_router)


def _route_tables(infos, counts, layer):
    cnts = [c[:N_EXPERTS, 0].astype(jnp.int32) for c in counts]
    cnt = functools.reduce(lambda a, b: a + b, cnts)
    padded = (cnt + MOE_BM - 1) // MOE_BM * MOE_BM
    pend = jnp.cumsum(padded)
    n_pairs = 2 * sum(info.shape[0] for info in infos)
    n_blocks = n_pairs // MOE_BM + N_EXPERTS
    n_used = pend[-1:] // MOE_BM
    experts = jnp.arange(N_EXPERTS, dtype=jnp.int32)
    pos = []
    first = pend - padded
    for info, c in zip(infos, cnts):
        e = info[:, INFO_E:INFO_E + 2].astype(jnp.int32)
        rank = info[:, INFO_RANK:INFO_RANK + 2].astype(jnp.int32)
        p = jnp.sum(jnp.where(e[..., None] == experts, first, 0), -1) + rank
        pos.append(p.reshape(-1).astype(jnp.int32))
        first = first + c
    blk = jnp.minimum(jnp.arange(n_blocks, dtype=jnp.int32), n_used - 1) * MOE_BM
    blk_e = jnp.sum((blk[:, None] >= pend[None, :]).astype(jnp.int32), axis=1) + layer * N_EXPERTS
    meta = jnp.concatenate([first, padded - cnt, n_used]).astype(jnp.int32)
    return pos, blk_e.astype(jnp.int32), meta, n_used, n_blocks


ROW_SLAB = D_MODEL // V7X_LANES


def _slab(row, n_rows=1):
    return pl.ds(pl.multiple_of(row * ROW_SLAB, ROW_SLAB), n_rows * ROW_SLAB)


def _store_rows(ref, x):
    rows = x.shape[0]
    for c in range(ROW_SLAB):
        ref[pl.ds(c, rows, stride=ROW_SLAB), :] = x[:, c * V7X_LANES:(c + 1) * V7X_LANES]


def _load_rows(ref, rows):
    return jnp.concatenate([ref[pl.ds(c, rows, stride=ROW_SLAB), :] for c in range(ROW_SLAB)], axis=1)


def _dispatch_kernel(pos_ref, meta_ref, x1a_ref, x1b_ref, mod_ref, xs_hbm, hbuf, zbuf, sem, sem_z, *,
                     n_blocks, tiles_a):
    i = pl.program_id(0)
    n_tiles = pl.num_programs(0)
    slot = i % 2

    @pl.when(i == 0)
    def _():
        zbuf[...] = jnp.zeros_like(zbuf)

    def wait_slot(s):
        for _ in range(2):
            pltpu.make_async_copy(hbuf.at[s], xs_hbm.at[pl.ds(0, TM * ROW_SLAB)], sem.at[s]).wait()

    @pl.when(i >= 2)
    def _():
        wait_slot(slot)

    @pl.when(i < tiles_a)
    def _():
        _store_rows(hbuf.at[slot], x1a_ref[...] * (1.0 + mod_ref[4:5, :]) + mod_ref[3:4, :])

    @pl.when(i >= tiles_a)
    def _():
        _store_rows(hbuf.at[slot], x1b_ref[...] * (1.0 + mod_ref[4:5, :]) + mod_ref[3:4, :])

    def scatter_row(r, carry):
        base = (i * TM + r) * 2
        src = hbuf.at[slot, _slab(r)]
        for k in range(2):
            pltpu.make_async_copy(src, xs_hbm.at[_slab(pos_ref[base + k])],
                                  sem.at[slot]).start(priority=k)
        return carry
    lax.fori_loop(0, TM, scatter_row, 0, unroll=8)

    @pl.when(i == n_tiles - 1)
    def _():
        def zero_copy(row, size):
            return pltpu.make_async_copy(zbuf.at[pl.ds(0, size * ROW_SLAB)],
                                         xs_hbm.at[_slab(row, size)], sem_z)

        def pad_expert(e, carry):
            start = meta_ref[e]
            length = meta_ref[N_EXPERTS + e]
            size = MOE_BM // 2
            while size >= 1:
                @pl.when((length & size) != 0)
                def _(size=size):
                    zero_copy(start + (length & ~(2 * size - 1)), size).start()
                size //= 2
            return carry
        lax.fori_loop(0, N_EXPERTS, pad_expert, 0)

        def tail_block(blk, carry):
            zero_copy(blk * MOE_BM, MOE_BM).start()
            return carry
        lax.fori_loop(meta_ref[2 * N_EXPERTS], n_blocks, tail_block, 0)

        wait_slot(slot)

        @pl.when(i >= 1)
        def _():
            wait_slot(1 - slot)

        for _ in range(N_EXPERTS):
            zero_copy(0, MOE_BM).wait()


def _dispatch(x1a, x1b, pos, meta, mods, mod_row_a, tpb_a, mod_row_b, tpb_b, n_blocks):
    tiles_a = x1a.shape[0] // TM
    tiles_b = x1b.shape[0] // TM
    assert n_blocks * MOE_BM - 2 * (tiles_a + tiles_b) * TM == N_EXPERTS * MOE_BM

    def mod_index(i, p, m):
        row = jnp.where(i < tiles_a, mod_row_a(i // tpb_a),
                        mod_row_b(jnp.maximum(i - tiles_a, 0) // tpb_b))
        return (row, 0, 0)

    grid_spec = pltpu.PrefetchScalarGridSpec(
        num_scalar_prefetch=2,
        grid=(tiles_a + tiles_b,),
        in_specs=[
            pl.BlockSpec((TM, D_MODEL), lambda i, p, m: (jnp.minimum(i, tiles_a - 1), 0)),
            pl.BlockSpec((TM, D_MODEL), lambda i, p, m: (jnp.maximum(i - tiles_a, 0), 0)),
            pl.BlockSpec((None, 6, D_MODEL), mod_index),
        ],
        out_specs=pl.BlockSpec(memory_space=pl.ANY),
        scratch_shapes=[
            pltpu.VMEM((2, TM * ROW_SLAB, V7X_LANES), F32),
            pltpu.VMEM((MOE_BM * ROW_SLAB, V7X_LANES), F32),
            pltpu.SemaphoreType.DMA((2,)),
            pltpu.SemaphoreType.DMA(()),
        ],
    )
    return pl.pallas_call(
        functools.partial(_dispatch_kernel, n_blocks=n_blocks, tiles_a=tiles_a),
        out_shape=jax.ShapeDtypeStruct((n_blocks * MOE_BM * ROW_SLAB, V7X_LANES), F32),
        grid_spec=grid_spec,
        compiler_params=_params(("arbitrary",)),
        name="dispatch",
    )(pos, meta, x1a, x1b, mods)


def _expert_kernel(be_ref, nused_ref, x_ref, wg_ref, wu_ref, wd_ref, o_ref, wg_s, wu_s, wd_s):
    i = pl.program_id(0)
    active = i < nused_ref[0]
    changed = (i == 0) | (be_ref[i] != be_ref[jnp.maximum(i - 1, 0)])

    @pl.when(active & changed)
    def _():
        wg_s[...] = wg_ref[...].astype(BF16)
        wu_s[...] = wu_ref[...].astype(BF16)
        wd_s[...] = wd_ref[...].astype(BF16)

    @pl.when(active)
    def _():
        xb = _load_rows(x_ref, MOE_BM).astype(BF16)
        g = jnp.dot(xb, wg_s[...], preferred_element_type=F32)
        u = jnp.dot(xb, wu_s[...], preferred_element_type=F32)
        hb = (g / (1.0 + jnp.exp(-g)) * u).astype(BF16)
        _store_rows(o_ref, jnp.dot(hb, wd_s[...], preferred_element_type=F32))

    @pl.when(jnp.logical_not(active))
    def _():
        o_ref[...] = jnp.zeros_like(o_ref)


def _experts(xs, blk_e, n_used, n_blocks, w_gate, w_up, w_down):
    w_gate = w_gate.reshape(DEPTH * N_EXPERTS, D_MODEL, D_EXPERT)
    w_up = w_up.reshape(DEPTH * N_EXPERTS, D_MODEL, D_EXPERT)
    w_down = w_down.reshape(DEPTH * N_EXPERTS, D_EXPERT, D_MODEL)
    rows_spec = pl.BlockSpec((MOE_BM * ROW_SLAB, V7X_LANES), lambda i, be, nu: (i, 0))
    grid_spec = pltpu.PrefetchScalarGridSpec(
        num_scalar_prefetch=2,
        grid=(n_blocks,),
        in_specs=[
            pl.BlockSpec((MOE_BM * ROW_SLAB, V7X_LANES),
                         lambda i, be, nu: (jnp.minimum(i, nu[0] - 1), 0)),
            pl.BlockSpec((None, D_MODEL, D_EXPERT), lambda i, be, nu: (be[i], 0, 0)),
            pl.BlockSpec((None, D_MODEL, D_EXPERT), lambda i, be, nu: (be[i], 0, 0)),
            pl.BlockSpec((None, D_EXPERT, D_MODEL), lambda i, be, nu: (be[i], 0, 0)),
        ],
        out_specs=rows_spec,
        scratch_shapes=[
            pltpu.VMEM((D_MODEL, D_EXPERT), BF16),
            pltpu.VMEM((D_MODEL, D_EXPERT), BF16),
            pltpu.VMEM((D_EXPERT, D_MODEL), BF16),
        ],
    )
    return pl.pallas_call(
        _expert_kernel,
        out_shape=jax.ShapeDtypeStruct(xs.shape, F32),
        grid_spec=grid_spec,
        compiler_params=_params(("arbitrary",)),
        name="experts",
    )(blk_e, n_used, xs, w_gate, w_up, w_down)


def _combine_kernel(pos_ref, rows_hbm, x1_ref, route_ref, mod_ref, g_ref, b_ref, y_ref, buf, sem):
    i = pl.program_id(0)
    n_tiles = pl.num_programs(0)

    def issue_row(tile, slot, r):
        base = (tile * TM + r) * 2
        for k in range(2):
            pltpu.make_async_copy(rows_hbm.at[_slab(pos_ref[base + k])],
                                  buf.at[slot, k, _slab(r)], sem.at[slot]).start(priority=k)

    def wait_slot(s):
        for k in range(2):
            pltpu.make_async_copy(rows_hbm.at[pl.ds(0, TM * ROW_SLAB)], buf.at[s, k],
                                  sem.at[s]).wait()

    def issue(tile, slot):
        def body(r, carry):
            issue_row(tile, slot, r)
            return carry
        lax.fori_loop(0, TM, body, 0, unroll=8)

    @pl.when(i == 0)
    def _():
        issue(0, 0)

    @pl.when(i + 1 < n_tiles)
    def _():
        issue(i + 1, (i + 1) % 2)

    slot = i % 2
    wait_slot(slot)
    route = route_ref[...]
    moe = (route[:, INFO_W:INFO_W + 1] * _load_rows(buf.at[slot, 0], TM)
           + route[:, INFO_W + 1:INFO_W + 2] * _load_rows(buf.at[slot, 1], TM))
    y_ref[...] = _layer_norm(DEEPNORM_ALPHA * x1_ref[...] + mod_ref[5:6, :] * moe,
                             g_ref[...], b_ref[...])


def _combine(rows, pos, x1, route, mods, mod_row, tpb, ln_g, ln_b):
    n = x1.shape[0]
    tile = lambda i, p: (i, 0)
    grid_spec = pltpu.PrefetchScalarGridSpec(
        num_scalar_prefetch=1,
        grid=(n // TM,),
        in_specs=[
            pl.BlockSpec(memory_space=pl.ANY),
            pl.BlockSpec((TM, D_MODEL), tile),
            pl.BlockSpec((TM, ROUTE_LANES), tile),
            pl.BlockSpec((None, 6, D_MODEL), lambda i, p: (mod_row(i // tpb), 0, 0)),
            pl.BlockSpec((1, D_MODEL), lambda i, p: (0, 0)),
            pl.BlockSpec((1, D_MODEL), lambda i, p: (0, 0)),
        ],
        out_specs=pl.BlockSpec((TM, D_MODEL), tile),
        scratch_shapes=[pltpu.VMEM((2, 2, TM * ROW_SLAB, V7X_LANES), F32),
                        pltpu.SemaphoreType.DMA((2,))],
    )
    return pl.pallas_call(
        _combine_kernel,
        out_shape=jax.ShapeDtypeStruct((n, D_MODEL), F32),
        grid_spec=grid_spec,
        compiler_params=_params(("arbitrary",)),
        name="combine",
    )(pos, rows, x1, route, mods, ln_g.reshape(1, D_MODEL), ln_b.reshape(1, D_MODEL))


def _diff_lambda_init(layer_idx):
    return 0.8 - 0.6 * math.exp(-0.3 * layer_idx)


def _router_weights(w_rg, b_rg, w_re, b_re):
    pad = ROUTE_LANES - N_EXPERTS - N_GROUPS
    w = jnp.concatenate([w_re, w_rg, jnp.zeros((D_MODEL, pad), F32)], axis=1)
    b = jnp.concatenate([b_re, b_rg, jnp.zeros((pad,), F32)]).reshape(1, ROUTE_LANES)
    return w.astype(BF16), b


def kernel(x_prompt, x_sample, cache_a_k, cache_a_v, cache_b_k, cache_b_v, c, c_ctx, w_ada, b_ada, ln_g, ln_b, w_qkv_a, lam_a, subln_a, w_o_a, w_qkv_b, rpb_b, w_o_b, w_rg, b_rg, w_re, b_re, w_gate, w_up, w_down):
    n_mod_rows = 16
    dec_batch = x_sample.shape[0]
    cond = jnp.zeros((n_mod_rows, D_MODEL), F32).at[0].set(c_ctx).at[1:1 + dec_batch].set(c)
    mods_all = _adaln_all(cond, w_ada, b_ada).reshape(DEPTH, n_mod_rows, 6, D_MODEL)
    rope_tables = _rope_tables(x_sample.shape[1])

    xs_in = [x_prompt, x_sample]
    mod_rows = [lambda b: 0, lambda b: 1 + b]
    kv = []
    for i in range(DEPTH):
        mods = mods_all[i]
        j = i // 2
        diff_layer = i % 2 == 0
        if diff_layer:
            w_qkv, w_o = w_qkv_a[j].astype(BF16), w_o_a[j].astype(BF16)
            lam_init = _diff_lambda_init(i)
        else:
            w_qkv, w_o = w_qkv_b[j].astype(BF16), w_o_b[j].astype(BF16)
        w_router, b_router = _router_weights(w_rg[i], b_rg[i], w_re[i], b_re[i])
        routed = []
        for x, mod_row, latent in zip(xs_in, mod_rows, (False, True)):
            B = x.shape[0]
            if diff_layer and latent:
                q, k, v = _qkv(x, mods, w_qkv, mod_row, BF16, rope_tables)
                segs = [(cache_a_k[:, j].reshape(B, -1, D_MODEL),
                         cache_a_v[:, j].reshape(B, -1, D_MODEL)), (k, v)]
                o = _diff_attention(q, segs, lam_a[j], subln_a[j], lam_init)
            elif diff_layer:
                q, k, v = _qkv(x, mods, w_qkv, mod_row, F32)
                kv.append((k, v))
                o = _diff_attention(q, [(k, v)], lam_a[j], subln_a[j], lam_init)
            elif latent:
                q, k, v = _qkv(x, mods, w_qkv, mod_row, BF16)
                o = _neighborhood_attention(q, k, v, cache_b_k[:, j].reshape(B, -1, D_MODEL),
                                            cache_b_v[:, j].reshape(B, -1, D_MODEL), rpb_b[j])
            else:
                q, k, v = _qkv(x, mods, w_qkv, mod_row, F32)
                kv.append((k, v))
                o = _dense_attention(q, k, v)
            routed.append(_proj_route(o, w_o, x, mods, mod_row, ln_g[i, 0], ln_b[i, 0],
                                      w_router, b_router))
        x1s, infos, counts = zip(*routed)
        pos, blk_e, meta, n_used, n_blocks = _route_tables(infos, counts, i)
        tpbs = [x.shape[1] // TM for x in xs_in]
        rows_in = _dispatch(x1s[0], x1s[1], jnp.concatenate(pos), meta, mods,
                            mod_rows[0], tpbs[0], mod_rows[1], tpbs[1], n_blocks)
        rows = _experts(rows_in, blk_e, n_used, n_blocks, w_gate, w_up, w_down)
        xs_in = [_combine(rows, p, x1, info, mods, mod_row, tpb, ln_g[i, 1], ln_b[i, 1]).reshape(x.shape)
                 for p, x1, info, mod_row, tpb, x in zip(pos, x1s, infos, mod_rows, tpbs, xs_in)]
    y_prompt, y_sample = xs_in

    B, T, _ = x_prompt.shape
    new_a_k = kv[0][0].reshape(B, 1, T, 2 * H_A, HD_A)
    new_a_v = kv[0][1].reshape(B, 1, T, H_A, 2 * HD_A)
    new_b_k = kv[1][0].reshape(B, 1, T, H_B, HD_B)
    new_b_v = kv[1][1].reshape(B, 1, T, H_B, HD_B)
    return y_prompt, y_sample, new_a_k, new_a_v, new_b_k, new_b_v
```

```python
import functools
import math

import jax
import jax.numpy as jnp
import numpy as np
from jax import lax
from jax.experimental import pallas as pl
from jax.experimental.pallas import tpu as pltpu

D_MODEL = 1024
DEPTH = 2
GRID_W = 64
HD_A = 64
H_A = D_MODEL // (2 * HD_A)
H_B = 16
HD_B = D_MODEL // H_B
NA_KH = 8
NA_KW = 16
ROPE_BASE = 10000.0
N_GROUPS = 4
EXP_PER_GROUP = 8
N_EXPERTS = N_GROUPS * EXP_PER_GROUP
D_EXPERT = 512
LN_EPS = 1e-5
DEEPNORM_ALPHA = (2.0 * DEPTH) ** 0.25

F32 = jnp.float32
BF16 = jnp.bfloat16

V7X_LANES = 128
V7X_VMEM_BYTES = 64 * 1024 * 1024
VMEM_LIMIT = V7X_VMEM_BYTES * 7 // 8

TM = 256
TM_QKV = 512
TQ = 256
TQ_DIFF = 512
MOE_BM = 256
N_PAIRS = D_MODEL // V7X_LANES
NA_QROWS = TQ // GRID_W
NA_WIN_ROWS = NA_QROWS + NA_KH
NA_WIN = NA_WIN_ROWS * GRID_W
MASK_VALUE = -1e30
ROUTE_LANES = 128


def _params(sem):
    return pltpu.CompilerParams(dimension_semantics=sem, vmem_limit_bytes=VMEM_LIMIT)


def _layer_norm(y, g, b):
    mu = jnp.mean(y, -1, keepdims=True)
    d = y - mu
    var = jnp.mean(d * d, -1, keepdims=True)
    return d * lax.rsqrt(var + LN_EPS) * g + b


def _dot_nt(a, b):
    return lax.dot_general(a, b, (((1,), (1,)), ((), ())), preferred_element_type=F32)


def _ada_kernel(c_ref, w_ref, b_ref, o_ref):
    c = c_ref[...]
    a = c / (1.0 + jnp.exp(-c))
    o_ref[...] = jnp.dot(a, w_ref[...], preferred_element_type=F32,
                         precision=lax.Precision.HIGHEST) + b_ref[...]


def _adaln_all(cond, w_ada, b_ada):
    rows = cond.shape[0]
    tn = 1536
    n6 = 6 * D_MODEL
    return pl.pallas_call(
        _ada_kernel,
        out_shape=jax.ShapeDtypeStruct((DEPTH, rows, n6), F32),
        grid=(DEPTH, n6 // tn),
        in_specs=[
            pl.BlockSpec((rows, D_MODEL), lambda l, j: (0, 0)),
            pl.BlockSpec((None, D_MODEL, tn), lambda l, j: (l, 0, j)),
            pl.BlockSpec((None, 1, tn), lambda l, j: (l, 0, j)),
        ],
        out_specs=pl.BlockSpec((None, rows, tn), lambda l, j: (l, 0, j)),
        compiler_params=_params(("arbitrary", "arbitrary")),
        name="adaln",
    )(cond, w_ada, b_ada.reshape(DEPTH, 1, n6))


def _qkv_kernel(*refs, rope):
    if rope:
        x_ref, mod_ref, w_ref, cos_ref, sin_ref, q_ref, k_ref, v_ref = refs
    else:
        x_ref, mod_ref, w_ref, q_ref, k_ref, v_ref = refs
    h = (x_ref[...] * (1.0 + mod_ref[1:2, :]) + mod_ref[0:1, :]).astype(BF16)
    if rope:
        lane = lax.broadcasted_iota(jnp.int32, (x_ref.shape[0], V7X_LANES), 1)
        first = (lane & (HD_A // 2 - 1)) < (HD_A // 4)
        cos = cos_ref[...]
        sin = sin_ref[...]
    for idx, o_ref in enumerate((q_ref, k_ref, v_ref)):
        y = jnp.dot(h, w_ref[:, idx * D_MODEL:(idx + 1) * D_MODEL], preferred_element_type=F32)
        if rope and idx < 2:
            for c in range(N_PAIRS):
                sl = slice(c * V7X_LANES, (c + 1) * V7X_LANES)
                yc = y[:, sl]
                partner = jnp.where(first, pltpu.roll(yc, V7X_LANES - HD_A // 4, 1),
                                    pltpu.roll(yc, HD_A // 4, 1))
                o_ref[:, sl] = (yc * cos + partner * sin).astype(o_ref.dtype)
        else:
            o_ref[...] = y.astype(o_ref.dtype)


def _qkv(x, mods, w_bf16, mod_row, kv_dtype, rope_tables=None):
    B, T, _ = x.shape
    n = B * T
    tm = min(TM_QKV, T)
    tpb = T // tm
    rope = rope_tables is not None
    in_specs = [
        pl.BlockSpec((tm, D_MODEL), lambda i: (i, 0)),
        pl.BlockSpec((None, 6, D_MODEL), lambda i: (mod_row(i // tpb), 0, 0)),
        pl.BlockSpec((D_MODEL, 3 * D_MODEL), lambda i: (0, 0)),
    ]
    args = [x.reshape(n, D_MODEL), mods, w_bf16]
    if rope:
        in_specs += [pl.BlockSpec((tm, V7X_LANES), lambda i: (i % tpb, 0))] * 2
        args += list(rope_tables)
    q, k, v = pl.pallas_call(
        functools.partial(_qkv_kernel, rope=rope),
        out_shape=(jax.ShapeDtypeStruct((n, D_MODEL), BF16),
                   jax.ShapeDtypeStruct((n, D_MODEL), kv_dtype),
                   jax.ShapeDtypeStruct((n, D_MODEL), kv_dtype)),
        grid=(n // tm,),
        in_specs=in_specs,
        out_specs=(pl.BlockSpec((tm, D_MODEL), lambda i: (i, 0)),) * 3,
        compiler_params=_params(("arbitrary",)),
        name="qkv",
    )(*args)
    shp = (B, T, D_MODEL)
    return q.reshape(shp), k.reshape(shp), v.reshape(shp)


def _rope_tables(n_tok):
    t = np.arange(n_tok)
    pos = jnp.asarray(np.stack([t // GRID_W, t % GRID_W], -1), F32)
    quarter = HD_A // 4
    inv_freq = ROPE_BASE ** (-jnp.arange(quarter, dtype=F32) / quarter)
    ang = pos[:, :, None] * inv_freq
    cos, sin = jnp.cos(ang), jnp.sin(ang)
    cos_h = jnp.broadcast_to(cos[:, :, None, :], (n_tok, 2, 2, quarter)).reshape(n_tok, HD_A)
    sin_h = jnp.stack([-sin, sin], axis=2).reshape(n_tok, HD_A)
    reps = V7X_LANES // HD_A
    return jnp.tile(cos_h, (1, reps)), jnp.tile(sin_h, (1, reps))


def _pair_scores(qc, key_chunks, biases=None):
    lo = lax.broadcasted_iota(jnp.int32, qc.shape, 1) < V7X_LANES // 2
    zero = jnp.zeros_like(qc)
    qq = jnp.concatenate([jnp.where(lo, qc, zero), jnp.where(lo, zero, qc)], axis=0)
    scores = [_dot_nt(qq, kc) for kc in key_chunks]
    if biases is not None:
        scores = [s if b is None else s + b for s, b in zip(scores, biases)]
    return scores


def _pair_softmax_pv(scores, value_chunks):
    m = functools.reduce(jnp.maximum, [jnp.max(s, -1, keepdims=True) for s in scores])
    acc = None
    for s, vc in zip(scores, value_chunks):
        p = jnp.exp(s - m).astype(BF16)
        v1 = jnp.concatenate([vc, jnp.ones_like(vc)], axis=1)
        part = jnp.dot(p, v1, preferred_element_type=F32)
        acc = part if acc is None else acc + part
    return acc[:, :V7X_LANES] / acc[:, V7X_LANES:]


PAIR_LOOKAHEAD = 1


def _pipelined_pairs(n_chunks, scores_fn, finish_fn):
    ready = [scores_fn(c) for c in range(min(PAIR_LOOKAHEAD, n_chunks))]
    for c in range(n_chunks):
        if c + PAIR_LOOKAHEAD < n_chunks:
            ready.append(scores_fn(c + PAIR_LOOKAHEAD))
        finish_fn(c, ready.pop(0))


def _diff_attn_kernel(*refs, nseg, lam_init):
    lam_ref, subln_ref, q_ref = refs[:3]
    seg_refs = refs[3:3 + 2 * nseg]
    o_ref = refs[3 + 2 * nseg]
    lp = lam_ref[...]
    lam = (jnp.exp(jnp.sum(lp[0:1] * lp[1:2], keepdims=True))
           - jnp.exp(jnp.sum(lp[2:3] * lp[3:4], keepdims=True)) + lam_init)
    scale = HD_A ** -0.5
    def lanes(h):
        return slice(h * V7X_LANES, (h + 1) * V7X_LANES)

    def scores(h):
        return _pair_scores(q_ref[:, lanes(h)] * scale,
                            [seg_refs[2 * j][:, lanes(h)].astype(BF16) for j in range(nseg)])

    def finish(h, s):
        o2 = _pair_softmax_pv(s, [seg_refs[2 * j + 1][:, lanes(h)].astype(BF16)
                                  for j in range(nseg)])
        tq = o2.shape[0] // 2
        acc = o2[:tq] - lam * o2[tq:]
        ms = jnp.mean(acc * acc, -1, keepdims=True)
        o = acc * lax.rsqrt(ms + LN_EPS) * subln_ref[...] * (1.0 - lam_init)
        o_ref[:, lanes(h)] = o.astype(BF16)

    _pipelined_pairs(H_A, scores, finish)


def _diff_attention(q, segs, lam_p, subln, lam_init):
    B, T, _ = q.shape
    tq = min(TQ_DIFF, T)
    in_specs = [
        pl.BlockSpec((4, HD_A), lambda b, i: (0, 0)),
        pl.BlockSpec((1, 2 * HD_A), lambda b, i: (0, 0)),
        pl.BlockSpec((None, tq, D_MODEL), lambda b, i: (b, i, 0)),
    ]
    args = [lam_p, subln.reshape(1, 2 * HD_A), q]
    for k, v in segs:
        tk = k.shape[1]
        in_specs += [pl.BlockSpec((None, tk, D_MODEL), lambda b, i: (b, 0, 0))] * 2
        args += [k, v]
    return pl.pallas_call(
        functools.partial(_diff_attn_kernel, nseg=len(segs), lam_init=lam_init),
        out_shape=jax.ShapeDtypeStruct((B, T, D_MODEL), BF16),
        grid=(B, T // tq),
        in_specs=in_specs,
        out_specs=pl.BlockSpec((None, tq, D_MODEL), lambda b, i: (b, i, 0)),
        compiler_params=_params(("arbitrary", "arbitrary")),
        name="diff_attn",
    )(*args)


def _merge_pair(o2):
    tq = o2.shape[0] // 2
    lo = lax.broadcasted_iota(jnp.int32, (tq, V7X_LANES), 1) < V7X_LANES // 2
    return jnp.where(lo, o2[:tq], o2[tq:])


def _dense_attn_kernel(q_ref, k_ref, v_ref, o_ref):
    scale = HD_B ** -0.5
    for c in range(N_PAIRS):
        sl = slice(c * V7X_LANES, (c + 1) * V7X_LANES)
        s = _pair_scores(q_ref[:, sl] * scale, [k_ref[:, sl].astype(BF16)])
        o2 = _pair_softmax_pv(s, [v_ref[:, sl].astype(BF16)])
        o_ref[:, sl] = _merge_pair(o2).astype(BF16)


def _dense_attention(q, k, v):
    B, T, _ = q.shape
    spec = pl.BlockSpec((None, T, D_MODEL), lambda b: (b, 0, 0))
    return pl.pallas_call(
        _dense_attn_kernel,
        out_shape=jax.ShapeDtypeStruct((B, T, D_MODEL), BF16),
        grid=(B,),
        in_specs=[spec, spec, spec],
        out_specs=spec,
        compiler_params=_params(("arbitrary",)),
        name="dense_attn",
    )(q, k, v)


NA_WIN_BLOCKS = NA_WIN // TQ


def _na_attn_kernel(*refs):
    q_ref = refs[0]
    k_refs = refs[1:1 + NA_WIN_BLOCKS]
    v_refs = refs[1 + NA_WIN_BLOCKS:1 + 2 * NA_WIN_BLOCKS]
    kc_ref, vc_ref, bias_ref, o_ref = refs[1 + 2 * NA_WIN_BLOCKS:]
    scale = HD_B ** -0.5
    def lanes(c):
        return slice(c * V7X_LANES, (c + 1) * V7X_LANES)

    def scores(c):
        kwin = jnp.concatenate([r[:, lanes(c)] for r in k_refs], axis=0)
        bias = bias_ref[2 * c:2 * c + 2].reshape(2 * TQ, NA_WIN)
        return _pair_scores(q_ref[:, lanes(c)] * scale, [kwin, kc_ref[:, lanes(c)].astype(BF16)],
                            biases=[bias, None])

    def finish(c, s):
        vwin = jnp.concatenate([r[:, lanes(c)] for r in v_refs], axis=0)
        o2 = _pair_softmax_pv(s, [vwin, vc_ref[:, lanes(c)].astype(BF16)])
        o_ref[:, lanes(c)] = _merge_pair(o2).astype(BF16)

    _pipelined_pairs(N_PAIRS, scores, finish)


def _na_bias_tables(rpb, rows):
    kh = min(NA_KH, rows)
    n_heads = rpb.shape[0]
    qcol = np.arange(GRID_W)[:, None]
    kcol = np.arange(GRID_W)[None, :]
    c_start = np.clip(qcol - NA_KW // 2, 0, GRID_W - NA_KW)
    col_ok = (kcol >= c_start) & (kcol < c_start + NA_KW)
    col_off = np.clip(kcol - qcol + NA_KW - 1, 0, 2 * NA_KW - 2)
    onehot = (col_off[None] == np.arange(2 * NA_KW - 1)[:, None, None]).astype(np.float32)
    cols = jnp.einsum('hrc,cqk->hrqk', rpb, jnp.asarray(onehot), precision=lax.Precision.HIGHEST)
    cols = jnp.where(col_ok[None, None], cols, MASK_VALUE)
    padded = jnp.pad(cols, ((0, 0), (1, 1), (0, 0), (0, 0)), constant_values=MASK_VALUE)
    pairs = jnp.concatenate([padded[:, :-1], padded[:, 1:]], axis=-1)
    plan = []
    for r0 in (0, NA_QROWS, rows - NA_QROWS):
        ws = int(np.clip(r0 - NA_KH // 2, 0, rows - NA_WIN_ROWS))
        for qrow in range(r0, r0 + NA_QROWS):
            r_start = int(np.clip(qrow - kh // 2, 0, rows - kh))
            chunks = []
            for c in range(NA_WIN_ROWS // 2):
                krow = ws + 2 * c
                in_band = [r_start <= krow + s < r_start + kh for s in (0, 1)]
                chunks.append((krow - qrow + NA_KH - 1, in_band[0], in_band[1]))
            plan.append(chunks)
    return pl.pallas_call(
        functools.partial(_na_bias_kernel, plan=plan),
        out_shape=jax.ShapeDtypeStruct((3, n_heads, TQ, NA_WIN), F32),
        grid=(n_heads,),
        in_specs=[pl.BlockSpec((None, 2 * NA_KH, GRID_W, 2 * GRID_W), lambda h: (h, 0, 0, 0))],
        out_specs=pl.BlockSpec((3, None, TQ, NA_WIN), lambda h: (0, h, 0, 0)),
        compiler_params=_params(("arbitrary",)),
        name="na_bias",
    )(pairs)


def _na_bias_kernel(pairs_ref, o_ref, *, plan):
    left = lax.broadcasted_iota(jnp.int32, (GRID_W, 2 * GRID_W), 1) < GRID_W
    masked = jnp.full((GRID_W, 2 * GRID_W), MASK_VALUE, F32)
    for t, chunks in enumerate(plan):
        table, qrow = divmod(t, NA_QROWS)
        for c, (row_off, left_ok, right_ok) in enumerate(chunks):
            if left_ok or right_ok:
                piece = pairs_ref[row_off + 1]
                if not left_ok:
                    piece = jnp.where(left, masked, piece)
                if not right_ok:
                    piece = jnp.where(left, piece, masked)
            else:
                piece = masked
            o_ref[table, qrow * GRID_W:(qrow + 1) * GRID_W,
                  c * 2 * GRID_W:(c + 1) * 2 * GRID_W] = piece


def _neighborhood_attention(q, k, v, k_ctx, v_ctx, rpb):
    B, S, _ = q.shape
    P = k_ctx.shape[1]
    rows = S // GRID_W
    nqb = S // TQ
    assert rows >= NA_WIN_ROWS and nqb >= NA_WIN_BLOCKS
    bias = _na_bias_tables(rpb, rows)

    def bias_cfg(qb):
        return jnp.where(qb == 0, 0, jnp.where(qb == nqb - 1, 2, 1))

    def win_spec(j):
        return pl.BlockSpec((None, TQ, D_MODEL),
                            lambda qb, b: (b, jnp.clip(qb - 1, 0, nqb - NA_WIN_BLOCKS) + j, 0))

    win_specs = [win_spec(j) for j in range(NA_WIN_BLOCKS)]
    ctx_spec = pl.BlockSpec((None, P, D_MODEL), lambda qb, b: (b, 0, 0))
    q_spec = pl.BlockSpec((None, TQ, D_MODEL), lambda qb, b: (b, qb, 0))
    bias_spec = pl.BlockSpec((None, H_B, TQ, NA_WIN), lambda qb, b: (bias_cfg(qb), 0, 0, 0),
                             pipeline_mode=pl.Buffered(1))
    return pl.pallas_call(
        _na_attn_kernel,
        out_shape=jax.ShapeDtypeStruct((B, S, D_MODEL), BF16),
        grid=(nqb, B),
        in_specs=[q_spec] + win_specs + win_specs + [ctx_spec, ctx_spec, bias_spec],
        out_specs=q_spec,
        compiler_params=_params(("arbitrary", "arbitrary")),
        name="na_attn",
    )(q, *([k] * NA_WIN_BLOCKS), *([v] * NA_WIN_BLOCKS), k_ctx, v_ctx, bias)


INFO_E, INFO_W, INFO_RANK = 0, 2, 4


ROUTE_ROWS = 64


def _route(lt):
    sub = lax.broadcasted_iota(jnp.int32, lt.shape, 0).astype(F32)
    neg = -jnp.inf
    gmask = (sub >= N_EXPERTS) & (sub < N_EXPERTS + N_GROUPS)
    gl = jnp.where(gmask, lt, neg)
    gmax = jnp.max(gl, 0, keepdims=True)
    g_sel = jnp.min(jnp.where(gl == gmax, sub - N_EXPERTS, float(N_GROUPS)), 0, keepdims=True)
    p_g = 1.0 / jnp.sum(jnp.where(gmask, jnp.exp(gl - gmax), 0.0), 0, keepdims=True)
    emask = (sub >= g_sel * EXP_PER_GROUP) & (sub < (g_sel + 1.0) * EXP_PER_GROUP)
    el = jnp.where(emask, lt, neg)
    v1 = jnp.max(el, 0, keepdims=True)
    e1 = jnp.min(jnp.where(el == v1, sub, float(ROUTE_ROWS)), 0, keepdims=True)
    el2 = jnp.where(sub == e1, neg, el)
    v2 = jnp.max(el2, 0, keepdims=True)
    e2 = jnp.min(jnp.where(el2 == v2, sub, float(ROUTE_ROWS)), 0, keepdims=True)
    e21 = jnp.exp(v2 - v1)
    w1 = p_g / (1.0 + e21)
    w2 = p_g * e21 / (1.0 + e21)
    return sub, e1, e2, w1, w2


def _proj_route_kernel(o_ref, w_ref, x_ref, mod_ref, g_ref, b_ref, wr_ref, br_ref,
                       x1_ref, info_ref, cnt_ref, run):
    @pl.when(pl.program_id(0) == 0)
    def _():
        run[...] = jnp.zeros_like(run)

    out = jnp.dot(o_ref[...], w_ref[...], preferred_element_type=F32)
    x1 = _layer_norm(DEEPNORM_ALPHA * x_ref[...] + mod_ref[2:3, :] * out, g_ref[...], b_ref[...])
    x1_ref[...] = x1
    h2 = x1 * (1.0 + mod_ref[4:5, :]) + mod_ref[3:4, :]
    logits = jnp.dot(h2.astype(BF16), wr_ref[...], preferred_element_type=F32) + br_ref[...]
    sub, e1, e2, w1, w2 = _route(logits.T[:ROUTE_ROWS])

    oh1 = sub == e1
    oh2 = sub == e2
    oh = jnp.where(oh1 | oh2, 1.0, 0.0)
    t_row = lax.broadcasted_iota(jnp.int32, (TM, TM), 0)
    t_col = lax.broadcasted_iota(jnp.int32, (TM, TM), 1)
    earlier = jnp.where(t_row < t_col, 1.0, 0.0).astype(BF16)
    base = run[:, 0:1] + jnp.dot(oh.astype(BF16), earlier, preferred_element_type=F32)
    rank1 = jnp.sum(jnp.where(oh1, base, 0.0), 0, keepdims=True)
    rank2 = jnp.sum(jnp.where(oh2, base, 0.0), 0, keepdims=True)
    run[...] = run[...] + jnp.sum(oh, 1, keepdims=True)
    cnt_ref[...] = run[...]

    rec_row = lax.broadcasted_iota(jnp.int32, (ROUTE_LANES, TM), 0)
    rec = jnp.zeros((ROUTE_LANES, TM), F32)
    for row, val in ((INFO_E, e1), (INFO_E + 1, e2), (INFO_W, w1), (INFO_W + 1, w2),
                     (INFO_RANK, rank1), (INFO_RANK + 1, rank2)):
        rec = jnp.where(rec_row == row, val, rec)
    info_ref[...] = rec.T


def _proj_route(o, w_o_bf16, x, mods, mod_row, ln_g, ln_b, w_router, b_router):
    B, T, _ = x.shape
    n = B * T
    tpb = T // TM
    tile = pl.BlockSpec((TM, D_MODEL), lambda i: (i, 0))
    vec = pl.BlockSpec((1, D_MODEL), lambda i: (0, 0))
    return pl.pallas_call(
        _proj_route_kernel,
        out_shape=(jax.ShapeDtypeStruct((n, D_MODEL), F32),
                   jax.ShapeDtypeStruct((n, ROUTE_LANES), F32),
                   jax.ShapeDtypeStruct((ROUTE_ROWS, ROUTE_LANES), F32)),
        grid=(n // TM,),
        in_specs=[tile,
                  pl.BlockSpec((D_MODEL, D_MODEL), lambda i: (0, 0)),
                  tile,
                  pl.BlockSpec((None, 6, D_MODEL), lambda i: (mod_row(i // tpb), 0, 0)),
                  vec, vec,
                  pl.BlockSpec((D_MODEL, ROUTE_LANES), lambda i: (0, 0)),
                  pl.BlockSpec((1, ROUTE_LANES), lambda i: (0, 0))],
        out_specs=(tile,
                   pl.BlockSpec((TM, ROUTE_LANES), lambda i: (i, 0)),
                   pl.BlockSpec((ROUTE_ROWS, ROUTE_LANES), lambda i: (0, 0))),
        scratch_shapes=[pltpu.VMEM((ROUTE_ROWS, ROUTE_LANES), F32)],
        compiler_params=_params(("arbitrary",)),
        name="proj_route",
    )(o.reshape(n, D_MODEL), w_o_bf16, x.reshape(n, D_MODEL), mods,
      ln_g.reshape(1, D_MODEL), ln_b.reshape(1, D_MODEL), w_router, b_router)


def _route_tables(infos, counts, layer):
    cnts = [c[:N_EXPERTS, 0].astype(jnp.int32) for c in counts]
    cnt = functools.reduce(lambda a, b: a + b, cnts)
    padded = (cnt + MOE_BM - 1) // MOE_BM * MOE_BM
    pend = jnp.cumsum(padded)
    n_pairs = 2 * sum(info.shape[0] for info in infos)
    n_blocks = n_pairs // MOE_BM + N_EXPERTS
    n_used = pend[-1:] // MOE_BM
    experts = jnp.arange(N_EXPERTS, dtype=jnp.int32)
    pos = []
    first = pend - padded
    for info, c in zip(infos, cnts):
        e = info[:, INFO_E:INFO_E + 2].astype(jnp.int32)
        rank = info[:, INFO_RANK:INFO_RANK + 2].astype(jnp.int32)
        p = jnp.sum(jnp.where(e[..., None] == experts, first, 0), -1) + rank
        pos.append(p.reshape(-1).astype(jnp.int32))
        first = first + c
    blk = jnp.minimum(jnp.arange(n_blocks, dtype=jnp.int32), n_used - 1) * MOE_BM
    blk_e = jnp.sum((blk[:, None] >= pend[None, :]).astype(jnp.int32), axis=1) + layer * N_EXPERTS
    meta = jnp.concatenate([first, padded - cnt, n_used]).astype(jnp.int32)
    return pos, blk_e.astype(jnp.int32), meta, n_used, n_blocks


ROW_SLAB = D_MODEL // V7X_LANES


def _slab(row, n_rows=1):
    return pl.ds(pl.multiple_of(row * ROW_SLAB, ROW_SLAB), n_rows * ROW_SLAB)


def _store_rows(ref, x):
    rows = x.shape[0]
    for c in range(ROW_SLAB):
        ref[pl.ds(c, rows, stride=ROW_SLAB), :] = x[:, c * V7X_LANES:(c + 1) * V7X_LANES]


def _load_rows(ref, rows):
    return jnp.concatenate([ref[pl.ds(c, rows, stride=ROW_SLAB), :] for c in range(ROW_SLAB)], axis=1)


def _dispatch_kernel(pos_ref, meta_ref, x1a_ref, x1b_ref, mod_ref, xs_hbm, hbuf, zbuf, sem, sem_z, *,
                     n_blocks, tiles_a):
    i = pl.program_id(0)
    n_tiles = pl.num_programs(0)
    slot = i % 2

    @pl.when(i == 0)
    def _():
        zbuf[...] = jnp.zeros_like(zbuf)

    def wait_slot(s):
        for _ in range(2):
            pltpu.make_async_copy(hbuf.at[s], xs_hbm.at[pl.ds(0, TM * ROW_SLAB)], sem.at[s]).wait()

    @pl.when(i >= 2)
    def _():
        wait_slot(slot)

    @pl.when(i < tiles_a)
    def _():
        _store_rows(hbuf.at[slot], x1a_ref[...] * (1.0 + mod_ref[4:5, :]) + mod_ref[3:4, :])

    @pl.when(i >= tiles_a)
    def _():
        _store_rows(hbuf.at[slot], x1b_ref[...] * (1.0 + mod_ref[4:5, :]) + mod_ref[3:4, :])

    def scatter_row(r, carry):
        base = (i * TM + r) * 2
        src = hbuf.at[slot, _slab(r)]
        for k in range(2):
            pltpu.make_async_copy(src, xs_hbm.at[_slab(pos_ref[base + k])],
                                  sem.at[slot]).start(priority=k)
        return carry
    lax.fori_loop(0, TM, scatter_row, 0, unroll=8)

    @pl.when(i == n_tiles - 1)
    def _():
        def zero_copy(row, size):
            return pltpu.make_async_copy(zbuf.at[pl.ds(0, size * ROW_SLAB)],
                                         xs_hbm.at[_slab(row, size)], sem_z)

        def pad_expert(e, carry):
            start = meta_ref[e]
            length = meta_ref[N_EXPERTS + e]
            size = MOE_BM // 2
            while size >= 1:
                @pl.when((length & size) != 0)
                def _(size=size):
                    zero_copy(start + (length & ~(2 * size - 1)), size).start()
                size //= 2
            return carry
        lax.fori_loop(0, N_EXPERTS, pad_expert, 0)

        def tail_block(blk, carry):
            zero_copy(blk * MOE_BM, MOE_BM).start()
            return carry
        lax.fori_loop(meta_ref[2 * N_EXPERTS], n_blocks, tail_block, 0)

        wait_slot(slot)

        @pl.when(i >= 1)
        def _():
            wait_slot(1 - slot)

        for _ in range(N_EXPERTS):
            zero_copy(0, MOE_BM).wait()


def _dispatch(x1a, x1b, pos, meta, mods, mod_row_a, tpb_a, mod_row_b, tpb_b, n_blocks):
    tiles_a = x1a.shape[0] // TM
    tiles_b = x1b.shape[0] // TM
    assert n_blocks * MOE_BM - 2 * (tiles_a + tiles_b) * TM == N_EXPERTS * MOE_BM

    def mod_index(i, p, m):
        row = jnp.where(i < tiles_a, mod_row_a(i // tpb_a),
                        mod_row_b(jnp.maximum(i - tiles_a, 0) // tpb_b))
        return (row, 0, 0)

    grid_spec = pltpu.PrefetchScalarGridSpec(
        num_scalar_prefetch=2,
        grid=(tiles_a + tiles_b,),
        in_specs=[
            pl.BlockSpec((TM, D_MODEL), lambda i, p, m: (jnp.minimum(i, tiles_a - 1), 0)),
            pl.BlockSpec((TM, D_MODEL), lambda i, p, m: (jnp.maximum(i - tiles_a, 0), 0)),
            pl.BlockSpec((None, 6, D_MODEL), mod_index),
        ],
        out_specs=pl.BlockSpec(memory_space=pl.ANY),
        scratch_shapes=[
            pltpu.VMEM((2, TM * ROW_SLAB, V7X_LANES), F32),
            pltpu.VMEM((MOE_BM * ROW_SLAB, V7X_LANES), F32),
            pltpu.SemaphoreType.DMA((2,)),
            pltpu.SemaphoreType.DMA(()),
        ],
    )
    return pl.pallas_call(
        functools.partial(_dispatch_kernel, n_blocks=n_blocks, tiles_a=tiles_a),
        out_shape=jax.ShapeDtypeStruct((n_blocks * MOE_BM * ROW_SLAB, V7X_LANES), F32),
        grid_spec=grid_spec,
        compiler_params=_params(("arbitrary",)),
        name="dispatch",
    )(pos, meta, x1a, x1b, mods)


def _expert_kernel(be_ref, nused_ref, x_ref, wg_ref, wu_ref, wd_ref, o_ref, wg_s, wu_s, wd_s):
    i = pl.program_id(0)
    active = i < nused_ref[0]
    changed = (i == 0) | (be_ref[i] != be_ref[jnp.maximum(i - 1, 0)])

    @pl.when(active & changed)
    def _():
        wg_s[...] = wg_ref[...].astype(BF16)
        wu_s[...] = wu_ref[...].astype(BF16)
        wd_s[...] = wd_ref[...].astype(BF16)

    @pl.when(active)
    def _():
        xb = _load_rows(x_ref, MOE_BM).astype(BF16)
        g = jnp.dot(xb, wg_s[...], preferred_element_type=F32)
        u = jnp.dot(xb, wu_s[...], preferred_element_type=F32)
        hb = (g / (1.0 + jnp.exp(-g)) * u).astype(BF16)
        _store_rows(o_ref, jnp.dot(hb, wd_s[...], preferred_element_type=F32))

    @pl.when(jnp.logical_not(active))
    def _():
        o_ref[...] = jnp.zeros_like(o_ref)


def _experts(xs, blk_e, n_used, n_blocks, w_gate, w_up, w_down):
    w_gate = w_gate.reshape(DEPTH * N_EXPERTS, D_MODEL, D_EXPERT)
    w_up = w_up.reshape(DEPTH * N_EXPERTS, D_MODEL, D_EXPERT)
    w_down = w_down.reshape(DEPTH * N_EXPERTS, D_EXPERT, D_MODEL)
    rows_spec = pl.BlockSpec((MOE_BM * ROW_SLAB, V7X_LANES), lambda i, be, nu: (i, 0))
    grid_spec = pltpu.PrefetchScalarGridSpec(
        num_scalar_prefetch=2,
        grid=(n_blocks,),
        in_specs=[
            pl.BlockSpec((MOE_BM * ROW_SLAB, V7X_LANES),
                         lambda i, be, nu: (jnp.minimum(i, nu[0] - 1), 0)),
            pl.BlockSpec((None, D_MODEL, D_EXPERT), lambda i, be, nu: (be[i], 0, 0)),
            pl.BlockSpec((None, D_MODEL, D_EXPERT), lambda i, be, nu: (be[i], 0, 0)),
            pl.BlockSpec((None, D_EXPERT, D_MODEL), lambda i, be, nu: (be[i], 0, 0)),
        ],
        out_specs=rows_spec,
        scratch_shapes=[
            pltpu.VMEM((D_MODEL, D_EXPERT), BF16),
            pltpu.VMEM((D_MODEL, D_EXPERT), BF16),
            pltpu.VMEM((D_EXPERT, D_MODEL), BF16),
        ],
    )
    return pl.pallas_call(
        _expert_kernel,
        out_shape=jax.ShapeDtypeStruct(xs.shape, F32),
        grid_spec=grid_spec,
        compiler_params=_params(("arbitrary",)),
        name="experts",
    )(blk_e, n_used, xs, w_gate, w_up, w_down)


COMBINE_SLOTS = 3


def _combine_kernel(pos_ref, rows_hbm, x1_ref, route_ref, mod_ref, g_ref, b_ref, y_ref, buf, sem):
    i = pl.program_id(0)
    n_tiles = pl.num_programs(0)

    def issue_row(tile, slot, r):
        base = (tile * TM + r) * 2
        for k in range(2):
            pltpu.make_async_copy(rows_hbm.at[_slab(pos_ref[base + k])],
                                  buf.at[slot, k, _slab(r)], sem.at[slot]).start(priority=k)

    def wait_slot(s):
        for k in range(2):
            pltpu.make_async_copy(rows_hbm.at[pl.ds(0, TM * ROW_SLAB)], buf.at[s, k],
                                  sem.at[s]).wait()

    def issue(tile, slot):
        def body(r, carry):
            issue_row(tile, slot, r)
            return carry
        lax.fori_loop(0, TM, body, 0, unroll=8)

    @pl.when(i == 0)
    def _():
        issue(0, 0)
        issue(1, 1)

    slot = i % COMBINE_SLOTS
    wait_slot(slot)
    route = route_ref[...]
    moe = (route[:, INFO_W:INFO_W + 1] * _load_rows(buf.at[slot, 0], TM)
           + route[:, INFO_W + 1:INFO_W + 2] * _load_rows(buf.at[slot, 1], TM))
    y_ref[...] = _layer_norm(DEEPNORM_ALPHA * x1_ref[...] + mod_ref[5:6, :] * moe,
                             g_ref[...], b_ref[...])

    ahead = jnp.minimum(i + 2, n_tiles - 1)
    ahead_slot = (i + 2) % COMBINE_SLOTS
    for r in range(TM):
        issue_row(ahead, ahead_slot, r)

    @pl.when(i == n_tiles - 1)
    def _():
        wait_slot((i + 1) % COMBINE_SLOTS)
        wait_slot((i + 2) % COMBINE_SLOTS)


def _combine(rows, pos, x1, route, mods, mod_row, tpb, ln_g, ln_b):
    n = x1.shape[0]
    assert n // TM >= 2
    tile = lambda i, p: (i, 0)
    grid_spec = pltpu.PrefetchScalarGridSpec(
        num_scalar_prefetch=1,
        grid=(n // TM,),
        in_specs=[
            pl.BlockSpec(memory_space=pl.ANY),
            pl.BlockSpec((TM, D_MODEL), tile),
            pl.BlockSpec((TM, ROUTE_LANES), tile),
            pl.BlockSpec((None, 6, D_MODEL), lambda i, p: (mod_row(i // tpb), 0, 0)),
            pl.BlockSpec((1, D_MODEL), lambda i, p: (0, 0)),
            pl.BlockSpec((1, D_MODEL), lambda i, p: (0, 0)),
        ],
        out_specs=pl.BlockSpec((TM, D_MODEL), tile),
        scratch_shapes=[pltpu.VMEM((COMBINE_SLOTS, 2, TM * ROW_SLAB, V7X_LANES), F32),
                        pltpu.SemaphoreType.DMA((COMBINE_SLOTS,))],
    )
    return pl.pallas_call(
        _combine_kernel,
        out_shape=jax.ShapeDtypeStruct((n, D_MODEL), F32),
        grid_spec=grid_spec,
        compiler_params=_params(("arbitrary",)),
        name="combine",
    )(pos, rows, x1, route, mods, ln_g.reshape(1, D_MODEL), ln_b.reshape(1, D_MODEL))


def _diff_lambda_init(layer_idx):
    return 0.8 - 0.6 * math.exp(-0.3 * layer_idx)


def _router_weights(w_rg, b_rg, w_re, b_re):
    pad = ROUTE_LANES - N_EXPERTS - N_GROUPS
    w = jnp.concatenate([w_re, w_rg, jnp.zeros((D_MODEL, pad), F32)], axis=1)
    b = jnp.concatenate([b_re, b_rg, jnp.zeros((pad,), F32)]).reshape(1, ROUTE_LANES)
    return w.astype(BF16), b


def kernel(x_prompt, x_sample, cache_a_k, cache_a_v, cache_b_k, cache_b_v, c, c_ctx, w_ada, b_ada, ln_g, ln_b, w_qkv_a, lam_a, subln_a, w_o_a, w_qkv_b, rpb_b, w_o_b, w_rg, b_rg, w_re, b_re, w_gate, w_up, w_down):
    n_mod_rows = 16
    dec_batch = x_sample.shape[0]
    cond = jnp.zeros((n_mod_rows, D_MODEL), F32).at[0].set(c_ctx).at[1:1 + dec_batch].set(c)
    mods_all = _adaln_all(cond, w_ada, b_ada).reshape(DEPTH, n_mod_rows, 6, D_MODEL)
    rope_tables = _rope_tables(x_sample.shape[1])

    xs_in = [x_prompt, x_sample]
    mod_rows = [lambda b: 0, lambda b: 1 + b]
    kv = []
    for i in range(DEPTH):
        mods = mods_all[i]
        j = i // 2
        diff_layer = i % 2 == 0
        if diff_layer:
            w_qkv, w_o = w_qkv_a[j].astype(BF16), w_o_a[j].astype(BF16)
            lam_init = _diff_lambda_init(i)
        else:
            w_qkv, w_o = w_qkv_b[j].astype(BF16), w_o_b[j].astype(BF16)
        w_router, b_router = _router_weights(w_rg[i], b_rg[i], w_re[i], b_re[i])
        routed = []
        for x, mod_row, latent in zip(xs_in, mod_rows, (False, True)):
            B = x.shape[0]
            if diff_layer and latent:
                q, k, v = _qkv(x, mods, w_qkv, mod_row, BF16, rope_tables)
                segs = [(cache_a_k[:, j].reshape(B, -1, D_MODEL),
                         cache_a_v[:, j].reshape(B, -1, D_MODEL)), (k, v)]
                o = _diff_attention(q, segs, lam_a[j], subln_a[j], lam_init)
            elif diff_layer:
                q, k, v = _qkv(x, mods, w_qkv, mod_row, F32)
                kv.append((k, v))
                o = _diff_attention(q, [(k, v)], lam_a[j], subln_a[j], lam_init)
            elif latent:
                q, k, v = _qkv(x, mods, w_qkv, mod_row, BF16)
                o = _neighborhood_attention(q, k, v, cache_b_k[:, j].reshape(B, -1, D_MODEL),
                                            cache_b_v[:, j].reshape(B, -1, D_MODEL), rpb_b[j])
            else:
                q, k, v = _qkv(x, mods, w_qkv, mod_row, F32)
                kv.append((k, v))
                o = _dense_attention(q, k, v)
            routed.append(_proj_route(o, w_o, x, mods, mod_row, ln_g[i, 0], ln_b[i, 0],
                                      w_router, b_router))
        x1s, infos, counts = zip(*routed)
        pos, blk_e, meta, n_used, n_blocks = _route_tables(infos, counts, i)
        tpbs = [x.shape[1] // TM for x in xs_in]
        rows_in = _dispatch(x1s[0], x1s[1], jnp.concatenate(pos), meta, mods,
                            mod_rows[0], tpbs[0], mod_rows[1], tpbs[1], n_blocks)
        rows = _experts(rows_in, blk_e, n_used, n_blocks, w_gate, w_up, w_down)
        xs_in = [_combine(rows, p, x1, info, mods, mod_row, tpb, ln_g[i, 1], ln_b[i, 1]).reshape(x.shape)
                 for p, x1, info, mod_row, tpb, x in zip(pos, x1s, infos, mod_rows, tpbs, xs_in)]
    y_prompt, y_sample = xs_in

    B, T, _ = x_prompt.shape
    new_a_k = kv[0][0].reshape(B, 1, T, 2 * H_A, HD_A)
    new_a_v = kv[0][1].reshape(B, 1, T, H_A, 2 * HD_A)
    new_b_k = kv[1][0].reshape(B, 1, T, H_B, HD_B)
    new_b_v = kv[1][1].reshape(B, 1, T, H_B, HD_B)
    return y_prompt, y_sample, new_a_k, new_a_v, new_b_k, new_b_v
```
